```python
import math
import jax, jax.numpy as jnp
from jax import lax
import numpy as np

D_MODEL = 1024
BATCH = 8
SEQ = 4096
DEPTH = 2

HEAD_DIM = 64
N_HEADS_MOBA = 6
N_HEADS_MLA = 4
N_HEADS_FOX = 6
MOBA_BLOCK = 256
MOBA_TOPK = 3
MOBA_Q_BLOCK = 32
Q_BLOCK = 128
MLA_Q_RANK = 256
MLA_KV_RANK = 128
MLA_NOPE_DIM = 64
MLA_ROPE_DIM = 32
MLA_V_DIM = 64
MLA_QK_DIM = MLA_NOPE_DIM + MLA_ROPE_DIM
ROPE_THETA = 10000.0
T5_BUCKETS = 32
T5_MAX_DIST = 1024
D_FF = 2816
CONV_WIDTH = 3
NORM_EPS = 1e-6
NEG_INF = -1e30

W_MOBA = N_HEADS_MOBA * HEAD_DIM
W_MLA = N_HEADS_MLA * MLA_V_DIM
W_FOX = N_HEADS_FOX * HEAD_DIM
MIX_WIDTH = W_MOBA + W_MLA + W_FOX
PROJ_SIZES = (W_MOBA, W_MOBA, W_MOBA, MLA_Q_RANK, MLA_KV_RANK, MLA_ROPE_DIM, W_FOX, W_FOX, W_FOX, N_HEADS_FOX)
PROJ_COLS = sum(PROJ_SIZES)

kernel_name = "hymba_moba_mla_fox_convffn"


def rms_norm(x, g):
    xf = x.astype(jnp.float32)
    y = xf * lax.rsqrt(jnp.mean(xf * xf, axis=-1, keepdims=True) + NORM_EPS)
    return (y * g.astype(jnp.float32)).astype(x.dtype)


def split_heads(t, n, d):
    b, s, _ = t.shape
    return t.reshape(b, s, n, d).transpose(0, 2, 1, 3)


def merge_heads(t):
    b, h, s, d = t.shape
    return t.transpose(0, 2, 1, 3).reshape(b, s, h * d)


def rotary(t, pos):
    half = t.shape[-1] // 2
    inv = ROPE_THETA ** (-jnp.arange(half, dtype=jnp.float32) / half)
    ang = pos.astype(jnp.float32)[:, None] * inv[None, :]
    cos, sin = jnp.cos(ang), jnp.sin(ang)
    t1 = t[..., :half].astype(jnp.float32)
    t2 = t[..., half:].astype(jnp.float32)
    return jnp.concatenate([t1 * cos - t2 * sin, t1 * sin + t2 * cos], axis=-1).astype(t.dtype)


def t5_bucket(rel):
    n = jnp.maximum(rel, 0)
    exact = T5_BUCKETS // 2
    large = exact + (jnp.log(jnp.maximum(n, 1).astype(jnp.float32) / exact)
                     / math.log(T5_MAX_DIST / exact) * (T5_BUCKETS - exact)).astype(jnp.int32)
    return jnp.where(n < exact, n, jnp.minimum(large, T5_BUCKETS - 1))


def moba_attention(q, k, v, rel_bias):
    b, h, s, dh = q.shape
    nb = -(-s // MOBA_BLOCK)
    pad = nb * MOBA_BLOCK - s
    kp = jnp.pad(k, ((0, 0), (0, 0), (0, pad), (0, 0)))
    vp = jnp.pad(v, ((0, 0), (0, 0), (0, pad), (0, 0)))
    k_blocks = kp.reshape(b, h, nb, MOBA_BLOCK, dh)
    v_blocks = vp.reshape(b, h, nb, MOBA_BLOCK, dh)
    k_mean = jnp.mean(k_blocks, axis=3)
    topk = min(MOBA_TOPK, nb)
    bias_tab = rel_bias.T
    scale = dh ** -0.5
    b_idx = jnp.arange(b)[:, None, None, None]
    h_idx = jnp.arange(h)[None, :, None, None]
    blk_pos = jnp.arange(MOBA_BLOCK)

    def chunk(c):
        q_start = c * MOBA_Q_BLOCK
        qc = lax.dynamic_slice_in_dim(q, q_start, MOBA_Q_BLOCK, axis=2)
        q_pos = q_start + jnp.arange(MOBA_Q_BLOCK)
        own = q_start // MOBA_BLOCK
        gate = jnp.einsum('bhqd,bhnd->bhqn', qc, k_mean).astype(jnp.float32)
        gate = jnp.where(jnp.arange(nb) < own, gate, NEG_INF)
        _, sel = lax.top_k(gate, topk)
        valid = jnp.arange(topk) < own
        k_sel = k_blocks[b_idx, h_idx, sel]
        v_sel = v_blocks[b_idx, h_idx, sel]
        k_pos_sel = sel[..., None] * MOBA_BLOCK + blk_pos
        s_sel = jnp.einsum('bhqd,bhqnkd->bhqnk', qc, k_sel).astype(jnp.float32) * scale
        s_sel = s_sel + bias_tab[h_idx[..., None], t5_bucket(q_pos[:, None, None] - k_pos_sel)]
        s_sel = jnp.where(valid[:, None], s_sel, NEG_INF)
        s_sel = s_sel.reshape(b, h, MOBA_Q_BLOCK, topk * MOBA_BLOCK)
        own_start = own * MOBA_BLOCK
        k_own = lax.dynamic_slice_in_dim(kp, own_start, MOBA_BLOCK, axis=2)
        v_own = lax.dynamic_slice_in_dim(vp, own_start, MOBA_BLOCK, axis=2)
        rel = q_pos[:, None] - (own_start + blk_pos)[None, :]
        s_own = jnp.einsum('bhqd,bhkd->bhqk', qc, k_own).astype(jnp.float32) * scale
        s_own = jnp.where(rel >= 0, s_own + bias_tab[:, t5_bucket(rel)], NEG_INF)
        p = jax.nn.softmax(jnp.concatenate([s_sel, s_own], axis=-1), axis=-1).astype(v.dtype)
        p_sel = p[..., :topk * MOBA_BLOCK].reshape(b, h, MOBA_Q_BLOCK, topk, MOBA_BLOCK)
        p_own = p[..., topk * MOBA_BLOCK:]
        return (jnp.einsum('bhqnk,bhqnkd->bhqd', p_sel, v_sel)
                + jnp.einsum('bhqk,bhkd->bhqd', p_own, v_own))

    outs = lax.map(chunk, jnp.arange(s // MOBA_Q_BLOCK))
    return outs.transpose(1, 2, 0, 3, 4).reshape(b, h, s, dh)


def causal_block_attention(q, k, v, scale, decay=None):
    b, h, s, _ = q.shape
    dv = v.shape[-1]
    k_pos = jnp.arange(s)

    def chunk(c):
        q_start = c * Q_BLOCK
        qc = lax.dynamic_slice_in_dim(q, q_start, Q_BLOCK, axis=2)
        q_pos = q_start + jnp.arange(Q_BLOCK)
        sc = jnp.einsum('bhqd,bhkd->bhqk', qc, k).astype(jnp.float32) * scale
        if decay is not None:
            dq = lax.dynamic_slice_in_dim(decay, q_start, Q_BLOCK, axis=2)
            sc = sc + dq[..., None] - decay[:, :, None, :]
        sc = jnp.where(k_pos[None, :] <= q_pos[:, None], sc, NEG_INF)
        p = jax.nn.softmax(sc, axis=-1).astype(v.dtype)
        return jnp.einsum('bhqk,bhkd->bhqd', p, v)

    outs = lax.map(chunk, jnp.arange(s // Q_BLOCK))
    return outs.transpose(1, 2, 0, 3, 4).reshape(b, h, s, dv)


def hybrid_mixer(h, rel_bias, w_in, b_f, q_norm, kv_norm, w_uq, w_ukv, w_o):
    b, s, _ = h.shape
    pos = jnp.arange(s)
    proj = h @ w_in
    (a_q, a_k, a_v, c_q, c_kv, k_r, f_q, f_k, f_v, f_g) = jnp.split(
        proj, np.cumsum(PROJ_SIZES)[:-1].tolist(), axis=-1)

    out_a = moba_attention(split_heads(a_q, N_HEADS_MOBA, HEAD_DIM),
                           split_heads(a_k, N_HEADS_MOBA, HEAD_DIM),
                           split_heads(a_v, N_HEADS_MOBA, HEAD_DIM), rel_bias)

    qb = split_heads(rms_norm(c_q, q_norm) @ w_uq, N_HEADS_MLA, MLA_QK_DIM)
    kv = split_heads(rms_norm(c_kv, kv_norm) @ w_ukv, N_HEADS_MLA, MLA_NOPE_DIM + MLA_V_DIM)
    k_nope, v_b = kv[..., :MLA_NOPE_DIM], kv[..., MLA_NOPE_DIM:]
    q_b = jnp.concatenate([qb[..., :MLA_NOPE_DIM], rotary(qb[..., MLA_NOPE_DIM:], pos)], axis=-1)
    k_rope = jnp.broadcast_to(rotary(k_r[:, None], pos), (b, N_HEADS_MLA, s, MLA_ROPE_DIM))
    k_b = jnp.concatenate([k_nope, k_rope], axis=-1)
    out_b = causal_block_attention(q_b, k_b, v_b, MLA_QK_DIM ** -0.5)

    log_f = jax.nn.log_sigmoid(f_g.astype(jnp.float32) + b_f.astype(jnp.float32))
    decay = jnp.cumsum(log_f, axis=1).transpose(0, 2, 1)
    out_c = causal_block_attention(split_heads(f_q, N_HEADS_FOX, HEAD_DIM),
                                   split_heads(f_k, N_HEADS_FOX, HEAD_DIM),
                                   split_heads(f_v, N_HEADS_FOX, HEAD_DIM),
                                   HEAD_DIM ** -0.5, decay)

    merged = jnp.concatenate([merge_heads(out_a), merge_heads(out_b), merge_heads(out_c)], axis=-1)
    return merged @ w_o


def conv_ffn(h, w_up, conv_w, conv_b, w_down):
    u = h @ w_up
    u = lax.conv_general_dilated(u, conv_w[:, None, :], window_strides=(1,),
                                 padding=[(CONV_WIDTH - 1, 0)],
                                 dimension_numbers=('NWC', 'WIO', 'NWC'),
                                 feature_group_count=u.shape[-1]) + conv_b
    g, val = jnp.split(u, 2, axis=-1)
    return (jax.nn.gelu(g, approximate=True) * val) @ w_down


def setup_inputs(seed: int = 0) -> dict:
    key = jax.random.key(seed)
    ks = jax.random.split(key, 20)
    f32 = jnp.float32

    def nrm(k, shape, fan_in):
        return jax.random.normal(k, shape, f32) * (fan_in ** -0.5)

    def gain(k, shape):
        return 1.0 + 0.1 * jax.random.normal(k, shape, f32)

    L = DEPTH
    return {
        "x": jax.random.normal(ks[0], (BATCH, SEQ, D_MODEL), f32),
        "rel_bias": 0.5 * jax.random.normal(ks[1], (T5_BUCKETS, N_HEADS_MOBA), f32),
        "ln_mix_pre": gain(ks[2], (L, D_MODEL)),
        "ln_mix_post": gain(ks[3], (L, D_MODEL)),
        "ln_ffn_pre": gain(ks[4], (L, D_MODEL)),
        "ln_ffn_post": gain(ks[5], (L, D_MODEL)),
        "w_in": nrm(ks[6], (L, D_MODEL, PROJ_COLS), D_MODEL),
        "b_f": 2.0 + 0.5 * jax.random.normal(ks[7], (L, N_HEADS_FOX), f32),
        "q_norm": gain(ks[8], (L, MLA_Q_RANK)),
        "kv_norm": gain(ks[9], (L, MLA_KV_RANK)),
        "w_uq": nrm(ks[10], (L, MLA_Q_RANK, N_HEADS_MLA * MLA_QK_DIM), MLA_Q_RANK),
        "w_ukv": nrm(ks[11], (L, MLA_KV_RANK, N_HEADS_MLA * (MLA_NOPE_DIM + MLA_V_DIM)), MLA_KV_RANK),
        "w_o": nrm(ks[12], (L, MIX_WIDTH, D_MODEL), MIX_WIDTH),
        "w_up": nrm(ks[13], (L, D_MODEL, 2 * D_FF), D_MODEL),
        "conv_w": nrm(ks[14], (L, CONV_WIDTH, 2 * D_FF), CONV_WIDTH),
        "conv_b": 0.01 * jax.random.normal(ks[15], (L, 2 * D_FF), f32),
        "w_down": nrm(ks[16], (L, D_FF, D_MODEL), D_FF),
    }


def reference(x, rel_bias, ln_mix_pre, ln_mix_post, ln_ffn_pre, ln_ffn_post, w_in, b_f,
              q_norm, kv_norm, w_uq, w_ukv, w_o, w_up, conv_w, conv_b, w_down):
    for l in range(DEPTH):
        h = rms_norm(x, ln_mix_pre[l])
        x = x + rms_norm(hybrid_mixer(h, rel_bias, w_in[l], b_f[l], q_norm[l], kv_norm[l],
                                      w_uq[l], w_ukv[l], w_o[l]), ln_mix_post[l])
        h = rms_norm(x, ln_ffn_pre[l])
        x = x + rms_norm(conv_ffn(h, w_up[l], conv_w[l], conv_b[l], w_down[l]), ln_ffn_post[l])
    return x
```

```python
import functools
import math

import jax
import jax.numpy as jnp
from jax import lax
from jax.experimental import pallas as pl
from jax.experimental.pallas import tpu as pltpu

F32 = jnp.float32
BF16 = jnp.bfloat16

HEAD_DIM = 64
N_HEADS_MOBA = 6
N_HEADS_MLA = 4
N_HEADS_FOX = 6
MOBA_BLOCK = 256
MOBA_TOPK = 3
MLA_Q_RANK = 256
MLA_KV_RANK = 128
MLA_NOPE_DIM = 64
MLA_ROPE_DIM = 32
MLA_V_DIM = 64
MLA_QK_DIM = MLA_NOPE_DIM + MLA_ROPE_DIM
ROPE_THETA = 10000.0
T5_BUCKETS = 32
T5_MAX_DIST = 1024
D_FF = 2816
CONV_WIDTH = 3
NORM_EPS = 1e-6
NEG_INF = -1e30
W_MOBA = N_HEADS_MOBA * HEAD_DIM
W_MLA = N_HEADS_MLA * MLA_V_DIM
W_FOX = N_HEADS_FOX * HEAD_DIM
PROJ_SIZES = (W_MOBA, W_MOBA, W_MOBA, MLA_Q_RANK, MLA_KV_RANK, MLA_ROPE_DIM, W_FOX, W_FOX, W_FOX, N_HEADS_FOX)

LANES = 128
SUBLANES = 8
VMEM_LIMIT = 56 * 1024 * 1024

ATT_TILE = MOBA_BLOCK
PROJ_ROWS = 512
OPROJ_ROWS = 512
FFN_ROWS = 1024
FFN_COLS = 256
FFN_HALO = 16
FEAT0 = HEAD_DIM
NBLK_PAD = 16
N_NEAR = (T5_MAX_DIST + ATT_TILE - 1 + ATT_TILE - 1) // ATT_TILE

_TOK_AK = (0, 384)
_TOK_FK = (384, 768)
_TOK_CQ = (768, 1024)
_TOK_CKV = (1024, 1152)
_TOK_KR = (1152, 1280)
_TOK_KRR = (1280, 1408)
_TOK_FG = (1408, 1536)
_TOK_COLS = 1536
_T_AQ = (0, 384)
_T_FQ = (384, 768)
_T_AV = (768, 1152)
_T_FV = (1152, 1536)


def _rms(xf, g):
    return xf * lax.rsqrt(jnp.mean(xf * xf, axis=-1, keepdims=True) + NORM_EPS) * g


def _dot(a, b):
    return jnp.dot(a, b, preferred_element_type=F32)


def _dot_nt(a, b):
    return lax.dot_general(a, b, (((1,), (1,)), ((), ())), preferred_element_type=F32)


def _bf16_pieces(x):
    hi = x.astype(BF16).astype(F32)
    r = x - hi
    mid = r.astype(BF16).astype(F32)
    lo = (r - mid).astype(BF16).astype(F32)
    return hi, mid, lo


def _dot_3pass(a, b):
    a_hi = a.astype(BF16)
    a_lo = (a - a_hi.astype(F32)).astype(BF16)
    b_hi = b.astype(BF16)
    b_lo = (b - b_hi.astype(F32)).astype(BF16)
    return _dot(a_hi, b_hi) + (_dot(a_hi, b_lo) + _dot(a_lo, b_hi))


def _head_slot(arr, hd):
    pair = arr[:, (hd // 2) * LANES:(hd // 2 + 1) * LANES]
    if hd % 2:
        pair = pltpu.roll(pair, HEAD_DIM, axis=1)
    return pair


def _proj_kernel(x_ref, g_ref, wtok_ref, wt_ref, bf_ref, qn_ref, kvn_ref, wuq_ref, wuqr_ref,
                 wukvk_ref, wukvv_ref, ctok_ref, stok_ref, ct_ref, st_ref, far_ref,
                 qat_ref, ka_ref, vat_ref, qft_ref, kf_ref, vft_ref, qmt_ref, km_ref, vmt_ref,
                 kmean_ref, fcarry_ref, *, tm):
    t = pl.program_id(1)
    n_sub = tm // ATT_TILE

    @pl.when(t == 0)
    def _():
        kmean_ref[...] = jnp.zeros_like(kmean_ref)
        fcarry_ref[...] = jnp.zeros_like(fcarry_ref)

    h = _rms(x_ref[0], g_ref[...]).astype(BF16)
    lane = lax.broadcasted_iota(jnp.int32, (tm, LANES), 1)
    row = lax.broadcasted_iota(jnp.int32, (tm, LANES), 0)
    blk0 = t * n_sub
    is_head = lane < HEAD_DIM

    def w_tok(rng):
        return wtok_ref[:, rng[0]:rng[1]]

    def w_t(rng):
        return wt_ref[rng[0]:rng[1], :]

    def store_vt(ref, vt):
        for ti in range(n_sub):
            ref[0, ti] = vt[:, ti * ATT_TILE:(ti + 1) * ATT_TILE].astype(BF16)

    ak = _dot(h, w_tok(_TOK_AK))
    blk_row = blk0 + lax.shift_right_logical(row, int(math.log2(MOBA_BLOCK)))
    k_feat = jnp.where(lane == FEAT0 + blk_row, 1.0, 0.0)
    for hd in range(N_HEADS_MOBA):
        ka_ref[0, :, hd * LANES:(hd + 1) * LANES] = jnp.where(is_head, _head_slot(ak, hd), k_feat).astype(BF16)

    lane_k = lax.broadcasted_iota(jnp.int32, (1, W_MOBA), 1)
    for bi in range(n_sub):
        mean_row = jnp.mean(ak[bi * MOBA_BLOCK:(bi + 1) * MOBA_BLOCK, :], axis=0, keepdims=True)
        for hd in range(N_HEADS_MOBA):
            in_head = (lane_k >= hd * HEAD_DIM) & (lane_k < (hd + 1) * HEAD_DIM)
            kmean_ref[pl.ds(hd * NBLK_PAD + blk0 + bi, 1), :] = jnp.where(in_head, mean_row, 0.0)

    qat = _dot_nt(w_t(_T_AQ), h) * (HEAD_DIM ** -0.5)
    gate = _dot_3pass(kmean_ref[...], qat)
    own = blk0 + lax.shift_right_logical(
        lax.broadcasted_iota(jnp.int32, (NBLK_PAD, tm), 1), int(math.log2(MOBA_BLOCK)))
    n_io = lax.broadcasted_iota(jnp.int32, (NBLK_PAD, tm), 0)
    past = n_io < own
    zeros_tail = jnp.zeros((LANES - HEAD_DIM - NBLK_PAD, tm), F32)
    for hd in range(N_HEADS_MOBA):
        g = jnp.where(past, gate[hd * NBLK_PAD:(hd + 1) * NBLK_PAD, :], NEG_INF)
        rank = jnp.zeros((NBLK_PAD, tm), jnp.int32)
        for n2 in range(NBLK_PAD):
            r = g[n2:n2 + 1, :]
            tie = jnp.where(n2 < n_io, 1, 0)
            rank = rank + jnp.where(r > g, 1, jnp.where(r == g, tie, 0))
        sel = jnp.where(past, rank, MOBA_TOPK) < MOBA_TOPK
        far = jnp.where(own - n_io >= N_NEAR, far_ref[hd], 0.0)
        q_feat = jnp.where(sel | (n_io == own), far, NEG_INF)
        qat_ref[0, hd * LANES:(hd + 1) * LANES, :] = jnp.concatenate(
            [qat[hd * HEAD_DIM:(hd + 1) * HEAD_DIM, :], q_feat, zeros_tail], axis=0).astype(BF16)
    store_vt(vat_ref, _dot_nt(w_t(_T_AV), h))

    fg = _dot(h, w_tok(_TOK_FG)) + bf_ref[...]
    logf = jnp.minimum(fg, 0.0) - jnp.log1p(jnp.exp(-jnp.abs(fg)))
    csum = jnp.where(lane < N_HEADS_FOX, logf, 0.0)
    sft = 1
    while sft < tm:
        csum = csum + jnp.where(row >= sft, pltpu.roll(csum, sft, axis=0), 0.0)
        sft *= 2
    decay = csum + fcarry_ref[0:1, :]
    fcarry_ref[0:1, :] = decay[tm - 1:tm, :]
    decay_t = decay.T

    fk = _dot(h, w_tok(_TOK_FK))
    fqt = _dot_nt(w_t(_T_FQ), h) * (HEAD_DIM ** -0.5)
    sub8 = lax.broadcasted_iota(jnp.int32, (SUBLANES, tm), 0)
    zeros_tail_f = jnp.zeros((LANES - HEAD_DIM - SUBLANES, tm), F32)
    for hd in range(N_HEADS_FOX):
        fcol = jnp.broadcast_to(decay[:, hd:hd + 1], (tm, LANES))
        hi, mid, lo = _bf16_pieces(fcol)
        k_feat_f = jnp.where(lane < FEAT0 + 3, 1.0,
                             jnp.where(lane == FEAT0 + 3, -hi,
                                       jnp.where(lane == FEAT0 + 4, -mid,
                                                 jnp.where(lane == FEAT0 + 5, -lo, 0.0))))
        kf_ref[0, :, hd * LANES:(hd + 1) * LANES] = jnp.where(is_head, _head_slot(fk, hd), k_feat_f).astype(BF16)
        hi, mid, lo = _bf16_pieces(decay_t[hd:hd + 1, :])
        q_feat_f = jnp.where(sub8 == 0, hi,
                             jnp.where(sub8 == 1, mid,
                                       jnp.where(sub8 == 2, lo,
                                                 jnp.where(sub8 < 6, 1.0, 0.0))))
        qft_ref[0, hd * LANES:(hd + 1) * LANES, :] = jnp.concatenate(
            [fqt[hd * HEAD_DIM:(hd + 1) * HEAD_DIM, :], q_feat_f, zeros_tail_f], axis=0).astype(BF16)
    store_vt(vft_ref, _dot_nt(w_t(_T_FV), h))

    cqn = _rms(_dot(h, w_tok(_TOK_CQ)), qn_ref[...]).astype(BF16)
    qm = _dot_nt(wuq_ref[...], cqn)
    qmr = _dot_nt(wuqr_ref[...], cqn)
    kvn = _rms(_dot(h, w_tok(_TOK_CKV)), kvn_ref[...]).astype(BF16)
    k_nope = _dot(kvn, wukvk_ref[...])
    k_rope = _dot(h, w_tok(_TOK_KR)) * ctok_ref[...] + _dot(h, w_tok(_TOK_KRR)) * stok_ref[...]
    cos_t = ct_ref[...]
    sin_t = st_ref[...]
    for hd in range(N_HEADS_MLA):
        sl = slice(hd * LANES, (hd + 1) * LANES)
        qmt_ref[0, sl, :] = ((qm[sl, :] * cos_t + qmr[sl, :] * sin_t) * (MLA_QK_DIM ** -0.5)).astype(BF16)
        km_ref[0, :, sl] = (k_nope[:, sl] + k_rope).astype(BF16)
    store_vt(vmt_ref, _dot_nt(wukvv_ref[...], kvn))


def _proj_call(x, g, p, tables, far, tm):
    b, s, d = x.shape
    nt = s // tm
    n_sub = tm // ATT_TILE
    nq = s // ATT_TILE

    def full(a):
        return pl.BlockSpec(a.shape, lambda bi, ti: (0,) * a.ndim)

    ctok, stok, ct, st = tables
    in_specs = [
        pl.BlockSpec((1, tm, d), lambda bi, ti: (bi, ti, 0)),
        full(g), full(p["w_tok"]), full(p["w_t"]), full(p["b_f"]), full(p["q_norm"]), full(p["kv_norm"]),
        full(p["wuq_t"]), full(p["wuq_rot_t"]), full(p["wukv_k"]), full(p["wukv_v_t"]),
        pl.BlockSpec((tm, LANES), lambda bi, ti: (ti, 0)),
        pl.BlockSpec((tm, LANES), lambda bi, ti: (ti, 0)),
        pl.BlockSpec((LANES, tm), lambda bi, ti: (0, ti)),
        pl.BlockSpec((LANES, tm), lambda bi, ti: (0, ti)),
        pl.BlockSpec(memory_space=pltpu.SMEM),
    ]

    def qt_spec(n_heads):
        return pl.BlockSpec((1, n_heads * LANES, tm), lambda bi, ti: (bi, 0, ti))

    def k_spec(n_heads):
        return pl.BlockSpec((1, tm, n_heads * LANES), lambda bi, ti: (bi, ti, 0))

    def vt_spec(width):
        return pl.BlockSpec((1, n_sub, width, ATT_TILE), lambda bi, ti: (bi, ti, 0, 0))

    def qt_shape(n_heads):
        return jax.ShapeDtypeStruct((b, n_heads * LANES, s), BF16)

    def k_shape(n_heads):
        return jax.ShapeDtypeStruct((b, s, n_heads * LANES), BF16)

    def vt_shape(width):
        return jax.ShapeDtypeStruct((b, nq, width, ATT_TILE), BF16)

    out_specs = [qt_spec(6), k_spec(6), vt_spec(W_MOBA), qt_spec(6), k_spec(6), vt_spec(W_FOX),
                 qt_spec(4), k_spec(4), vt_spec(W_MLA)]
    out_shape = [qt_shape(6), k_shape(6), vt_shape(W_MOBA), qt_shape(6), k_shape(6), vt_shape(W_FOX),
                 qt_shape(4), k_shape(4), vt_shape(W_MLA)]
    return pl.pallas_call(
        functools.partial(_proj_kernel, tm=tm),
        grid=(b, nt),
        in_specs=in_specs,
        out_specs=out_specs,
        out_shape=out_shape,
        scratch_shapes=[pltpu.VMEM((N_HEADS_MOBA * NBLK_PAD + 32, W_MOBA), F32),
                        pltpu.VMEM((SUBLANES, LANES), F32)],
        compiler_params=pltpu.CompilerParams(dimension_semantics=("arbitrary", "arbitrary"),
                                             vmem_limit_bytes=VMEM_LIMIT),
        name="proj",
    )(x, g, p["w_tok"], p["w_t"], p["b_f"], p["q_norm"], p["kv_norm"], p["wuq_t"], p["wuq_rot_t"],
      p["wukv_k"], p["wukv_v_t"], ctok, stok, ct, st, far)


def _attn_kernel(*refs, has_bias, tile):
    if has_bias:
        qt_ref, k_ref, vt_ref, bidx_ref, tab_ref, o_ref, acc_ref, bias_ref = refs
    else:
        qt_ref, k_ref, vt_ref, o_ref, acc_ref = refs
    hp = pl.program_id(0)
    bi = pl.program_id(1)
    qi = pl.program_id(2)

    if has_bias:
        @pl.when((bi == 0) & (qi == 0))
        def _():
            for hh in range(2):
                def build(d, carry, hh=hh):
                    idx = bidx_ref[d]
                    bias_ref[hh, d] = lax.fori_loop(
                        0, T5_BUCKETS,
                        lambda bk, tl: jnp.where(idx == bk, tab_ref[bk, hp * 2 + hh], tl),
                        jnp.zeros((tile, tile), F32))
                    return carry
                lax.fori_loop(0, N_NEAR, build, 0)

    acc_ref[...] = jnp.zeros_like(acc_ref)
    causal = (lax.broadcasted_iota(jnp.int32, (tile, tile), 0)
              <= lax.broadcasted_iota(jnp.int32, (tile, tile), 1))

    def kv_tile(j, carry, diag, near):
        out = []
        for hh in range(2):
            m, l = carry[hh]
            start = pl.multiple_of(j * tile, tile)
            k = k_ref[0, pl.ds(start, tile), hh * LANES:(hh + 1) * LANES]
            s = _dot(k, qt_ref[0, hh * LANES:(hh + 1) * LANES, :])
            if has_bias and near:
                s = s + bias_ref[hh, qi - j]
            if diag:
                s = jnp.where(causal, s, NEG_INF)
            m_new = jnp.maximum(m, jnp.max(s, axis=0, keepdims=True))
            alpha = jnp.exp(m - m_new)
            p = jnp.exp(s - m_new)
            l_new = alpha * l + jnp.sum(p, axis=0, keepdims=True)
            v = vt_ref[0, j, hh * HEAD_DIM:(hh + 1) * HEAD_DIM, :]
            rows = slice(hh * HEAD_DIM, (hh + 1) * HEAD_DIM)
            acc_ref[rows, :] = alpha * acc_ref[rows, :] + _dot(v, p.astype(BF16))
            out.append((m_new, l_new))
        return tuple(out)

    init = jnp.full((1, tile), NEG_INF, F32), jnp.zeros((1, tile), F32)
    carry = kv_tile(qi, (init, init), True, True)
    if has_bias:
        first_near = jnp.maximum(qi - (N_NEAR - 1), 0)
        carry = lax.fori_loop(first_near, qi, lambda j, c: kv_tile(j, c, False, True), carry)
        carry = lax.fori_loop(0, first_near, lambda j, c: kv_tile(j, c, False, False), carry)
    else:
        carry = lax.fori_loop(0, qi, lambda j, c: kv_tile(j, c, False, False), carry)

    inv = jnp.concatenate([jnp.broadcast_to(1.0 / carry[hh][1], (HEAD_DIM, tile)) for hh in range(2)], axis=0)
    o_ref[0] = (acc_ref[...] * inv).T.astype(BF16)


def _attn_call(qt, k, vt, n_pairs, bias_inputs=None):
    b, _, s = qt.shape
    tile = ATT_TILE
    nq = s // tile
    has_bias = bias_inputs is not None
    in_specs = [
        pl.BlockSpec((1, 2 * LANES, tile), lambda p, bi, qi: (bi, p, qi)),
        pl.BlockSpec((1, s, 2 * LANES), lambda p, bi, qi: (bi, 0, p)),
        pl.BlockSpec((1, nq, 2 * HEAD_DIM, tile), lambda p, bi, qi: (bi, 0, p, 0)),
    ]
    args = [qt, k, vt]
    scratch = [pltpu.VMEM((2 * HEAD_DIM, tile), F32)]
    if has_bias:
        bidx, tab = bias_inputs
        in_specs += [pl.BlockSpec(bidx.shape, lambda p, bi, qi: (0, 0, 0)),
                     pl.BlockSpec(memory_space=pltpu.SMEM)]
        args += [bidx, tab]
        scratch.append(pltpu.VMEM((2, N_NEAR, tile, tile), F32))
    return pl.pallas_call(
        functools.partial(_attn_kernel, has_bias=has_bias, tile=tile),
        grid=(n_pairs, b, nq),
        in_specs=in_specs,
        out_specs=pl.BlockSpec((1, tile, 2 * HEAD_DIM), lambda p, bi, qi: (bi, qi, p)),
        out_shape=jax.ShapeDtypeStruct((b, s, n_pairs * 2 * HEAD_DIM), BF16),
        scratch_shapes=scratch,
        compiler_params=pltpu.CompilerParams(dimension_semantics=("arbitrary",) * 3,
                                             vmem_limit_bytes=VMEM_LIMIT),
        name="attn_bias" if has_bias else "attn",
    )(*args)


def _oproj_kernel(x_ref, oa_ref, ob_ref, oc_ref, wa_ref, wb_ref, wc_ref, g_ref, o_ref):
    y = _dot(oa_ref[0], wa_ref[...]) + _dot(ob_ref[0], wb_ref[...]) + _dot(oc_ref[0], wc_ref[...])
    o_ref[0] = x_ref[0] + _rms(y, g_ref[...])


def _oproj_call(x, oa, ob, oc, p, g, tm):
    b, s, d = x.shape

    def rows(width):
        return pl.BlockSpec((1, tm, width), lambda bi, ti: (bi, ti, 0))

    def full(a):
        return pl.BlockSpec(a.shape, lambda bi, ti: (0,) * a.ndim)

    return pl.pallas_call(
        _oproj_kernel,
        grid=(b, s // tm),
        in_specs=[rows(d), rows(W_MOBA), rows(W_MLA), rows(W_FOX),
                  full(p["wo_a"]), full(p["wo_b"]), full(p["wo_c"]), full(g)],
        out_specs=rows(d),
        out_shape=jax.ShapeDtypeStruct(x.shape, F32),
        compiler_params=pltpu.CompilerParams(dimension_semantics=("arbitrary", "arbitrary"),
                                             vmem_limit_bytes=VMEM_LIMIT),
        name="oproj",
    )(x, oa, ob, oc, p["wo_a"], p["wo_b"], p["wo_c"], g)


def _ffn_kernel(x_ref, xp_ref, gpre_ref, wg_ref, wv_ref, cwg_ref, cwv_ref, cbg_ref, cbv_ref, wd_ref, gpost_ref,
                o_ref, h_ref, acc_ref, *, tm):
    ti = pl.program_id(1)
    f = pl.program_id(2)
    halo = FFN_HALO

    @pl.when(f == 0)
    def _():
        h_ref[halo:, :] = _rms(x_ref[0], gpre_ref[...]).astype(BF16)
        prev = _rms(xp_ref[0], gpre_ref[...])
        h_ref[0:halo, :] = jnp.where(ti == 0, 0.0, prev).astype(BF16)
        acc_ref[...] = jnp.zeros_like(acc_ref)

    h = h_ref[...]

    def conv(w_ref, cw_ref, cb_ref):
        u = _dot(h, w_ref[...])
        return (cw_ref[0:1, :] * u[halo - 2:halo - 2 + tm, :]
                + cw_ref[1:2, :] * u[halo - 1:halo - 1 + tm, :]
                + cw_ref[2:3, :] * u[halo:halo + tm, :]) + cb_ref[...]

    gate = conv(wg_ref, cwg_ref, cbg_ref)
    val = conv(wv_ref, cwv_ref, cbv_ref)
    act = (jax.nn.gelu(gate, approximate=True) * val).astype(BF16)
    acc_ref[...] += _dot(act, wd_ref[...])

    @pl.when(f == pl.num_programs(2) - 1)
    def _():
        o_ref[0] = x_ref[0] + _rms(acc_ref[...], gpost_ref[...])


def _ffn_call(x, p, gpre, gpost, tm, tf):
    b, s, d = x.shape
    nf = D_FF // tf
    halo = FFN_HALO
    blocks_per_tile = tm // halo

    def full(a):
        return pl.BlockSpec(a.shape, lambda bi, ti, f: (0,) * a.ndim)

    in_specs = [
        pl.BlockSpec((1, tm, d), lambda bi, ti, f: (bi, ti, 0)),
        pl.BlockSpec((1, halo, d), lambda bi, ti, f: (bi, jnp.maximum(ti * blocks_per_tile - 1, 0), 0)),
        full(gpre),
        pl.BlockSpec((d, tf), lambda bi, ti, f: (0, f)),
        pl.BlockSpec((d, tf), lambda bi, ti, f: (0, nf + f)),
        pl.BlockSpec((CONV_WIDTH, tf), lambda bi, ti, f: (0, f)),
        pl.BlockSpec((CONV_WIDTH, tf), lambda bi, ti, f: (0, nf + f)),
        pl.BlockSpec((1, tf), lambda bi, ti, f: (0, f)),
        pl.BlockSpec((1, tf), lambda bi, ti, f: (0, nf + f)),
        pl.BlockSpec((tf, d), lambda bi, ti, f: (f, 0)),
        full(gpost),
    ]
    return pl.pallas_call(
        functools.partial(_ffn_kernel, tm=tm),
        grid=(b, s // tm, nf),
        in_specs=in_specs,
        out_specs=pl.BlockSpec((1, tm, d), lambda bi, ti, f: (bi, ti, 0)),
        out_shape=jax.ShapeDtypeStruct(x.shape, F32),
        scratch_shapes=[pltpu.VMEM((halo + tm, d), BF16), pltpu.VMEM((tm, d), F32)],
        compiler_params=pltpu.CompilerParams(dimension_semantics=("arbitrary",) * 3,
                                             vmem_limit_bytes=VMEM_LIMIT),
        name="ffn",
    )(x, x, gpre, p["w_up"], p["w_up"], p["conv_w"], p["conv_w"], p["conv_b"], p["conv_b"], p["w_down"], gpost)


def _pad_cols(w, left, total):
    return jnp.pad(w, ((0, 0), (left, total - left - w.shape[1])))


def _prep_layer(w_in, b_f, q_norm, kv_norm, w_uq, w_ukv, w_o, w_up, conv_w, conv_b, w_down):
    offs = [0]
    for sz in PROJ_SIZES:
        offs.append(offs[-1] + sz)
    a_q, a_k, a_v, c_q, c_kv, k_r, f_q, f_k, f_v, f_g = [w_in[:, offs[i]:offs[i + 1]] for i in range(10)]
    half = MLA_ROPE_DIM // 2

    def rot(w):
        return jnp.concatenate([-w[:, half:], w[:, :half]], axis=1)

    w_tok = jnp.concatenate([
        a_k, f_k, c_q, c_kv,
        _pad_cols(k_r, FEAT0, LANES), _pad_cols(rot(k_r), FEAT0, LANES), _pad_cols(f_g, 0, LANES)], axis=1)
    w_t = jnp.concatenate([a_q, f_q, a_v, f_v], axis=1).T
    uq = w_uq.reshape(MLA_Q_RANK, N_HEADS_MLA, MLA_QK_DIM)
    uq_main = jnp.pad(uq, ((0, 0), (0, 0), (0, LANES - MLA_QK_DIM)))
    uq_rope = uq[:, :, MLA_NOPE_DIM:]
    uq_rot = jnp.concatenate([-uq_rope[:, :, half:], uq_rope[:, :, :half]], axis=2)
    uq_rot = jnp.pad(uq_rot, ((0, 0), (0, 0), (MLA_NOPE_DIM, LANES - MLA_QK_DIM)))
    ukv = w_ukv.reshape(MLA_KV_RANK, N_HEADS_MLA, MLA_NOPE_DIM + MLA_V_DIM)
    ukv_k = jnp.pad(ukv[:, :, :MLA_NOPE_DIM], ((0, 0), (0, 0), (0, LANES - MLA_NOPE_DIM)))
    ukv_v = ukv[:, :, MLA_NOPE_DIM:]
    return {
        "w_tok": w_tok.astype(BF16),
        "w_t": w_t.astype(BF16),
        "b_f": _pad_cols(b_f[None, :], 0, LANES),
        "q_norm": q_norm[None, :],
        "kv_norm": kv_norm[None, :],
        "wuq_t": uq_main.reshape(MLA_Q_RANK, N_HEADS_MLA * LANES).T.astype(BF16),
        "wuq_rot_t": uq_rot.reshape(MLA_Q_RANK, N_HEADS_MLA * LANES).T.astype(BF16),
        "wukv_k": ukv_k.reshape(MLA_KV_RANK, N_HEADS_MLA * LANES).astype(BF16),
        "wukv_v_t": ukv_v.reshape(MLA_KV_RANK, W_MLA).T.astype(BF16),
        "wo_a": w_o[:W_MOBA].astype(BF16),
        "wo_b": w_o[W_MOBA:W_MOBA + W_MLA].astype(BF16),
        "wo_c": w_o[W_MOBA + W_MLA:].astype(BF16),
        "w_up": w_up.astype(BF16),
        "conv_w": conv_w,
        "conv_b": conv_b[None, :],
        "w_down": w_down.astype(BF16),
    }


def _rope_tables(s):
    half = MLA_ROPE_DIM // 2
    inv = ROPE_THETA ** (-jnp.arange(half, dtype=F32) / half)
    ang = jnp.arange(s).astype(F32)[:, None] * inv[None, :]
    cos = jnp.concatenate([jnp.cos(ang)] * 2, axis=1)
    sin = jnp.concatenate([jnp.sin(ang)] * 2, axis=1)
    ctok = _pad_cols(cos, FEAT0, LANES)
    stok = _pad_cols(sin, FEAT0, LANES)
    ones = jnp.ones((s, MLA_NOPE_DIM), F32)
    ct = jnp.concatenate([ones, cos, jnp.zeros((s, LANES - MLA_QK_DIM), F32)], axis=1).T
    st = stok.T
    return ctok, stok, ct, st


def _t5_bucket_tiles(tile):
    d = jnp.arange(N_NEAR)[:, None, None]
    j = jnp.arange(tile)[None, :, None]
    i = jnp.arange(tile)[None, None, :]
    n = jnp.maximum(d * tile + i - j, 0)
    exact = T5_BUCKETS // 2
    large = exact + (jnp.log(jnp.maximum(n, 1).astype(F32) / exact)
                     / math.log(T5_MAX_DIST / exact) * (T5_BUCKETS - exact)).astype(jnp.int32)
    return jnp.where(n < exact, n, jnp.minimum(large, T5_BUCKETS - 1)).astype(jnp.int32)


def kernel(x, rel_bias, ln_mix_pre, ln_mix_post, ln_ffn_pre, ln_ffn_post, w_in, b_f, q_norm, kv_norm, w_uq,
           w_ukv, w_o, w_up, conv_w, conv_b, w_down):
    b, s, d = x.shape
    depth = w_in.shape[0]
    assert s % PROJ_ROWS == 0 and s % FFN_ROWS == 0 and s % OPROJ_ROWS == 0
    assert s // MOBA_BLOCK <= NBLK_PAD and D_FF % FFN_COLS == 0
    tables = _rope_tables(s)
    bidx = _t5_bucket_tiles(ATT_TILE)
    far = rel_bias[T5_BUCKETS - 1, :]
    for l in range(depth):
        p = _prep_layer(w_in[l], b_f[l], q_norm[l], kv_norm[l], w_uq[l], w_ukv[l], w_o[l], w_up[l],
                        conv_w[l], conv_b[l], w_down[l])
        qat, ka, vat, qft, kf, vft, qmt, km, vmt = _proj_call(x, ln_mix_pre[l][None, :], p, tables, far, PROJ_ROWS)
        oa = _attn_call(qat, ka, vat, N_HEADS_MOBA // 2, (bidx, rel_bias))
        ob = _attn_call(qmt, km, vmt, N_HEADS_MLA // 2)
        oc = _attn_call(qft, kf, vft, N_HEADS_FOX // 2)
        x = _oproj_call(x, oa, ob, oc, p, ln_mix_post[l][None, :], OPROJ_ROWS)
        x = _ffn_call(x, p, ln_ffn_pre[l][None, :], ln_ffn_post[l][None, :], FFN_ROWS, FFN_COLS)
    return x
```

```python
import functools
import math

import jax
import jax.numpy as jnp
from jax import lax
from jax.experimental import pallas as pl
from jax.experimental.pallas import tpu as pltpu

F32 = jnp.float32
BF16 = jnp.bfloat16

HEAD_DIM = 64
N_HEADS_MOBA = 6
N_HEADS_MLA = 4
N_HEADS_FOX = 6
MOBA_BLOCK = 256
MOBA_TOPK = 3
MLA_Q_RANK = 256
MLA_KV_RANK = 128
MLA_NOPE_DIM = 64
MLA_ROPE_DIM = 32
MLA_V_DIM = 64
MLA_QK_DIM = MLA_NOPE_DIM + MLA_ROPE_DIM
ROPE_THETA = 10000.0
T5_BUCKETS = 32
T5_MAX_DIST = 1024
D_FF = 2816
CONV_WIDTH = 3
NORM_EPS = 1e-6
NEG_INF = -1e30
W_MOBA = N_HEADS_MOBA * HEAD_DIM
W_MLA = N_HEADS_MLA * MLA_V_DIM
W_FOX = N_HEADS_FOX * HEAD_DIM
PROJ_SIZES = (W_MOBA, W_MOBA, W_MOBA, MLA_Q_RANK, MLA_KV_RANK, MLA_ROPE_DIM, W_FOX, W_FOX, W_FOX, N_HEADS_FOX)

LANES = 128
SUBLANES = 8
VMEM_LIMIT = 56 * 1024 * 1024

ATT_TILE = MOBA_BLOCK
KV_TILE = 512
PROJ_ROWS = KV_TILE
OPROJ_ROWS = 512
FFN_ROWS = 1024
FFN_COLS = 256
FFN_HALO = 16
FEAT0 = HEAD_DIM
NBLK_PAD = 16
N_NEAR = (T5_MAX_DIST + ATT_TILE - 1 + ATT_TILE - 1) // ATT_TILE

_TOK_AK = (0, 384)
_TOK_FK = (384, 768)
_TOK_CQ = (768, 1024)
_TOK_CKV = (1024, 1152)
_TOK_KR = (1152, 1280)
_TOK_KRR = (1280, 1408)
_TOK_FG = (1408, 1536)
_TOK_COLS = 1536
_T_AQ = (0, 384)
_T_FQ = (384, 768)
_T_AV = (768, 1152)
_T_FV = (1152, 1536)


def _rms(xf, g):
    return xf * lax.rsqrt(jnp.mean(xf * xf, axis=-1, keepdims=True) + NORM_EPS) * g


def _dot(a, b):
    return jnp.dot(a, b, preferred_element_type=F32)


def _dot_nt(a, b):
    return lax.dot_general(a, b, (((1,), (1,)), ((), ())), preferred_element_type=F32)


def _bf16_pieces(x):
    hi = x.astype(BF16).astype(F32)
    r = x - hi
    mid = r.astype(BF16).astype(F32)
    lo = (r - mid).astype(BF16).astype(F32)
    return hi, mid, lo


def _dot_3pass(a, b):
    a_hi = a.astype(BF16)
    a_lo = (a - a_hi.astype(F32)).astype(BF16)
    b_hi = b.astype(BF16)
    b_lo = (b - b_hi.astype(F32)).astype(BF16)
    return _dot(a_hi, b_hi) + (_dot(a_hi, b_lo) + _dot(a_lo, b_hi))


def _head_slot(arr, hd):
    pair = arr[:, (hd // 2) * LANES:(hd // 2 + 1) * LANES]
    if hd % 2:
        pair = pltpu.roll(pair, HEAD_DIM, axis=1)
    return pair


def _proj_kernel(x_ref, g_ref, wtok_ref, wt_ref, bf_ref, qn_ref, kvn_ref, wuq_ref, wuqr_ref,
                 wukvk_ref, wukvv_ref, ctok_ref, stok_ref, ct_ref, st_ref, far_ref,
                 qat_ref, ka_ref, vat_ref, qft_ref, kf_ref, vft_ref, qmt_ref, km_ref, vmt_ref,
                 kmean_ref, fcarry_ref, *, tm):
    t = pl.program_id(1)
    n_sub = tm // ATT_TILE

    @pl.when(t == 0)
    def _():
        kmean_ref[...] = jnp.zeros_like(kmean_ref)
        fcarry_ref[...] = jnp.zeros_like(fcarry_ref)

    h = _rms(x_ref[0], g_ref[...]).astype(BF16)
    lane = lax.broadcasted_iota(jnp.int32, (tm, LANES), 1)
    row = lax.broadcasted_iota(jnp.int32, (tm, LANES), 0)
    blk0 = t * n_sub
    is_head = lane < HEAD_DIM

    def w_tok(rng):
        return wtok_ref[:, rng[0]:rng[1]]

    def w_t(rng):
        return wt_ref[rng[0]:rng[1], :]

    def store_vt(ref, vt):
        ref[0, 0] = vt.astype(BF16)

    ak = _dot(h, w_tok(_TOK_AK))
    blk_row = blk0 + lax.shift_right_logical(row, int(math.log2(MOBA_BLOCK)))
    k_feat = jnp.where(lane == FEAT0 + blk_row, 1.0, 0.0)
    for hd in range(N_HEADS_MOBA):
        ka_ref[0, :, hd * LANES:(hd + 1) * LANES] = jnp.where(is_head, _head_slot(ak, hd), k_feat).astype(BF16)

    lane_k = lax.broadcasted_iota(jnp.int32, (1, W_MOBA), 1)
    for bi in range(n_sub):
        mean_row = jnp.mean(ak[bi * MOBA_BLOCK:(bi + 1) * MOBA_BLOCK, :], axis=0, keepdims=True)
        for hd in range(N_HEADS_MOBA):
            in_head = (lane_k >= hd * HEAD_DIM) & (lane_k < (hd + 1) * HEAD_DIM)
            kmean_ref[pl.ds(hd * NBLK_PAD + blk0 + bi, 1), :] = jnp.where(in_head, mean_row, 0.0)

    qat = _dot_nt(w_t(_T_AQ), h) * (HEAD_DIM ** -0.5)
    gate = _dot_3pass(kmean_ref[...], qat)
    own = blk0 + lax.shift_right_logical(
        lax.broadcasted_iota(jnp.int32, (NBLK_PAD, tm), 1), int(math.log2(MOBA_BLOCK)))
    n_io = lax.broadcasted_iota(jnp.int32, (NBLK_PAD, tm), 0)
    past = n_io < own
    zeros_tail = jnp.zeros((LANES - HEAD_DIM - NBLK_PAD, tm), F32)
    for hd in range(N_HEADS_MOBA):
        g = jnp.where(past, gate[hd * NBLK_PAD:(hd + 1) * NBLK_PAD, :], NEG_INF)
        rank = jnp.zeros((NBLK_PAD, tm), jnp.int32)
        for n2 in range(NBLK_PAD):
            r = g[n2:n2 + 1, :]
            tie = jnp.where(n2 < n_io, 1, 0)
            rank = rank + jnp.where(r > g, 1, jnp.where(r == g, tie, 0))
        sel = jnp.where(past, rank, MOBA_TOPK) < MOBA_TOPK
        far = jnp.where(own - n_io >= N_NEAR, far_ref[hd], 0.0)
        q_feat = jnp.where(sel | (n_io == own), far, NEG_INF)
        qat_ref[0, hd * LANES:(hd + 1) * LANES, :] = jnp.concatenate(
            [qat[hd * HEAD_DIM:(hd + 1) * HEAD_DIM, :], q_feat, zeros_tail], axis=0).astype(BF16)
    store_vt(vat_ref, _dot_nt(w_t(_T_AV), h))

    fg = _dot(h, w_tok(_TOK_FG)) + bf_ref[...]
    logf = jnp.minimum(fg, 0.0) - jnp.log1p(jnp.exp(-jnp.abs(fg)))
    csum = jnp.where(lane < N_HEADS_FOX, logf, 0.0)
    sft = 1
    while sft < tm:
        csum = csum + jnp.where(row >= sft, pltpu.roll(csum, sft, axis=0), 0.0)
        sft *= 2
    decay = csum + fcarry_ref[0:1, :]
    fcarry_ref[0:1, :] = decay[tm - 1:tm, :]
    decay_t = decay.T

    fk = _dot(h, w_tok(_TOK_FK))
    fqt = _dot_nt(w_t(_T_FQ), h) * (HEAD_DIM ** -0.5)
    sub8 = lax.broadcasted_iota(jnp.int32, (SUBLANES, tm), 0)
    zeros_tail_f = jnp.zeros((LANES - HEAD_DIM - SUBLANES, tm), F32)
    for hd in range(N_HEADS_FOX):
        fcol = jnp.broadcast_to(decay[:, hd:hd + 1], (tm, LANES))
        hi, mid, lo = _bf16_pieces(fcol)
        k_feat_f = jnp.where(lane < FEAT0 + 3, 1.0,
                             jnp.where(lane == FEAT0 + 3, -hi,
                                       jnp.where(lane == FEAT0 + 4, -mid,
                                                 jnp.where(lane == FEAT0 + 5, -lo, 0.0))))
        kf_ref[0, :, hd * LANES:(hd + 1) * LANES] = jnp.where(is_head, _head_slot(fk, hd), k_feat_f).astype(BF16)
        hi, mid, lo = _bf16_pieces(decay_t[hd:hd + 1, :])
        q_feat_f = jnp.where(sub8 == 0, hi,
                             jnp.where(sub8 == 1, mid,
                                       jnp.where(sub8 == 2, lo,
                                                 jnp.where(sub8 < 6, 1.0, 0.0))))
        qft_ref[0, hd * LANES:(hd + 1) * LANES, :] = jnp.concatenate(
            [fqt[hd * HEAD_DIM:(hd + 1) * HEAD_DIM, :], q_feat_f, zeros_tail_f], axis=0).astype(BF16)
    store_vt(vft_ref, _dot_nt(w_t(_T_FV), h))

    cqn = _rms(_dot(h, w_tok(_TOK_CQ)), qn_ref[...]).astype(BF16)
    qm = _dot_nt(wuq_ref[...], cqn)
    qmr = _dot_nt(wuqr_ref[...], cqn)
    kvn = _rms(_dot(h, w_tok(_TOK_CKV)), kvn_ref[...]).astype(BF16)
    k_nope = _dot(kvn, wukvk_ref[...])
    k_rope = _dot(h, w_tok(_TOK_KR)) * ctok_ref[...] + _dot(h, w_tok(_TOK_KRR)) * stok_ref[...]
    cos_t = ct_ref[...]
    sin_t = st_ref[...]
    for hd in range(N_HEADS_MLA):
        sl = slice(hd * LANES, (hd + 1) * LANES)
        qmt_ref[0, sl, :] = ((qm[sl, :] * cos_t + qmr[sl, :] * sin_t) * (MLA_QK_DIM ** -0.5)).astype(BF16)
        km_ref[0, :, sl] = (k_nope[:, sl] + k_rope).astype(BF16)
    store_vt(vmt_ref, _dot_nt(wukvv_ref[...], kvn))


def _proj_call(x, g, p, tables, far, tm):
    b, s, d = x.shape
    nt = s // tm
    n_sub = tm // ATT_TILE
    nq = s // ATT_TILE

    def full(a):
        return pl.BlockSpec(a.shape, lambda bi, ti: (0,) * a.ndim)

    ctok, stok, ct, st = tables
    in_specs = [
        pl.BlockSpec((1, tm, d), lambda bi, ti: (bi, ti, 0)),
        full(g), full(p["w_tok"]), full(p["w_t"]), full(p["b_f"]), full(p["q_norm"]), full(p["kv_norm"]),
        full(p["wuq_t"]), full(p["wuq_rot_t"]), full(p["wukv_k"]), full(p["wukv_v_t"]),
        pl.BlockSpec((tm, LANES), lambda bi, ti: (ti, 0)),
        pl.BlockSpec((tm, LANES), lambda bi, ti: (ti, 0)),
        pl.BlockSpec((LANES, tm), lambda bi, ti: (0, ti)),
        pl.BlockSpec((LANES, tm), lambda bi, ti: (0, ti)),
        pl.BlockSpec(memory_space=pltpu.SMEM),
    ]

    def qt_spec(n_heads):
        return pl.BlockSpec((1, n_heads * LANES, tm), lambda bi, ti: (bi, 0, ti))

    def k_spec(n_heads):
        return pl.BlockSpec((1, tm, n_heads * LANES), lambda bi, ti: (bi, ti, 0))

    def vt_spec(width):
        return pl.BlockSpec((1, 1, width, tm), lambda bi, ti: (bi, ti, 0, 0))

    def qt_shape(n_heads):
        return jax.ShapeDtypeStruct((b, n_heads * LANES, s), BF16)

    def k_shape(n_heads):
        return jax.ShapeDtypeStruct((b, s, n_heads * LANES), BF16)

    def vt_shape(width):
        return jax.ShapeDtypeStruct((b, nt, width, tm), BF16)

    out_specs = [qt_spec(6), k_spec(6), vt_spec(W_MOBA), qt_spec(6), k_spec(6), vt_spec(W_FOX),
                 qt_spec(4), k_spec(4), vt_spec(W_MLA)]
    out_shape = [qt_shape(6), k_shape(6), vt_shape(W_MOBA), qt_shape(6), k_shape(6), vt_shape(W_FOX),
                 qt_shape(4), k_shape(4), vt_shape(W_MLA)]
    return pl.pallas_call(
        functools.partial(_proj_kernel, tm=tm),
        grid=(b, nt),
        in_specs=in_specs,
        out_specs=out_specs,
        out_shape=out_shape,
        scratch_shapes=[pltpu.VMEM((N_HEADS_MOBA * NBLK_PAD + 32, W_MOBA), F32),
                        pltpu.VMEM((SUBLANES, LANES), F32)],
        compiler_params=pltpu.CompilerParams(dimension_semantics=("arbitrary", "arbitrary"),
                                             vmem_limit_bytes=VMEM_LIMIT),
        name="proj",
    )(x, g, p["w_tok"], p["w_t"], p["b_f"], p["q_norm"], p["kv_norm"], p["wuq_t"], p["wuq_rot_t"],
      p["wukv_k"], p["wukv_v_t"], ctok, stok, ct, st, far)


def _attn_kernel(*refs, has_bias, tq, tk):
    if has_bias:
        qt_ref, k_ref, vt_ref, bidx_ref, tab_ref, o_ref, acc_ref, sa_ref, sb_ref, bias_ref = refs
    else:
        qt_ref, k_ref, vt_ref, o_ref, acc_ref, sa_ref, sb_ref = refs
    hp = pl.program_id(0)
    bi = pl.program_id(1)
    qi = pl.program_id(2)
    sub = tk // tq

    if has_bias:
        @pl.when((bi == 0) & (qi == 0))
        def _():
            for hh in range(2):
                def build(d, carry, hh=hh):
                    idx = bidx_ref[d]
                    bias_ref[hh, d] = lax.fori_loop(
                        0, T5_BUCKETS,
                        lambda bk, tl: jnp.where(idx == bk, tab_ref[bk, hp * 2 + hh], tl),
                        jnp.zeros((tq, tq), F32))
                    return carry
                lax.fori_loop(0, N_NEAR, build, 0)
                bias_ref[hh, N_NEAR] = jnp.zeros((tq, tq), F32)

    acc_ref[...] = jnp.zeros_like(acc_ref)

    def scores(j, s_ref, masked):
        start = pl.multiple_of(j * tk, tk)
        tmax = []
        for hh in range(2):
            k = k_ref[0, pl.ds(start, tk), hh * LANES:(hh + 1) * LANES]
            s = _dot(k, qt_ref[0, hh * LANES:(hh + 1) * LANES, :])
            if has_bias:
                dist = [jnp.clip(qi - (j * sub + r), 0, N_NEAR) for r in range(sub)]
                s = s + jnp.concatenate([bias_ref[hh, d] for d in dist], axis=0)
            if masked:
                kpos = j * tk + lax.broadcasted_iota(jnp.int32, (tk, tq), 0)
                qpos = qi * tq + lax.broadcasted_iota(jnp.int32, (tk, tq), 1)
                s = jnp.where(kpos <= qpos, s, NEG_INF)
            s_ref[hh] = s
            tmax.append(jnp.max(s, axis=0, keepdims=True))
        return tuple(tmax)

    def update(j, s_ref, ml, tmax):
        out = []
        for hh in range(2):
            m, l = ml[hh]
            m_new = jnp.maximum(m, tmax[hh])
            alpha = jnp.exp(m - m_new)
            p = jnp.exp(s_ref[hh] - m_new)
            l_new = alpha * l + jnp.sum(p, axis=0, keepdims=True)
            v = vt_ref[0, j, hh * HEAD_DIM:(hh + 1) * HEAD_DIM, :]
            rows = slice(hh * HEAD_DIM, (hh + 1) * HEAD_DIM)
            acc_ref[rows, :] = alpha * acc_ref[rows, :] + _dot(v, p.astype(BF16))
            out.append((m_new, l_new))
        return tuple(out)

    n_full = (qi * tq) // tk
    init = jnp.full((1, tq), NEG_INF, F32), jnp.zeros((1, tq), F32)
    tmax0 = scores(n_full, sa_ref, True)

    def body(i, carry):
        ml, tmax = carry

        def even(ml, tmax):
            nxt = scores(n_full - i - 1, sb_ref, False)
            return update(n_full - i, sa_ref, ml, tmax), nxt

        def odd(ml, tmax):
            nxt = scores(n_full - i - 1, sa_ref, False)
            return update(n_full - i, sb_ref, ml, tmax), nxt

        return lax.cond(i % 2 == 0, even, odd, ml, tmax)

    ml, tmax = lax.fori_loop(0, n_full, body, ((init, init), tmax0))
    ml = lax.cond(n_full % 2 == 0,
                  lambda ml, tmax: update(0, sa_ref, ml, tmax),
                  lambda ml, tmax: update(0, sb_ref, ml, tmax), ml, tmax)

    inv = jnp.concatenate([jnp.broadcast_to(1.0 / ml[hh][1], (HEAD_DIM, tq)) for hh in range(2)], axis=0)
    o_ref[0] = (acc_ref[...] * inv).T.astype(BF16)


def _attn_call(qt, k, vt, n_pairs, bias_inputs=None):
    b, _, s = qt.shape
    tq, tk = ATT_TILE, KV_TILE
    has_bias = bias_inputs is not None
    in_specs = [
        pl.BlockSpec((1, 2 * LANES, tq), lambda p, bi, qi: (bi, p, qi)),
        pl.BlockSpec((1, s, 2 * LANES), lambda p, bi, qi: (bi, 0, p)),
        pl.BlockSpec((1, s // tk, 2 * HEAD_DIM, tk), lambda p, bi, qi: (bi, 0, p, 0)),
    ]
    args = [qt, k, vt]
    scratch = [pltpu.VMEM((2 * HEAD_DIM, tq), F32),
               pltpu.VMEM((2, tk, tq), F32), pltpu.VMEM((2, tk, tq), F32)]
    if has_bias:
        bidx, tab = bias_inputs
        in_specs += [pl.BlockSpec(bidx.shape, lambda p, bi, qi: (0, 0, 0)),
                     pl.BlockSpec(memory_space=pltpu.SMEM)]
        args += [bidx, tab]
        scratch.append(pltpu.VMEM((2, N_NEAR + 1, tq, tq), F32))
    return pl.pallas_call(
        functools.partial(_attn_kernel, has_bias=has_bias, tq=tq, tk=tk),
        grid=(n_pairs, b, s // tq),
        in_specs=in_specs,
        out_specs=pl.BlockSpec((1, tq, 2 * HEAD_DIM), lambda p, bi, qi: (bi, qi, p)),
        out_shape=jax.ShapeDtypeStruct((b, s, n_pairs * 2 * HEAD_DIM), BF16),
        scratch_shapes=scratch,
        compiler_params=pltpu.CompilerParams(dimension_semantics=("arbitrary",) * 3,
                                             vmem_limit_bytes=VMEM_LIMIT),
        name="attn_bias" if has_bias else "attn",
    )(*args)


def _oproj_kernel(x_ref, oa_ref, ob_ref, oc_ref, wa_ref, wb_ref, wc_ref, g_ref, o_ref):
    y = _dot(oa_ref[0], wa_ref[...]) + _dot(ob_ref[0], wb_ref[...]) + _dot(oc_ref[0], wc_ref[...])
    o_ref[0] = x_ref[0] + _rms(y, g_ref[...])


def _oproj_call(x, oa, ob, oc, p, g, tm):
    b, s, d = x.shape

    def rows(width):
        return pl.BlockSpec((1, tm, width), lambda bi, ti: (bi, ti, 0))

    def full(a):
        return pl.BlockSpec(a.shape, lambda bi, ti: (0,) * a.ndim)

    return pl.pallas_call(
        _oproj_kernel,
        grid=(b, s // tm),
        in_specs=[rows(d), rows(W_MOBA), rows(W_MLA), rows(W_FOX),
                  full(p["wo_a"]), full(p["wo_b"]), full(p["wo_c"]), full(g)],
        out_specs=rows(d),
        out_shape=jax.ShapeDtypeStruct(x.shape, F32),
        compiler_params=pltpu.CompilerParams(dimension_semantics=("arbitrary", "arbitrary"),
                                             vmem_limit_bytes=VMEM_LIMIT),
        name="oproj",
    )(x, oa, ob, oc, p["wo_a"], p["wo_b"], p["wo_c"], g)


def _ffn_kernel(x_ref, xp_ref, gpre_ref, wg_ref, wv_ref, cwg_ref, cwv_ref, cbg_ref, cbv_ref, wd_ref, gpost_ref,
                o_ref, h_ref, acc_ref, *, tm):
    ti = pl.program_id(1)
    f = pl.program_id(2)
    halo = FFN_HALO

    @pl.when(f == 0)
    def _():
        h_ref[halo:, :] = _rms(x_ref[0], gpre_ref[...]).astype(BF16)
        prev = _rms(xp_ref[0], gpre_ref[...])
        h_ref[0:halo, :] = jnp.where(ti == 0, 0.0, prev).astype(BF16)
        acc_ref[...] = jnp.zeros_like(acc_ref)

    h = h_ref[...]

    def conv(w_ref, cw_ref, cb_ref):
        u = _dot(h, w_ref[...])
        return (cw_ref[0:1, :] * u[halo - 2:halo - 2 + tm, :]
                + cw_ref[1:2, :] * u[halo - 1:halo - 1 + tm, :]
                + cw_ref[2:3, :] * u[halo:halo + tm, :]) + cb_ref[...]

    gate = conv(wg_ref, cwg_ref, cbg_ref)
    val = conv(wv_ref, cwv_ref, cbv_ref)
    act = (jax.nn.gelu(gate, approximate=True) * val).astype(BF16)
    acc_ref[...] += _dot(act, wd_ref[...])

    @pl.when(f == pl.num_programs(2) - 1)
    def _():
        o_ref[0] = x_ref[0] + _rms(acc_ref[...], gpost_ref[...])


def _ffn_call(x, p, gpre, gpost, tm, tf):
    b, s, d = x.shape
    nf = D_FF // tf
    halo = FFN_HALO
    blocks_per_tile = tm // halo

    def full(a):
        return pl.BlockSpec(a.shape, lambda bi, ti, f: (0,) * a.ndim)

    in_specs = [
        pl.BlockSpec((1, tm, d), lambda bi, ti, f: (bi, ti, 0)),
        pl.BlockSpec((1, halo, d), lambda bi, ti, f: (bi, jnp.maximum(ti * blocks_per_tile - 1, 0), 0)),
        full(gpre),
        pl.BlockSpec((d, tf), lambda bi, ti, f: (0, f)),
        pl.BlockSpec((d, tf), lambda bi, ti, f: (0, nf + f)),
        pl.BlockSpec((CONV_WIDTH, tf), lambda bi, ti, f: (0, f)),
        pl.BlockSpec((CONV_WIDTH, tf), lambda bi, ti, f: (0, nf + f)),
        pl.BlockSpec((1, tf), lambda bi, ti, f: (0, f)),
        pl.BlockSpec((1, tf), lambda bi, ti, f: (0, nf + f)),
        pl.BlockSpec((tf, d), lambda bi, ti, f: (f, 0)),
        full(gpost),
    ]
    return pl.pallas_call(
        functools.partial(_ffn_kernel, tm=tm),
        grid=(b, s // tm, nf),
        in_specs=in_specs,
        out_specs=pl.BlockSpec((1, tm, d), lambda bi, ti, f: (bi, ti, 0)),
        out_shape=jax.ShapeDtypeStruct(x.shape, F32),
        scratch_shapes=[pltpu.VMEM((halo + tm, d), BF16), pltpu.VMEM((tm, d), F32)],
        compiler_params=pltpu.CompilerParams(dimension_semantics=("arbitrary",) * 3,
                                             vmem_limit_bytes=VMEM_LIMIT),
        name="ffn",
    )(x, x, gpre, p["w_up"], p["w_up"], p["conv_w"], p["conv_w"], p["conv_b"], p["conv_b"], p["w_down"], gpost)


def _pad_cols(w, left, total):
    return jnp.pad(w, ((0, 0), (left, total - left - w.shape[1])))


def _prep_layer(w_in, b_f, q_norm, kv_norm, w_uq, w_ukv, w_o, w_up, conv_w, conv_b, w_down):
    offs = [0]
    for sz in PROJ_SIZES:
        offs.append(offs[-1] + sz)
    a_q, a_k, a_v, c_q, c_kv, k_r, f_q, f_k, f_v, f_g = [w_in[:, offs[i]:offs[i + 1]] for i in range(10)]
    half = MLA_ROPE_DIM // 2

    def rot(w):
        return jnp.concatenate([-w[:, half:], w[:, :half]], axis=1)

    w_tok = jnp.concatenate([
        a_k, f_k, c_q, c_kv,
        _pad_cols(k_r, FEAT0, LANES), _pad_cols(rot(k_r), FEAT0, LANES), _pad_cols(f_g, 0, LANES)], axis=1)
    w_t = jnp.concatenate([a_q, f_q, a_v, f_v], axis=1).T
    uq = w_uq.reshape(MLA_Q_RANK, N_HEADS_MLA, MLA_QK_DIM)
    uq_main = jnp.pad(uq, ((0, 0), (0, 0), (0, LANES - MLA_QK_DIM)))
    uq_rope = uq[:, :, MLA_NOPE_DIM:]
    uq_rot = jnp.concatenate([-uq_rope[:, :, half:], uq_rope[:, :, :half]], axis=2)
    uq_rot = jnp.pad(uq_rot, ((0, 0), (0, 0), (MLA_NOPE_DIM, LANES - MLA_QK_DIM)))
    ukv = w_ukv.reshape(MLA_KV_RANK, N_HEADS_MLA, MLA_NOPE_DIM + MLA_V_DIM)
    ukv_k = jnp.pad(ukv[:, :, :MLA_NOPE_DIM], ((0, 0), (0, 0), (0, LANES - MLA_NOPE_DIM)))
    ukv_v = ukv[:, :, MLA_NOPE_DIM:]
    return {
        "w_tok": w_tok.astype(BF16),
        "w_t": w_t.astype(BF16),
        "b_f": _pad_cols(b_f[None, :], 0, LANES),
        "q_norm": q_norm[None, :],
        "kv_norm": kv_norm[None, :],
        "wuq_t": uq_main.reshape(MLA_Q_RANK, N_HEADS_MLA * LANES).T.astype(BF16),
        "wuq_rot_t": uq_rot.reshape(MLA_Q_RANK, N_HEADS_MLA * LANES).T.astype(BF16),
        "wukv_k": ukv_k.reshape(MLA_KV_RANK, N_HEADS_MLA * LANES).astype(BF16),
        "wukv_v_t": ukv_v.reshape(MLA_KV_RANK, W_MLA).T.astype(BF16),
        "wo_a": w_o[:W_MOBA].astype(BF16),
        "wo_b": w_o[W_MOBA:W_MOBA + W_MLA].astype(BF16),
        "wo_c": w_o[W_MOBA + W_MLA:].astype(BF16),
        "w_up": w_up.astype(BF16),
        "conv_w": conv_w,
        "conv_b": conv_b[None, :],
        "w_down": w_down.astype(BF16),
    }


def _rope_tables(s):
    half = MLA_ROPE_DIM // 2
    inv = ROPE_THETA ** (-jnp.arange(half, dtype=F32) / half)
    ang = jnp.arange(s).astype(F32)[:, None] * inv[None, :]
    cos = jnp.concatenate([jnp.cos(ang)] * 2, axis=1)
    sin = jnp.concatenate([jnp.sin(ang)] * 2, axis=1)
    ctok = _pad_cols(cos, FEAT0, LANES)
    stok = _pad_cols(sin, FEAT0, LANES)
    ones = jnp.ones((s, MLA_NOPE_DIM), F32)
    ct = jnp.concatenate([ones, cos, jnp.zeros((s, LANES - MLA_QK_DIM), F32)], axis=1).T
    st = stok.T
    return ctok, stok, ct, st


def _t5_bucket_tiles(tile):
    d = jnp.arange(N_NEAR)[:, None, None]
    j = jnp.arange(tile)[None, :, None]
    i = jnp.arange(tile)[None, None, :]
    n = jnp.maximum(d * tile + i - j, 0)
    exact = T5_BUCKETS // 2
    large = exact + (jnp.log(jnp.maximum(n, 1).astype(F32) / exact)
                     / math.log(T5_MAX_DIST / exact) * (T5_BUCKETS - exact)).astype(jnp.int32)
    return jnp.where(n < exact, n, jnp.minimum(large, T5_BUCKETS - 1)).astype(jnp.int32)


def kernel(x, rel_bias, ln_mix_pre, ln_mix_post, ln_ffn_pre, ln_ffn_post, w_in, b_f, q_norm, kv_norm, w_uq,
           w_ukv, w_o, w_up, conv_w, conv_b, w_down):
    b, s, d = x.shape
    depth = w_in.shape[0]
    assert s % PROJ_ROWS == 0 and s % FFN_ROWS == 0 and s % OPROJ_ROWS == 0
    assert s // MOBA_BLOCK <= NBLK_PAD and D_FF % FFN_COLS == 0
    tables = _rope_tables(s)
    bidx = _t5_bucket_tiles(ATT_TILE)
    far = rel_bias[T5_BUCKETS - 1, :]
    for l in range(depth):
        p = _prep_layer(w_in[l], b_f[l], q_norm[l], kv_norm[l], w_uq[l], w_ukv[l], w_o[l], w_up[l],
                        conv_w[l], conv_b[l], w_down[l])
        qat, ka, vat, qft, kf, vft, qmt, km, vmt = _proj_call(x, ln_mix_pre[l][None, :], p, tables, far, PROJ_ROWS)
        oa = _attn_call(qat, ka, vat, N_HEADS_MOBA // 2, (bidx, rel_bias))
        ob = _attn_call(qmt, km, vmt, N_HEADS_MLA // 2)
        oc = _attn_call(qft, kf, vft, N_HEADS_FOX // 2)
        x = _oproj_call(x, oa, ob, oc, p, ln_mix_post[l][None, :], OPROJ_ROWS)
        x = _ffn_call(x, p, ln_ffn_pre[l][None, :], ln_ffn_post[l][None, :], FFN_ROWS, FFN_COLS)
    return x
```

```python
import functools
import math

import jax
import jax.numpy as jnp
from jax import lax
from jax.experimental import pallas as pl
from jax.experimental.pallas import tpu as pltpu

F32 = jnp.float32
BF16 = jnp.bfloat16

HEAD_DIM = 64
N_HEADS_MOBA = 6
N_HEADS_MLA = 4
N_HEADS_FOX = 6
MOBA_BLOCK = 256
MOBA_TOPK = 3
MLA_Q_RANK = 256
MLA_KV_RANK = 128
MLA_NOPE_DIM = 64
MLA_ROPE_DIM = 32
MLA_V_DIM = 64
MLA_QK_DIM = MLA_NOPE_DIM + MLA_ROPE_DIM
ROPE_THETA = 10000.0
T5_BUCKETS = 32
T5_MAX_DIST = 1024
D_FF = 2816
CONV_WIDTH = 3
NORM_EPS = 1e-6
NEG_INF = -1e30
W_MOBA = N_HEADS_MOBA * HEAD_DIM
W_MLA = N_HEADS_MLA * MLA_V_DIM
W_FOX = N_HEADS_FOX * HEAD_DIM
PROJ_SIZES = (W_MOBA, W_MOBA, W_MOBA, MLA_Q_RANK, MLA_KV_RANK, MLA_ROPE_DIM, W_FOX, W_FOX, W_FOX, N_HEADS_FOX)

LANES = 128
SUBLANES = 8
VMEM_LIMIT = 56 * 1024 * 1024

ATT_TILE = 512
KV_TILE = 512
PROJ_ROWS = KV_TILE
OPROJ_ROWS = 512
FFN_ROWS = 1024
FFN_COLS = 256
FFN_HALO = 16
FEAT0 = HEAD_DIM
V_ROWS = HEAD_DIM + 16
LOG2E = math.log2(math.e)
NBLK_PAD = 16
N_NEAR = (T5_MAX_DIST + MOBA_BLOCK - 1 + MOBA_BLOCK - 1) // MOBA_BLOCK

_TOK_AK = (0, 384)
_TOK_FK = (384, 768)
_TOK_CQ = (768, 1024)
_TOK_CKV = (1024, 1152)
_TOK_KR = (1152, 1280)
_TOK_KRR = (1280, 1408)
_TOK_FG = (1408, 1536)
_TOK_COLS = 1536
_T_AQ = (0, 384)
_T_FQ = (384, 768)
_T_AV = (768, 1152)
_T_FV = (1152, 1536)


def _rms(xf, g):
    return xf * lax.rsqrt(jnp.mean(xf * xf, axis=-1, keepdims=True) + NORM_EPS) * g


def _dot(a, b):
    return jnp.dot(a, b, preferred_element_type=F32)


def _dot_nt(a, b):
    return lax.dot_general(a, b, (((1,), (1,)), ((), ())), preferred_element_type=F32)


def _bf16_pieces(x):
    hi = x.astype(BF16).astype(F32)
    r = x - hi
    mid = r.astype(BF16).astype(F32)
    lo = (r - mid).astype(BF16).astype(F32)
    return hi, mid, lo


def _dot_3pass(a, b):
    a_hi = a.astype(BF16)
    a_lo = (a - a_hi.astype(F32)).astype(BF16)
    b_hi = b.astype(BF16)
    b_lo = (b - b_hi.astype(F32)).astype(BF16)
    return _dot(a_hi, b_hi) + (_dot(a_hi, b_lo) + _dot(a_lo, b_hi))


def _head_slot(arr, hd):
    pair = arr[:, (hd // 2) * LANES:(hd // 2 + 1) * LANES]
    if hd % 2:
        pair = pltpu.roll(pair, HEAD_DIM, axis=1)
    return pair


def _proj_kernel(x_ref, g_ref, wtok_ref, wt_ref, bf_ref, qn_ref, kvn_ref, wuq_ref, wuqr_ref,
                 wukvk_ref, wukvv_ref, ctok_ref, stok_ref, ct_ref, st_ref, far_ref,
                 qat_ref, ka_ref, vat_ref, qft_ref, kf_ref, vft_ref, qmt_ref, km_ref, vmt_ref,
                 kmean_ref, fcarry_ref, *, tm):
    t = pl.program_id(1)
    n_sub = tm // MOBA_BLOCK

    @pl.when(t == 0)
    def _():
        kmean_ref[...] = jnp.zeros_like(kmean_ref)
        fcarry_ref[...] = jnp.zeros_like(fcarry_ref)

    h = _rms(x_ref[0], g_ref[...]).astype(BF16)
    lane = lax.broadcasted_iota(jnp.int32, (tm, LANES), 1)
    row = lax.broadcasted_iota(jnp.int32, (tm, LANES), 0)
    blk0 = t * n_sub
    is_head = lane < HEAD_DIM

    def w_tok(rng):
        return wtok_ref[:, rng[0]:rng[1]]

    def w_t(rng):
        return wt_ref[rng[0]:rng[1], :]

    sub8v = lax.broadcasted_iota(jnp.int32, (SUBLANES, tm), 0)
    ones_row = jnp.where(sub8v == 0, 1.0, 0.0)
    v_pad = jnp.zeros((V_ROWS - HEAD_DIM - SUBLANES, tm), F32)

    def store_vt(ref, vt):
        n_heads = vt.shape[0] // HEAD_DIM
        ref[0, 0] = jnp.concatenate(
            [piece for hd in range(n_heads)
             for piece in (vt[hd * HEAD_DIM:(hd + 1) * HEAD_DIM, :], ones_row, v_pad)], axis=0).astype(BF16)

    ak = _dot(h, w_tok(_TOK_AK))
    blk_row = blk0 + lax.shift_right_logical(row, int(math.log2(MOBA_BLOCK)))
    k_feat = jnp.where(lane == FEAT0 + blk_row, 1.0, 0.0)
    for hd in range(N_HEADS_MOBA):
        ka_ref[0, :, hd * LANES:(hd + 1) * LANES] = jnp.where(is_head, _head_slot(ak, hd), k_feat).astype(BF16)

    lane_k = lax.broadcasted_iota(jnp.int32, (1, W_MOBA), 1)
    for bi in range(n_sub):
        mean_row = jnp.mean(ak[bi * MOBA_BLOCK:(bi + 1) * MOBA_BLOCK, :], axis=0, keepdims=True)
        for hd in range(N_HEADS_MOBA):
            in_head = (lane_k >= hd * HEAD_DIM) & (lane_k < (hd + 1) * HEAD_DIM)
            kmean_ref[pl.ds(hd * NBLK_PAD + blk0 + bi, 1), :] = jnp.where(in_head, mean_row, 0.0)

    qat = _dot_nt(w_t(_T_AQ), h) * (HEAD_DIM ** -0.5 * LOG2E)
    gate = _dot_3pass(kmean_ref[...], qat)
    own = blk0 + lax.shift_right_logical(
        lax.broadcasted_iota(jnp.int32, (NBLK_PAD, tm), 1), int(math.log2(MOBA_BLOCK)))
    n_io = lax.broadcasted_iota(jnp.int32, (NBLK_PAD, tm), 0)
    past = n_io < own
    zeros_tail = jnp.zeros((LANES - HEAD_DIM - NBLK_PAD, tm), F32)
    for hd in range(N_HEADS_MOBA):
        g = jnp.where(past, gate[hd * NBLK_PAD:(hd + 1) * NBLK_PAD, :], NEG_INF)
        rank = jnp.zeros((NBLK_PAD, tm), jnp.int32)
        for n2 in range(NBLK_PAD):
            r = g[n2:n2 + 1, :]
            tie = jnp.where(n2 < n_io, 1, 0)
            rank = rank + jnp.where(r > g, 1, jnp.where(r == g, tie, 0))
        sel = jnp.where(past, rank, MOBA_TOPK) < MOBA_TOPK
        far = jnp.where(own - n_io >= N_NEAR, far_ref[hd] * LOG2E, 0.0)
        q_feat = jnp.where(sel | (n_io == own), far, NEG_INF)
        qat_ref[0, hd * LANES:(hd + 1) * LANES, :] = jnp.concatenate(
            [qat[hd * HEAD_DIM:(hd + 1) * HEAD_DIM, :], q_feat, zeros_tail], axis=0).astype(BF16)
    store_vt(vat_ref, _dot_nt(w_t(_T_AV), h))

    fg = _dot(h, w_tok(_TOK_FG)) + bf_ref[...]
    logf = jnp.minimum(fg, 0.0) - jnp.log1p(jnp.exp(-jnp.abs(fg)))
    csum = jnp.where(lane < N_HEADS_FOX, logf, 0.0)
    sft = 1
    while sft < tm:
        csum = csum + jnp.where(row >= sft, pltpu.roll(csum, sft, axis=0), 0.0)
        sft *= 2
    decay = csum + fcarry_ref[0:1, :]
    fcarry_ref[0:1, :] = decay[tm - 1:tm, :]
    decay = decay * LOG2E
    decay_t = decay.T

    fk = _dot(h, w_tok(_TOK_FK))
    fqt = _dot_nt(w_t(_T_FQ), h) * (HEAD_DIM ** -0.5 * LOG2E)
    sub8 = lax.broadcasted_iota(jnp.int32, (SUBLANES, tm), 0)
    zeros_tail_f = jnp.zeros((LANES - HEAD_DIM - SUBLANES, tm), F32)
    for hd in range(N_HEADS_FOX):
        fcol = jnp.broadcast_to(decay[:, hd:hd + 1], (tm, LANES))
        hi, mid, lo = _bf16_pieces(fcol)
        k_feat_f = jnp.where(lane < FEAT0 + 3, 1.0,
                             jnp.where(lane == FEAT0 + 3, -hi,
                                       jnp.where(lane == FEAT0 + 4, -mid,
                                                 jnp.where(lane == FEAT0 + 5, -lo, 0.0))))
        kf_ref[0, :, hd * LANES:(hd + 1) * LANES] = jnp.where(is_head, _head_slot(fk, hd), k_feat_f).astype(BF16)
        hi, mid, lo = _bf16_pieces(decay_t[hd:hd + 1, :])
        q_feat_f = jnp.where(sub8 == 0, hi,
                             jnp.where(sub8 == 1, mid,
                                       jnp.where(sub8 == 2, lo,
                                                 jnp.where(sub8 < 6, 1.0, 0.0))))
        qft_ref[0, hd * LANES:(hd + 1) * LANES, :] = jnp.concatenate(
            [fqt[hd * HEAD_DIM:(hd + 1) * HEAD_DIM, :], q_feat_f, zeros_tail_f], axis=0).astype(BF16)
    store_vt(vft_ref, _dot_nt(w_t(_T_FV), h))

    cqn = _rms(_dot(h, w_tok(_TOK_CQ)), qn_ref[...]).astype(BF16)
    qm = _dot_nt(wuq_ref[...], cqn)
    qmr = _dot_nt(wuqr_ref[...], cqn)
    kvn = _rms(_dot(h, w_tok(_TOK_CKV)), kvn_ref[...]).astype(BF16)
    k_nope = _dot(kvn, wukvk_ref[...])
    k_rope = _dot(h, w_tok(_TOK_KR)) * ctok_ref[...] + _dot(h, w_tok(_TOK_KRR)) * stok_ref[...]
    cos_t = ct_ref[...]
    sin_t = st_ref[...]
    for hd in range(N_HEADS_MLA):
        sl = slice(hd * LANES, (hd + 1) * LANES)
        qmt_ref[0, sl, :] = ((qm[sl, :] * cos_t + qmr[sl, :] * sin_t) * (MLA_QK_DIM ** -0.5 * LOG2E)).astype(BF16)
        km_ref[0, :, sl] = (k_nope[:, sl] + k_rope).astype(BF16)
    store_vt(vmt_ref, _dot_nt(wukvv_ref[...], kvn))


def _proj_call(x, g, p, tables, far, tm):
    b, s, d = x.shape
    nt = s // tm
    n_sub = tm // MOBA_BLOCK
    nq = s // ATT_TILE

    def full(a):
        return pl.BlockSpec(a.shape, lambda bi, ti: (0,) * a.ndim)

    ctok, stok, ct, st = tables
    in_specs = [
        pl.BlockSpec((1, tm, d), lambda bi, ti: (bi, ti, 0)),
        full(g), full(p["w_tok"]), full(p["w_t"]), full(p["b_f"]), full(p["q_norm"]), full(p["kv_norm"]),
        full(p["wuq_t"]), full(p["wuq_rot_t"]), full(p["wukv_k"]), full(p["wukv_v_t"]),
        pl.BlockSpec((tm, LANES), lambda bi, ti: (ti, 0)),
        pl.BlockSpec((tm, LANES), lambda bi, ti: (ti, 0)),
        pl.BlockSpec((LANES, tm), lambda bi, ti: (0, ti)),
        pl.BlockSpec((LANES, tm), lambda bi, ti: (0, ti)),
        pl.BlockSpec(memory_space=pltpu.SMEM),
    ]

    def qt_spec(n_heads):
        return pl.BlockSpec((1, n_heads * LANES, tm), lambda bi, ti: (bi, 0, ti))

    def k_spec(n_heads):
        return pl.BlockSpec((1, tm, n_heads * LANES), lambda bi, ti: (bi, ti, 0))

    def vt_spec(width):
        return pl.BlockSpec((1, 1, width, tm), lambda bi, ti: (bi, ti, 0, 0))

    def qt_shape(n_heads):
        return jax.ShapeDtypeStruct((b, n_heads * LANES, s), BF16)

    def k_shape(n_heads):
        return jax.ShapeDtypeStruct((b, s, n_heads * LANES), BF16)

    def vt_shape(width):
        return jax.ShapeDtypeStruct((b, nt, width, tm), BF16)

    out_specs = [qt_spec(6), k_spec(6), vt_spec(6 * V_ROWS), qt_spec(6), k_spec(6), vt_spec(6 * V_ROWS),
                 qt_spec(4), k_spec(4), vt_spec(4 * V_ROWS)]
    out_shape = [qt_shape(6), k_shape(6), vt_shape(6 * V_ROWS), qt_shape(6), k_shape(6), vt_shape(6 * V_ROWS),
                 qt_shape(4), k_shape(4), vt_shape(4 * V_ROWS)]
    return pl.pallas_call(
        functools.partial(_proj_kernel, tm=tm),
        grid=(b, nt),
        in_specs=in_specs,
        out_specs=out_specs,
        out_shape=out_shape,
        scratch_shapes=[pltpu.VMEM((N_HEADS_MOBA * NBLK_PAD + 32, W_MOBA), F32),
                        pltpu.VMEM((SUBLANES, LANES), F32)],
        compiler_params=pltpu.CompilerParams(dimension_semantics=("arbitrary", "arbitrary"),
                                             vmem_limit_bytes=VMEM_LIMIT),
        name="proj",
    )(x, g, p["w_tok"], p["w_t"], p["b_f"], p["q_norm"], p["kv_norm"], p["wuq_t"], p["wuq_rot_t"],
      p["wukv_k"], p["wukv_v_t"], ctok, stok, ct, st, far)


def _attn_kernel(*refs, has_bias, tq, tk):
    if has_bias:
        qt_ref, k_ref, vt_ref, bidx_ref, tab_ref, o_ref, acc_ref, sa_ref, sb_ref, bias_ref = refs
    else:
        qt_ref, k_ref, vt_ref, o_ref, acc_ref, sa_ref, sb_ref = refs
    hp = pl.program_id(0)
    bi = pl.program_id(1)
    qi = pl.program_id(2)
    blk = MOBA_BLOCK

    if has_bias:
        @pl.when((bi == 0) & (qi == 0))
        def _():
            for hh in range(2):
                def build(d, carry, hh=hh):
                    idx = bidx_ref[d]
                    bias_ref[hh, d] = lax.fori_loop(
                        0, T5_BUCKETS,
                        lambda bk, tl: jnp.where(idx == bk, tab_ref[bk, hp * 2 + hh] * LOG2E, tl),
                        jnp.zeros((blk, blk), F32))
                    return carry
                lax.fori_loop(0, N_NEAR, build, 0)
                bias_ref[hh, N_NEAR] = jnp.zeros((blk, blk), F32)

    acc_ref[...] = jnp.zeros_like(acc_ref)

    def scores(j, s_ref, masked):
        start = pl.multiple_of(j * tk, tk)
        tmax = []
        for hh in range(2):
            k = k_ref[0, pl.ds(start, tk), hh * LANES:(hh + 1) * LANES]
            s = _dot(k, qt_ref[0, hh * LANES:(hh + 1) * LANES, :])
            if has_bias:
                s = s + jnp.concatenate([
                    jnp.concatenate([
                        bias_ref[hh, jnp.clip((qi * (tq // blk) + c) - (j * (tk // blk) + r), 0, N_NEAR)]
                        for c in range(tq // blk)], axis=1)
                    for r in range(tk // blk)], axis=0)
            if masked:
                kpos = j * tk + lax.broadcasted_iota(jnp.int32, (tk, tq), 0)
                qpos = qi * tq + lax.broadcasted_iota(jnp.int32, (tk, tq), 1)
                s = jnp.where(kpos <= qpos, s, NEG_INF)
            s_ref[hh] = s
            tmax.append(jnp.max(s, axis=0, keepdims=True))
        return tuple(tmax)

    def update(j, s_ref, ml, tmax):
        out = []
        for hh in range(2):
            m_new = jnp.maximum(ml[hh], tmax[hh])
            alpha = jnp.exp2(ml[hh] - m_new)
            p = jnp.exp2(s_ref[hh] - m_new).astype(BF16)
            v = vt_ref[0, j, hh * V_ROWS:(hh + 1) * V_ROWS, :]
            acc_ref[hh] = alpha * acc_ref[hh] + _dot(v, p)
            out.append(m_new)
        return tuple(out)

    n_full = (qi * tq) // tk
    init = jnp.full((1, tq), NEG_INF, F32)
    tmax0 = scores(n_full, sa_ref, True)

    def body(i, carry):
        ml, tmax = carry

        def even(ml, tmax):
            nxt = scores(n_full - i - 1, sb_ref, False)
            return update(n_full - i, sa_ref, ml, tmax), nxt

        def odd(ml, tmax):
            nxt = scores(n_full - i - 1, sa_ref, False)
            return update(n_full - i, sb_ref, ml, tmax), nxt

        return lax.cond(i % 2 == 0, even, odd, ml, tmax)

    ml, tmax = lax.fori_loop(0, n_full, body, ((init, init), tmax0))
    lax.cond(n_full % 2 == 0,
             lambda ml, tmax: update(0, sa_ref, ml, tmax),
             lambda ml, tmax: update(0, sb_ref, ml, tmax), ml, tmax)

    out = [acc_ref[hh, 0:HEAD_DIM, :] / acc_ref[hh, HEAD_DIM:HEAD_DIM + 1, :] for hh in range(2)]
    o_ref[0] = jnp.concatenate(out, axis=0).T.astype(BF16)


def _attn_call(qt, k, vt, n_pairs, bias_inputs=None):
    b, _, s = qt.shape
    tq, tk = ATT_TILE, KV_TILE
    has_bias = bias_inputs is not None
    in_specs = [
        pl.BlockSpec((1, 2 * LANES, tq), lambda p, bi, qi: (bi, p, qi)),
        pl.BlockSpec((1, s, 2 * LANES), lambda p, bi, qi: (bi, 0, p)),
        pl.BlockSpec((1, s // tk, 2 * V_ROWS, tk), lambda p, bi, qi: (bi, 0, p, 0)),
    ]
    args = [qt, k, vt]
    scratch = [pltpu.VMEM((2, V_ROWS, tq), F32),
               pltpu.VMEM((2, tk, tq), F32), pltpu.VMEM((2, tk, tq), F32)]
    if has_bias:
        bidx, tab = bias_inputs
        in_specs += [pl.BlockSpec(bidx.shape, lambda p, bi, qi: (0, 0, 0)),
                     pl.BlockSpec(memory_space=pltpu.SMEM)]
        args += [bidx, tab]
        scratch.append(pltpu.VMEM((2, N_NEAR + 1, MOBA_BLOCK, MOBA_BLOCK), F32))
    return pl.pallas_call(
        functools.partial(_attn_kernel, has_bias=has_bias, tq=tq, tk=tk),
        grid=(n_pairs, b, s // tq),
        in_specs=in_specs,
        out_specs=pl.BlockSpec((1, tq, 2 * HEAD_DIM), lambda p, bi, qi: (bi, qi, p)),
        out_shape=jax.ShapeDtypeStruct((b, s, n_pairs * 2 * HEAD_DIM), BF16),
        scratch_shapes=scratch,
        compiler_params=pltpu.CompilerParams(dimension_semantics=("arbitrary",) * 3,
                                             vmem_limit_bytes=VMEM_LIMIT),
        name="attn_bias" if has_bias else "attn",
    )(*args)


def _oproj_kernel(x_ref, oa_ref, ob_ref, oc_ref, wa_ref, wb_ref, wc_ref, g_ref, o_ref):
    y = _dot(oa_ref[0], wa_ref[...]) + _dot(ob_ref[0], wb_ref[...]) + _dot(oc_ref[0], wc_ref[...])
    o_ref[0] = x_ref[0] + _rms(y, g_ref[...])


def _oproj_call(x, oa, ob, oc, p, g, tm):
    b, s, d = x.shape

    def rows(width):
        return pl.BlockSpec((1, tm, width), lambda bi, ti: (bi, ti, 0))

    def full(a):
        return pl.BlockSpec(a.shape, lambda bi, ti: (0,) * a.ndim)

    return pl.pallas_call(
        _oproj_kernel,
        grid=(b, s // tm),
        in_specs=[rows(d), rows(W_MOBA), rows(W_MLA), rows(W_FOX),
                  full(p["wo_a"]), full(p["wo_b"]), full(p["wo_c"]), full(g)],
        out_specs=rows(d),
        out_shape=jax.ShapeDtypeStruct(x.shape, F32),
        compiler_params=pltpu.CompilerParams(dimension_semantics=("arbitrary", "arbitrary"),
                                             vmem_limit_bytes=VMEM_LIMIT),
        name="oproj",
    )(x, oa, ob, oc, p["wo_a"], p["wo_b"], p["wo_c"], g)


def _ffn_kernel(x_ref, xp_ref, gpre_ref, wg_ref, wv_ref, cwg_ref, cwv_ref, cbg_ref, cbv_ref, wd_ref, gpost_ref,
                o_ref, h_ref, acc_ref, *, tm):
    ti = pl.program_id(1)
    f = pl.program_id(2)
    halo = FFN_HALO

    @pl.when(f == 0)
    def _():
        h_ref[halo:, :] = _rms(x_ref[0], gpre_ref[...]).astype(BF16)
        prev = _rms(xp_ref[0], gpre_ref[...])
        h_ref[0:halo, :] = jnp.where(ti == 0, 0.0, prev).astype(BF16)
        acc_ref[...] = jnp.zeros_like(acc_ref)

    h = h_ref[...]

    def conv(w_ref, cw_ref, cb_ref):
        u = _dot(h, w_ref[...])
        return (cw_ref[0:1, :] * u[halo - 2:halo - 2 + tm, :]
                + cw_ref[1:2, :] * u[halo - 1:halo - 1 + tm, :]
                + cw_ref[2:3, :] * u[halo:halo + tm, :]) + cb_ref[...]

    gate = conv(wg_ref, cwg_ref, cbg_ref)
    val = conv(wv_ref, cwv_ref, cbv_ref)
    act = (jax.nn.gelu(gate, approximate=True) * val).astype(BF16)
    acc_ref[...] += _dot(act, wd_ref[...])

    @pl.when(f == pl.num_programs(2) - 1)
    def _():
        o_ref[0] = x_ref[0] + _rms(acc_ref[...], gpost_ref[...])


def _ffn_call(x, p, gpre, gpost, tm, tf):
    b, s, d = x.shape
    nf = D_FF // tf
    halo = FFN_HALO
    blocks_per_tile = tm // halo

    def full(a):
        return pl.BlockSpec(a.shape, lambda bi, ti, f: (0,) * a.ndim)

    in_specs = [
        pl.BlockSpec((1, tm, d), lambda bi, ti, f: (bi, ti, 0)),
        pl.BlockSpec((1, halo, d), lambda bi, ti, f: (bi, jnp.maximum(ti * blocks_per_tile - 1, 0), 0)),
        full(gpre),
        pl.BlockSpec((d, tf), lambda bi, ti, f: (0, f)),
        pl.BlockSpec((d, tf), lambda bi, ti, f: (0, nf + f)),
        pl.BlockSpec((CONV_WIDTH, tf), lambda bi, ti, f: (0, f)),
        pl.BlockSpec((CONV_WIDTH, tf), lambda bi, ti, f: (0, nf + f)),
        pl.BlockSpec((1, tf), lambda bi, ti, f: (0, f)),
        pl.BlockSpec((1, tf), lambda bi, ti, f: (0, nf + f)),
        pl.BlockSpec((tf, d), lambda bi, ti, f: (f, 0)),
        full(gpost),
    ]
    return pl.pallas_call(
        functools.partial(_ffn_kernel, tm=tm),
        grid=(b, s // tm, nf),
        in_specs=in_specs,
        out_specs=pl.BlockSpec((1, tm, d), lambda bi, ti, f: (bi, ti, 0)),
        out_shape=jax.ShapeDtypeStruct(x.shape, F32),
        scratch_shapes=[pltpu.VMEM((halo + tm, d), BF16), pltpu.VMEM((tm, d), F32)],
        compiler_params=pltpu.CompilerParams(dimension_semantics=("arbitrary",) * 3,
                                             vmem_limit_bytes=VMEM_LIMIT),
        name="ffn",
    )(x, x, gpre, p["w_up"], p["w_up"], p["conv_w"], p["conv_w"], p["conv_b"], p["conv_b"], p["w_down"], gpost)


def _pad_cols(w, left, total):
    return jnp.pad(w, ((0, 0), (left, total - left - w.shape[1])))


def _prep_layer(w_in, b_f, q_norm, kv_norm, w_uq, w_ukv, w_o, w_up, conv_w, conv_b, w_down):
    offs = [0]
    for sz in PROJ_SIZES:
        offs.append(offs[-1] + sz)
    a_q, a_k, a_v, c_q, c_kv, k_r, f_q, f_k, f_v, f_g = [w_in[:, offs[i]:offs[i + 1]] for i in range(10)]
    half = MLA_ROPE_DIM // 2

    def rot(w):
        return jnp.concatenate([-w[:, half:], w[:, :half]], axis=1)

    w_tok = jnp.concatenate([
        a_k, f_k, c_q, c_kv,
        _pad_cols(k_r, FEAT0, LANES), _pad_cols(rot(k_r), FEAT0, LANES), _pad_cols(f_g, 0, LANES)], axis=1)
    w_t = jnp.concatenate([a_q, f_q, a_v, f_v], axis=1).T
    uq = w_uq.reshape(MLA_Q_RANK, N_HEADS_MLA, MLA_QK_DIM)
    uq_main = jnp.pad(uq, ((0, 0), (0, 0), (0, LANES - MLA_QK_DIM)))
    uq_rope = uq[:, :, MLA_NOPE_DIM:]
    uq_rot = jnp.concatenate([-uq_rope[:, :, half:], uq_rope[:, :, :half]], axis=2)
    uq_rot = jnp.pad(uq_rot, ((0, 0), (0, 0), (MLA_NOPE_DIM, LANES - MLA_QK_DIM)))
    ukv = w_ukv.reshape(MLA_KV_RANK, N_HEADS_MLA, MLA_NOPE_DIM + MLA_V_DIM)
    ukv_k = jnp.pad(ukv[:, :, :MLA_NOPE_DIM], ((0, 0), (0, 0), (0, LANES - MLA_NOPE_DIM)))
    ukv_v = ukv[:, :, MLA_NOPE_DIM:]
    return {
        "w_tok": w_tok.astype(BF16),
        "w_t": w_t.astype(BF16),
        "b_f": _pad_cols(b_f[None, :], 0, LANES),
        "q_norm": q_norm[None, :],
        "kv_norm": kv_norm[None, :],
        "wuq_t": uq_main.reshape(MLA_Q_RANK, N_HEADS_MLA * LANES).T.astype(BF16),
        "wuq_rot_t": uq_rot.reshape(MLA_Q_RANK, N_HEADS_MLA * LANES).T.astype(BF16),
        "wukv_k": ukv_k.reshape(MLA_KV_RANK, N_HEADS_MLA * LANES).astype(BF16),
        "wukv_v_t": ukv_v.reshape(MLA_KV_RANK, W_MLA).T.astype(BF16),
        "wo_a": w_o[:W_MOBA].astype(BF16),
        "wo_b": w_o[W_MOBA:W_MOBA + W_MLA].astype(BF16),
        "wo_c": w_o[W_MOBA + W_MLA:].astype(BF16),
        "w_up": w_up.astype(BF16),
        "conv_w": conv_w,
        "conv_b": conv_b[None, :],
        "w_down": w_down.astype(BF16),
    }


def _rope_tables(s):
    half = MLA_ROPE_DIM // 2
    inv = ROPE_THETA ** (-jnp.arange(half, dtype=F32) / half)
    ang = jnp.arange(s).astype(F32)[:, None] * inv[None, :]
    cos = jnp.concatenate([jnp.cos(ang)] * 2, axis=1)
    sin = jnp.concatenate([jnp.sin(ang)] * 2, axis=1)
    ctok = _pad_cols(cos, FEAT0, LANES)
    stok = _pad_cols(sin, FEAT0, LANES)
    ones = jnp.ones((s, MLA_NOPE_DIM), F32)
    ct = jnp.concatenate([ones, cos, jnp.zeros((s, LANES - MLA_QK_DIM), F32)], axis=1).T
    st = stok.T
    return ctok, stok, ct, st


def _t5_bucket_tiles(tile):
    d = jnp.arange(N_NEAR)[:, None, None]
    j = jnp.arange(tile)[None, :, None]
    i = jnp.arange(tile)[None, None, :]
    n = jnp.maximum(d * tile + i - j, 0)
    exact = T5_BUCKETS // 2
    large = exact + (jnp.log(jnp.maximum(n, 1).astype(F32) / exact)
                     / math.log(T5_MAX_DIST / exact) * (T5_BUCKETS - exact)).astype(jnp.int32)
    return jnp.where(n < exact, n, jnp.minimum(large, T5_BUCKETS - 1)).astype(jnp.int32)


def kernel(x, rel_bias, ln_mix_pre, ln_mix_post, ln_ffn_pre, ln_ffn_post, w_in, b_f, q_norm, kv_norm, w_uq,
           w_ukv, w_o, w_up, conv_w, conv_b, w_down):
    b, s, d = x.shape
    depth = w_in.shape[0]
    assert s % PROJ_ROWS == 0 and s % FFN_ROWS == 0 and s % OPROJ_ROWS == 0
    assert s // MOBA_BLOCK <= NBLK_PAD and D_FF % FFN_COLS == 0
    tables = _rope_tables(s)
    bidx = _t5_bucket_tiles(MOBA_BLOCK)
    far = rel_bias[T5_BUCKETS - 1, :]
    for l in range(depth):
        p = _prep_layer(w_in[l], b_f[l], q_norm[l], kv_norm[l], w_uq[l], w_ukv[l], w_o[l], w_up[l],
                        conv_w[l], conv_b[l], w_down[l])
        qat, ka, vat, qft, kf, vft, qmt, km, vmt = _proj_call(x, ln_mix_pre[l][None, :], p, tables, far, PROJ_ROWS)
        oa = _attn_call(qat, ka, vat, N_HEADS_MOBA // 2, (bidx, rel_bias))
        ob = _attn_call(qmt, km, vmt, N_HEADS_MLA // 2)
        oc = _attn_call(qft, kf, vft, N_HEADS_FOX // 2)
        x = _oproj_call(x, oa, ob, oc, p, ln_mix_post[l][None, :], OPROJ_ROWS)
        x = _ffn_call(x, p, ln_ffn_pre[l][None, :], ln_ffn_post[l][None, :], FFN_ROWS, FFN_COLS)
    return x
```

```python
import functools
import math

import jax
import jax.numpy as jnp
from jax import lax
from jax.experimental import pallas as pl
from jax.experimental.pallas import tpu as pltpu

F32 = jnp.float32
BF16 = jnp.bfloat16

HEAD_DIM = 64
N_HEADS_MOBA = 6
N_HEADS_MLA = 4
N_HEADS_FOX = 6
MOBA_BLOCK = 256
MOBA_TOPK = 3
MLA_Q_RANK = 256
MLA_KV_RANK = 128
MLA_NOPE_DIM = 64
MLA_ROPE_DIM = 32
MLA_V_DIM = 64
MLA_QK_DIM = MLA_NOPE_DIM + MLA_ROPE_DIM
ROPE_THETA = 10000.0
T5_BUCKETS = 32
T5_MAX_DIST = 1024
D_FF = 2816
CONV_WIDTH = 3
NORM_EPS = 1e-6
NEG_INF = -1e30
W_MOBA = N_HEADS_MOBA * HEAD_DIM
W_MLA = N_HEADS_MLA * MLA_V_DIM
W_FOX = N_HEADS_FOX * HEAD_DIM
PROJ_SIZES = (W_MOBA, W_MOBA, W_MOBA, MLA_Q_RANK, MLA_KV_RANK, MLA_ROPE_DIM, W_FOX, W_FOX, W_FOX, N_HEADS_FOX)

LANES = 128
SUBLANES = 8
VMEM_LIMIT = 56 * 1024 * 1024

ATT_TILE = 512
KV_TILE = 512
PROJ_ROWS = KV_TILE
OPROJ_ROWS = 512
FFN_ROWS = 512
FFN_COLS = 256
FFN_HALO = 16
FEAT0 = HEAD_DIM
V_ROWS = HEAD_DIM + 16
LOG2E = math.log2(math.e)
NBLK_PAD = 16
N_NEAR = (T5_MAX_DIST + MOBA_BLOCK - 1 + MOBA_BLOCK - 1) // MOBA_BLOCK

_TOK_AK = (0, 384)
_TOK_FK = (384, 768)
_TOK_CQ = (768, 1024)
_TOK_CKV = (1024, 1152)
_TOK_KR = (1152, 1280)
_TOK_KRR = (1280, 1408)
_TOK_FG = (1408, 1536)
_TOK_COLS = 1536
_T_AQ = (0, 384)
_T_FQ = (384, 768)
_T_AV = (768, 1152)
_T_FV = (1152, 1536)


def _rms(xf, g):
    return xf * lax.rsqrt(jnp.mean(xf * xf, axis=-1, keepdims=True) + NORM_EPS) * g


def _dot(a, b):
    return jnp.dot(a, b, preferred_element_type=F32)


def _dot_nt(a, b):
    return lax.dot_general(a, b, (((1,), (1,)), ((), ())), preferred_element_type=F32)


def _bf16_pieces(x):
    hi = x.astype(BF16).astype(F32)
    r = x - hi
    mid = r.astype(BF16).astype(F32)
    lo = (r - mid).astype(BF16).astype(F32)
    return hi, mid, lo


def _dot_3pass(a, b):
    a_hi = a.astype(BF16)
    a_lo = (a - a_hi.astype(F32)).astype(BF16)
    b_hi = b.astype(BF16)
    b_lo = (b - b_hi.astype(F32)).astype(BF16)
    return _dot(a_hi, b_hi) + (_dot(a_hi, b_lo) + _dot(a_lo, b_hi))


def _head_slot(arr, hd):
    pair = arr[:, (hd // 2) * LANES:(hd // 2 + 1) * LANES]
    if hd % 2:
        pair = pltpu.roll(pair, HEAD_DIM, axis=1)
    return pair


def _proj_kernel(x_ref, g_ref, wtok_ref, wt_ref, bf_ref, qn_ref, kvn_ref, wuq_ref, wuqr_ref,
                 wukvk_ref, wukvv_ref, ctok_ref, stok_ref, ct_ref, st_ref, far_ref,
                 qat_ref, ka_ref, vat_ref, qft_ref, kf_ref, vft_ref, qmt_ref, km_ref, vmt_ref,
                 kmean_ref, fcarry_ref, *, tm):
    t = pl.program_id(1)
    n_sub = tm // MOBA_BLOCK

    @pl.when(t == 0)
    def _():
        kmean_ref[...] = jnp.zeros_like(kmean_ref)
        fcarry_ref[...] = jnp.zeros_like(fcarry_ref)

    h = _rms(x_ref[0], g_ref[...]).astype(BF16)
    lane = lax.broadcasted_iota(jnp.int32, (tm, LANES), 1)
    row = lax.broadcasted_iota(jnp.int32, (tm, LANES), 0)
    blk0 = t * n_sub
    is_head = lane < HEAD_DIM

    def w_tok(rng):
        return wtok_ref[:, rng[0]:rng[1]]

    def w_t(rng):
        return wt_ref[rng[0]:rng[1], :]

    sub8v = lax.broadcasted_iota(jnp.int32, (SUBLANES, tm), 0)
    ones_row = jnp.where(sub8v == 0, 1.0, 0.0)
    v_pad = jnp.zeros((V_ROWS - HEAD_DIM - SUBLANES, tm), F32)

    def store_vt(ref, vt):
        n_heads = vt.shape[0] // HEAD_DIM
        ref[0, 0] = jnp.concatenate(
            [piece for hd in range(n_heads)
             for piece in (vt[hd * HEAD_DIM:(hd + 1) * HEAD_DIM, :], ones_row, v_pad)], axis=0).astype(BF16)

    ak = _dot(h, w_tok(_TOK_AK))
    blk_row = blk0 + lax.shift_right_logical(row, int(math.log2(MOBA_BLOCK)))
    k_feat = jnp.where(lane == FEAT0 + blk_row, 1.0, 0.0)
    for hd in range(N_HEADS_MOBA):
        ka_ref[0, :, hd * LANES:(hd + 1) * LANES] = jnp.where(is_head, _head_slot(ak, hd), k_feat).astype(BF16)

    lane_k = lax.broadcasted_iota(jnp.int32, (1, W_MOBA), 1)
    for bi in range(n_sub):
        mean_row = jnp.mean(ak[bi * MOBA_BLOCK:(bi + 1) * MOBA_BLOCK, :], axis=0, keepdims=True)
        for hd in range(N_HEADS_MOBA):
            in_head = (lane_k >= hd * HEAD_DIM) & (lane_k < (hd + 1) * HEAD_DIM)
            kmean_ref[pl.ds(hd * NBLK_PAD + blk0 + bi, 1), :] = jnp.where(in_head, mean_row, 0.0)

    qat = _dot_nt(w_t(_T_AQ), h) * (HEAD_DIM ** -0.5 * LOG2E)
    gate = _dot_3pass(kmean_ref[...], qat)
    own = blk0 + lax.shift_right_logical(
        lax.broadcasted_iota(jnp.int32, (NBLK_PAD, tm), 1), int(math.log2(MOBA_BLOCK)))
    n_io = lax.broadcasted_iota(jnp.int32, (NBLK_PAD, tm), 0)
    past = n_io < own
    zeros_tail = jnp.zeros((LANES - HEAD_DIM - NBLK_PAD, tm), F32)
    for hd in range(N_HEADS_MOBA):
        g = jnp.where(past, gate[hd * NBLK_PAD:(hd + 1) * NBLK_PAD, :], NEG_INF)
        rank = jnp.zeros((NBLK_PAD, tm), jnp.int32)
        for n2 in range(NBLK_PAD):
            r = g[n2:n2 + 1, :]
            tie = jnp.where(n2 < n_io, 1, 0)
            rank = rank + jnp.where(r > g, 1, jnp.where(r == g, tie, 0))
        sel = jnp.where(past, rank, MOBA_TOPK) < MOBA_TOPK
        far = jnp.where(own - n_io >= N_NEAR, far_ref[hd] * LOG2E, 0.0)
        q_feat = jnp.where(sel | (n_io == own), far, NEG_INF)
        qat_ref[0, hd * LANES:(hd + 1) * LANES, :] = jnp.concatenate(
            [qat[hd * HEAD_DIM:(hd + 1) * HEAD_DIM, :], q_feat, zeros_tail], axis=0).astype(BF16)
    store_vt(vat_ref, _dot_nt(w_t(_T_AV), h))

    fg = _dot(h, w_tok(_TOK_FG)) + bf_ref[...]
    logf = jnp.minimum(fg, 0.0) - jnp.log1p(jnp.exp(-jnp.abs(fg)))
    csum = jnp.where(lane < N_HEADS_FOX, logf, 0.0)
    sft = 1
    while sft < tm:
        csum = csum + jnp.where(row >= sft, pltpu.roll(csum, sft, axis=0), 0.0)
        sft *= 2
    decay = csum + fcarry_ref[0:1, :]
    fcarry_ref[0:1, :] = decay[tm - 1:tm, :]
    decay = decay * LOG2E
    decay_t = decay.T

    fk = _dot(h, w_tok(_TOK_FK))
    fqt = _dot_nt(w_t(_T_FQ), h) * (HEAD_DIM ** -0.5 * LOG2E)
    sub8 = lax.broadcasted_iota(jnp.int32, (SUBLANES, tm), 0)
    zeros_tail_f = jnp.zeros((LANES - HEAD_DIM - SUBLANES, tm), F32)
    for hd in range(N_HEADS_FOX):
        fcol = jnp.broadcast_to(decay[:, hd:hd + 1], (tm, LANES))
        hi, mid, lo = _bf16_pieces(fcol)
        k_feat_f = jnp.where(lane < FEAT0 + 3, 1.0,
                             jnp.where(lane == FEAT0 + 3, -hi,
                                       jnp.where(lane == FEAT0 + 4, -mid,
                                                 jnp.where(lane == FEAT0 + 5, -lo, 0.0))))
        kf_ref[0, :, hd * LANES:(hd + 1) * LANES] = jnp.where(is_head, _head_slot(fk, hd), k_feat_f).astype(BF16)
        hi, mid, lo = _bf16_pieces(decay_t[hd:hd + 1, :])
        q_feat_f = jnp.where(sub8 == 0, hi,
                             jnp.where(sub8 == 1, mid,
                                       jnp.where(sub8 == 2, lo,
                                                 jnp.where(sub8 < 6, 1.0, 0.0))))
        qft_ref[0, hd * LANES:(hd + 1) * LANES, :] = jnp.concatenate(
            [fqt[hd * HEAD_DIM:(hd + 1) * HEAD_DIM, :], q_feat_f, zeros_tail_f], axis=0).astype(BF16)
    store_vt(vft_ref, _dot_nt(w_t(_T_FV), h))

    cqn = _rms(_dot(h, w_tok(_TOK_CQ)), qn_ref[...]).astype(BF16)
    qm = _dot_nt(wuq_ref[...], cqn)
    qmr = _dot_nt(wuqr_ref[...], cqn)
    kvn = _rms(_dot(h, w_tok(_TOK_CKV)), kvn_ref[...]).astype(BF16)
    k_nope = _dot(kvn, wukvk_ref[...])
    k_rope = _dot(h, w_tok(_TOK_KR)) * ctok_ref[...] + _dot(h, w_tok(_TOK_KRR)) * stok_ref[...]
    cos_t = ct_ref[...]
    sin_t = st_ref[...]
    for hd in range(N_HEADS_MLA):
        sl = slice(hd * LANES, (hd + 1) * LANES)
        qmt_ref[0, sl, :] = ((qm[sl, :] * cos_t + qmr[sl, :] * sin_t) * (MLA_QK_DIM ** -0.5 * LOG2E)).astype(BF16)
        km_ref[0, :, sl] = (k_nope[:, sl] + k_rope).astype(BF16)
    store_vt(vmt_ref, _dot_nt(wukvv_ref[...], kvn))


def _proj_call(x, g, p, tables, far, tm):
    b, s, d = x.shape
    nt = s // tm
    n_sub = tm // MOBA_BLOCK
    nq = s // ATT_TILE

    def full(a):
        return pl.BlockSpec(a.shape, lambda bi, ti: (0,) * a.ndim)

    ctok, stok, ct, st = tables
    in_specs = [
        pl.BlockSpec((1, tm, d), lambda bi, ti: (bi, ti, 0)),
        full(g), full(p["w_tok"]), full(p["w_t"]), full(p["b_f"]), full(p["q_norm"]), full(p["kv_norm"]),
        full(p["wuq_t"]), full(p["wuq_rot_t"]), full(p["wukv_k"]), full(p["wukv_v_t"]),
        pl.BlockSpec((tm, LANES), lambda bi, ti: (ti, 0)),
        pl.BlockSpec((tm, LANES), lambda bi, ti: (ti, 0)),
        pl.BlockSpec((LANES, tm), lambda bi, ti: (0, ti)),
        pl.BlockSpec((LANES, tm), lambda bi, ti: (0, ti)),
        pl.BlockSpec(memory_space=pltpu.SMEM),
    ]

    def qt_spec(n_heads):
        return pl.BlockSpec((1, n_heads * LANES, tm), lambda bi, ti: (bi, 0, ti))

    def k_spec(n_heads):
        return pl.BlockSpec((1, tm, n_heads * LANES), lambda bi, ti: (bi, ti, 0))

    def vt_spec(width):
        return pl.BlockSpec((1, 1, width, tm), lambda bi, ti: (bi, ti, 0, 0))

    def qt_shape(n_heads):
        return jax.ShapeDtypeStruct((b, n_heads * LANES, s), BF16)

    def k_shape(n_heads):
        return jax.ShapeDtypeStruct((b, s, n_heads * LANES), BF16)

    def vt_shape(width):
        return jax.ShapeDtypeStruct((b, nt, width, tm), BF16)

    out_specs = [qt_spec(6), k_spec(6), vt_spec(6 * V_ROWS), qt_spec(6), k_spec(6), vt_spec(6 * V_ROWS),
                 qt_spec(4), k_spec(4), vt_spec(4 * V_ROWS)]
    out_shape = [qt_shape(6), k_shape(6), vt_shape(6 * V_ROWS), qt_shape(6), k_shape(6), vt_shape(6 * V_ROWS),
                 qt_shape(4), k_shape(4), vt_shape(4 * V_ROWS)]
    return pl.pallas_call(
        functools.partial(_proj_kernel, tm=tm),
        grid=(b, nt),
        in_specs=in_specs,
        out_specs=out_specs,
        out_shape=out_shape,
        scratch_shapes=[pltpu.VMEM((N_HEADS_MOBA * NBLK_PAD + 32, W_MOBA), F32),
                        pltpu.VMEM((SUBLANES, LANES), F32)],
        compiler_params=pltpu.CompilerParams(dimension_semantics=("arbitrary", "arbitrary"),
                                             vmem_limit_bytes=VMEM_LIMIT),
        name="proj",
    )(x, g, p["w_tok"], p["w_t"], p["b_f"], p["q_norm"], p["kv_norm"], p["wuq_t"], p["wuq_rot_t"],
      p["wukv_k"], p["wukv_v_t"], ctok, stok, ct, st, far)


def _attn_kernel(*refs, has_bias, tq, tk):
    if has_bias:
        qt_ref, k_ref, vt_ref, bidx_ref, tab_ref, o_ref, acc_ref, sa_ref, sb_ref, bias_ref = refs
    else:
        qt_ref, k_ref, vt_ref, o_ref, acc_ref, sa_ref, sb_ref = refs
    hp = pl.program_id(0)
    bi = pl.program_id(1)
    qi = pl.program_id(2)
    blk = MOBA_BLOCK

    if has_bias:
        @pl.when((bi == 0) & (qi == 0))
        def _():
            for hh in range(2):
                def build(d, carry, hh=hh):
                    idx = bidx_ref[d]
                    bias_ref[hh, d] = lax.fori_loop(
                        0, T5_BUCKETS,
                        lambda bk, tl: jnp.where(idx == bk, tab_ref[bk, hp * 2 + hh] * LOG2E, tl),
                        jnp.zeros((blk, blk), F32))
                    return carry
                lax.fori_loop(0, N_NEAR, build, 0)
                bias_ref[hh, N_NEAR] = jnp.zeros((blk, blk), F32)

    acc_ref[...] = jnp.zeros_like(acc_ref)

    def scores(j, s_ref, masked):
        start = pl.multiple_of(j * tk, tk)
        tmax = []
        for hh in range(2):
            k = k_ref[0, pl.ds(start, tk), hh * LANES:(hh + 1) * LANES]
            s = _dot(k, qt_ref[0, hh * LANES:(hh + 1) * LANES, :])
            if has_bias:
                s = s + jnp.concatenate([
                    jnp.concatenate([
                        bias_ref[hh, jnp.clip((qi * (tq // blk) + c) - (j * (tk // blk) + r), 0, N_NEAR)]
                        for c in range(tq // blk)], axis=1)
                    for r in range(tk // blk)], axis=0)
            if masked:
                kpos = j * tk + lax.broadcasted_iota(jnp.int32, (tk, tq), 0)
                qpos = qi * tq + lax.broadcasted_iota(jnp.int32, (tk, tq), 1)
                s = jnp.where(kpos <= qpos, s, NEG_INF)
            s_ref[hh] = s
            tmax.append(jnp.max(s, axis=0, keepdims=True))
        return tuple(tmax)

    def update(j, s_ref, ml, tmax):
        out = []
        for hh in range(2):
            m_new = jnp.maximum(ml[hh], tmax[hh])
            alpha = jnp.exp2(ml[hh] - m_new)
            p = jnp.exp2(s_ref[hh] - m_new).astype(BF16)
            v = vt_ref[0, j, hh * V_ROWS:(hh + 1) * V_ROWS, :]
            acc_ref[hh] = alpha * acc_ref[hh] + _dot(v, p)
            out.append(m_new)
        return tuple(out)

    n_full = (qi * tq) // tk
    init = jnp.full((1, tq), NEG_INF, F32)
    tmax0 = scores(n_full, sa_ref, True)

    def body(i, carry):
        ml, tmax = carry

        def even(ml, tmax):
            nxt = scores(n_full - i - 1, sb_ref, False)
            return update(n_full - i, sa_ref, ml, tmax), nxt

        def odd(ml, tmax):
            nxt = scores(n_full - i - 1, sa_ref, False)
            return update(n_full - i, sb_ref, ml, tmax), nxt

        return lax.cond(i % 2 == 0, even, odd, ml, tmax)

    ml, tmax = lax.fori_loop(0, n_full, body, ((init, init), tmax0))
    lax.cond(n_full % 2 == 0,
             lambda ml, tmax: update(0, sa_ref, ml, tmax),
             lambda ml, tmax: update(0, sb_ref, ml, tmax), ml, tmax)

    out = [acc_ref[hh, 0:HEAD_DIM, :] / acc_ref[hh, HEAD_DIM:HEAD_DIM + 1, :] for hh in range(2)]
    o_ref[0] = jnp.concatenate(out, axis=0).T.astype(BF16)


def _attn_call(qt, k, vt, n_pairs, bias_inputs=None):
    b, _, s = qt.shape
    tq, tk = ATT_TILE, KV_TILE
    has_bias = bias_inputs is not None
    in_specs = [
        pl.BlockSpec((1, 2 * LANES, tq), lambda p, bi, qi: (bi, p, qi)),
        pl.BlockSpec((1, s, 2 * LANES), lambda p, bi, qi: (bi, 0, p)),
        pl.BlockSpec((1, s // tk, 2 * V_ROWS, tk), lambda p, bi, qi: (bi, 0, p, 0)),
    ]
    args = [qt, k, vt]
    scratch = [pltpu.VMEM((2, V_ROWS, tq), F32),
               pltpu.VMEM((2, tk, tq), F32), pltpu.VMEM((2, tk, tq), F32)]
    if has_bias:
        bidx, tab = bias_inputs
        in_specs += [pl.BlockSpec(bidx.shape, lambda p, bi, qi: (0, 0, 0)),
                     pl.BlockSpec(memory_space=pltpu.SMEM)]
        args += [bidx, tab]
        scratch.append(pltpu.VMEM((2, N_NEAR + 1, MOBA_BLOCK, MOBA_BLOCK), F32))
    return pl.pallas_call(
        functools.partial(_attn_kernel, has_bias=has_bias, tq=tq, tk=tk),
        grid=(n_pairs, b, s // tq),
        in_specs=in_specs,
        out_specs=pl.BlockSpec((1, tq, 2 * HEAD_DIM), lambda p, bi, qi: (bi, qi, p)),
        out_shape=jax.ShapeDtypeStruct((b, s, n_pairs * 2 * HEAD_DIM), BF16),
        scratch_shapes=scratch,
        compiler_params=pltpu.CompilerParams(dimension_semantics=("arbitrary",) * 3,
                                             vmem_limit_bytes=VMEM_LIMIT),
        name="attn_bias" if has_bias else "attn",
    )(*args)


def _oproj_kernel(x_ref, oa_ref, ob_ref, oc_ref, wa_ref, wb_ref, wc_ref, g_ref, o_ref):
    y = _dot(oa_ref[0], wa_ref[...]) + _dot(ob_ref[0], wb_ref[...]) + _dot(oc_ref[0], wc_ref[...])
    o_ref[0] = x_ref[0] + _rms(y, g_ref[...])


def _oproj_call(x, oa, ob, oc, p, g, tm):
    b, s, d = x.shape

    def rows(width):
        return pl.BlockSpec((1, tm, width), lambda bi, ti: (bi, ti, 0))

    def full(a):
        return pl.BlockSpec(a.shape, lambda bi, ti: (0,) * a.ndim)

    return pl.pallas_call(
        _oproj_kernel,
        grid=(b, s // tm),
        in_specs=[rows(d), rows(W_MOBA), rows(W_MLA), rows(W_FOX),
                  full(p["wo_a"]), full(p["wo_b"]), full(p["wo_c"]), full(g)],
        out_specs=rows(d),
        out_shape=jax.ShapeDtypeStruct(x.shape, F32),
        compiler_params=pltpu.CompilerParams(dimension_semantics=("arbitrary", "arbitrary"),
                                             vmem_limit_bytes=VMEM_LIMIT),
        name="oproj",
    )(x, oa, ob, oc, p["wo_a"], p["wo_b"], p["wo_c"], g)


def _ffn_kernel(x_ref, xp_ref, gpre_ref, wup_ref, cw_ref, cb_ref, wd_ref, gpost_ref,
                o_ref, h_ref, acc_ref, ua_ref, ub_ref, *, tm, nf):
    ti = pl.program_id(1)
    halo = FFN_HALO

    h_ref[halo:, :] = _rms(x_ref[0], gpre_ref[...]).astype(BF16)
    prev = _rms(xp_ref[0], gpre_ref[...])
    h_ref[0:halo, :] = jnp.where(ti == 0, 0.0, prev).astype(BF16)
    acc_ref[...] = jnp.zeros_like(acc_ref)

    def up(f, u_ref):
        h = h_ref[...]
        u_ref[0] = _dot(h, wup_ref[f])
        u_ref[1] = _dot(h, wup_ref[nf + f])

    def conv(u_ref, half, c):
        cw = cw_ref[c]
        return (cw[0:1, :] * u_ref[half, halo - 2:halo - 2 + tm, :]
                + cw[1:2, :] * u_ref[half, halo - 1:halo - 1 + tm, :]
                + cw[2:3, :] * u_ref[half, halo:halo + tm, :]) + cb_ref[c]

    def down(f, u_ref):
        act = (jax.nn.gelu(conv(u_ref, 0, f), approximate=True) * conv(u_ref, 1, nf + f)).astype(BF16)
        acc_ref[...] += _dot(act, wd_ref[f])

    up(0, ua_ref)

    def pair(i, carry):
        up(2 * i + 1, ub_ref)
        down(2 * i, ua_ref)
        up(2 * i + 2, ua_ref)
        down(2 * i + 1, ub_ref)
        return carry

    lax.fori_loop(0, (nf - 1) // 2, pair, 0)
    if nf % 2 == 0:
        up(nf - 1, ub_ref)
        down(nf - 2, ua_ref)
        down(nf - 1, ub_ref)
    else:
        down(nf - 1, ua_ref)

    o_ref[0] = x_ref[0] + _rms(acc_ref[...], gpost_ref[...])


def _ffn_call(x, p, gpre, gpost, tm, tf):
    b, s, d = x.shape
    nf = D_FF // tf
    halo = FFN_HALO
    blocks_per_tile = tm // halo

    def full(a):
        return pl.BlockSpec(a.shape, lambda bi, ti: (0,) * a.ndim)

    in_specs = [
        pl.BlockSpec((1, tm, d), lambda bi, ti: (bi, ti, 0)),
        pl.BlockSpec((1, halo, d), lambda bi, ti: (bi, jnp.maximum(ti * blocks_per_tile - 1, 0), 0)),
        full(gpre), full(p["w_up"]), full(p["conv_w"]), full(p["conv_b"]), full(p["w_down"]), full(gpost),
    ]
    return pl.pallas_call(
        functools.partial(_ffn_kernel, tm=tm, nf=nf),
        grid=(b, s // tm),
        in_specs=in_specs,
        out_specs=pl.BlockSpec((1, tm, d), lambda bi, ti: (bi, ti, 0)),
        out_shape=jax.ShapeDtypeStruct(x.shape, F32),
        scratch_shapes=[pltpu.VMEM((halo + tm, d), BF16), pltpu.VMEM((tm, d), F32),
                        pltpu.VMEM((2, halo + tm, tf), F32), pltpu.VMEM((2, halo + tm, tf), F32)],
        compiler_params=pltpu.CompilerParams(dimension_semantics=("arbitrary",) * 2,
                                             vmem_limit_bytes=VMEM_LIMIT),
        name="ffn",
    )(x, x, gpre, p["w_up"], p["conv_w"], p["conv_b"], p["w_down"], gpost)


def _pad_cols(w, left, total):
    return jnp.pad(w, ((0, 0), (left, total - left - w.shape[1])))


def _prep_layer(w_in, b_f, q_norm, kv_norm, w_uq, w_ukv, w_o, w_up, conv_w, conv_b, w_down):
    offs = [0]
    for sz in PROJ_SIZES:
        offs.append(offs[-1] + sz)
    a_q, a_k, a_v, c_q, c_kv, k_r, f_q, f_k, f_v, f_g = [w_in[:, offs[i]:offs[i + 1]] for i in range(10)]
    half = MLA_ROPE_DIM // 2

    def rot(w):
        return jnp.concatenate([-w[:, half:], w[:, :half]], axis=1)

    w_tok = jnp.concatenate([
        a_k, f_k, c_q, c_kv,
        _pad_cols(k_r, FEAT0, LANES), _pad_cols(rot(k_r), FEAT0, LANES), _pad_cols(f_g, 0, LANES)], axis=1)
    w_t = jnp.concatenate([a_q, f_q, a_v, f_v], axis=1).T
    uq = w_uq.reshape(MLA_Q_RANK, N_HEADS_MLA, MLA_QK_DIM)
    uq_main = jnp.pad(uq, ((0, 0), (0, 0), (0, LANES - MLA_QK_DIM)))
    uq_rope = uq[:, :, MLA_NOPE_DIM:]
    uq_rot = jnp.concatenate([-uq_rope[:, :, half:], uq_rope[:, :, :half]], axis=2)
    uq_rot = jnp.pad(uq_rot, ((0, 0), (0, 0), (MLA_NOPE_DIM, LANES - MLA_QK_DIM)))
    ukv = w_ukv.reshape(MLA_KV_RANK, N_HEADS_MLA, MLA_NOPE_DIM + MLA_V_DIM)
    ukv_k = jnp.pad(ukv[:, :, :MLA_NOPE_DIM], ((0, 0), (0, 0), (0, LANES - MLA_NOPE_DIM)))
    ukv_v = ukv[:, :, MLA_NOPE_DIM:]
    n_chunks = 2 * D_FF // FFN_COLS
    return {
        "w_tok": w_tok.astype(BF16),
        "w_t": w_t.astype(BF16),
        "b_f": _pad_cols(b_f[None, :], 0, LANES),
        "q_norm": q_norm[None, :],
        "kv_norm": kv_norm[None, :],
        "wuq_t": uq_main.reshape(MLA_Q_RANK, N_HEADS_MLA * LANES).T.astype(BF16),
        "wuq_rot_t": uq_rot.reshape(MLA_Q_RANK, N_HEADS_MLA * LANES).T.astype(BF16),
        "wukv_k": ukv_k.reshape(MLA_KV_RANK, N_HEADS_MLA * LANES).astype(BF16),
        "wukv_v_t": ukv_v.reshape(MLA_KV_RANK, W_MLA).T.astype(BF16),
        "wo_a": w_o[:W_MOBA].astype(BF16),
        "wo_b": w_o[W_MOBA:W_MOBA + W_MLA].astype(BF16),
        "wo_c": w_o[W_MOBA + W_MLA:].astype(BF16),
        "w_up": w_up.reshape(-1, n_chunks, FFN_COLS).transpose(1, 0, 2).astype(BF16),
        "conv_w": conv_w.reshape(CONV_WIDTH, n_chunks, FFN_COLS).transpose(1, 0, 2),
        "conv_b": conv_b.reshape(n_chunks, 1, FFN_COLS),
        "w_down": w_down.reshape(D_FF // FFN_COLS, FFN_COLS, -1).astype(BF16),
    }


def _rope_tables(s):
    half = MLA_ROPE_DIM // 2
    inv = ROPE_THETA ** (-jnp.arange(half, dtype=F32) / half)
    ang = jnp.arange(s).astype(F32)[:, None] * inv[None, :]
    cos = jnp.concatenate([jnp.cos(ang)] * 2, axis=1)
    sin = jnp.concatenate([jnp.sin(ang)] * 2, axis=1)
    ctok = _pad_cols(cos, FEAT0, LANES)
    stok = _pad_cols(sin, FEAT0, LANES)
    ones = jnp.ones((s, MLA_NOPE_DIM), F32)
    ct = jnp.concatenate([ones, cos, jnp.zeros((s, LANES - MLA_QK_DIM), F32)], axis=1).T
    st = stok.T
    return ctok, stok, ct, st


def _t5_bucket_tiles(tile):
    d = jnp.arange(N_NEAR)[:, None, None]
    j = jnp.arange(tile)[None, :, None]
    i = jnp.arange(tile)[None, None, :]
    n = jnp.maximum(d * tile + i - j, 0)
    exact = T5_BUCKETS // 2
    large = exact + (jnp.log(jnp.maximum(n, 1).astype(F32) / exact)
                     / math.log(T5_MAX_DIST / exact) * (T5_BUCKETS - exact)).astype(jnp.int32)
    return jnp.where(n < exact, n, jnp.minimum(large, T5_BUCKETS - 1)).astype(jnp.int32)


def kernel(x, rel_bias, ln_mix_pre, ln_mix_post, ln_ffn_pre, ln_ffn_post, w_in, b_f, q_norm, kv_norm, w_uq,
           w_ukv, w_o, w_up, conv_w, conv_b, w_down):
    b, s, d = x.shape
    depth = w_in.shape[0]
    assert s % PROJ_ROWS == 0 and s % FFN_ROWS == 0 and s % OPROJ_ROWS == 0
    assert s // MOBA_BLOCK <= NBLK_PAD and D_FF % FFN_COLS == 0
    tables = _rope_tables(s)
    bidx = _t5_bucket_tiles(MOBA_BLOCK)
    far = rel_bias[T5_BUCKETS - 1, :]
    for l in range(depth):
        p = _prep_layer(w_in[l], b_f[l], q_norm[l], kv_norm[l], w_uq[l], w_ukv[l], w_o[l], w_up[l],
                        conv_w[l], conv_b[l], w_down[l])
        qat, ka, vat, qft, kf, vft, qmt, km, vmt = _proj_call(x, ln_mix_pre[l][None, :], p, tables, far, PROJ_ROWS)
        oa = _attn_call(qat, ka, vat, N_HEADS_MOBA // 2, (bidx, rel_bias))
        ob = _attn_call(qmt, km, vmt, N_HEADS_MLA // 2)
        oc = _attn_call(qft, kf, vft, N_HEADS_FOX // 2)
        x = _oproj_call(x, oa, ob, oc, p, ln_mix_post[l][None, :], OPROJ_ROWS)
        x = _ffn_call(x, p, ln_ffn_pre[l][None, :], ln_ffn_post[l][None, :], FFN_ROWS, FFN_COLS)
    return x
```

```python
import functools
import math

import jax
import jax.numpy as jnp
from jax import lax
from jax.experimental import pallas as pl
from jax.experimental.pallas import tpu as pltpu

F32 = jnp.float32
BF16 = jnp.bfloat16

HEAD_DIM = 64
N_HEADS_MOBA = 6
N_HEADS_MLA = 4
N_HEADS_FOX = 6
MOBA_BLOCK = 256
MOBA_TOPK = 3
MLA_Q_RANK = 256
MLA_KV_RANK = 128
MLA_NOPE_DIM = 64
MLA_ROPE_DIM = 32
MLA_V_DIM = 64
MLA_QK_DIM = MLA_NOPE_DIM + MLA_ROPE_DIM
ROPE_THETA = 10000.0
T5_BUCKETS = 32
T5_MAX_DIST = 1024
D_FF = 2816
CONV_WIDTH = 3
NORM_EPS = 1e-6
NEG_INF = -1e30
W_MOBA = N_HEADS_MOBA * HEAD_DIM
W_MLA = N_HEADS_MLA * MLA_V_DIM
W_FOX = N_HEADS_FOX * HEAD_DIM
PROJ_SIZES = (W_MOBA, W_MOBA, W_MOBA, MLA_Q_RANK, MLA_KV_RANK, MLA_ROPE_DIM, W_FOX, W_FOX, W_FOX, N_HEADS_FOX)

LANES = 128
SUBLANES = 8
VMEM_LIMIT = 56 * 1024 * 1024

ATT_TILE = 512
KV_TILE = 512
PROJ_ROWS = KV_TILE
OPROJ_ROWS = 512
FFN_ROWS = 512
FFN_COLS = 256
FFN_HALO = 16
FEAT0 = HEAD_DIM
V_ROWS = HEAD_DIM + 16
LOG2E = math.log2(math.e)
NBLK_PAD = 16
N_NEAR = (T5_MAX_DIST + MOBA_BLOCK - 1 + MOBA_BLOCK - 1) // MOBA_BLOCK

_TOK_AK = (0, 384)
_TOK_FK = (384, 768)
_TOK_CQ = (768, 1024)
_TOK_CKV = (1024, 1152)
_TOK_KR = (1152, 1280)
_TOK_KRR = (1280, 1408)
_TOK_FG = (1408, 1536)
_TOK_COLS = 1536
_T_AQ = (0, 384)
_T_FQ = (384, 768)
_T_AV = (768, 1152)
_T_FV = (1152, 1536)


def _rms(xf, g):
    return xf * lax.rsqrt(jnp.mean(xf * xf, axis=-1, keepdims=True) + NORM_EPS) * g


def _dot(a, b):
    return jnp.dot(a, b, preferred_element_type=F32)


def _dot_nt(a, b):
    return lax.dot_general(a, b, (((1,), (1,)), ((), ())), preferred_element_type=F32)


def _bf16_pieces(x):
    hi = x.astype(BF16).astype(F32)
    r = x - hi
    mid = r.astype(BF16).astype(F32)
    lo = (r - mid).astype(BF16).astype(F32)
    return hi, mid, lo


def _dot_3pass(a, b):
    a_hi = a.astype(BF16)
    a_lo = (a - a_hi.astype(F32)).astype(BF16)
    b_hi = b.astype(BF16)
    b_lo = (b - b_hi.astype(F32)).astype(BF16)
    return _dot(a_hi, b_hi) + (_dot(a_hi, b_lo) + _dot(a_lo, b_hi))


def _head_slot(arr, hd):
    pair = arr[:, (hd // 2) * LANES:(hd // 2 + 1) * LANES]
    if hd % 2:
        pair = pltpu.roll(pair, HEAD_DIM, axis=1)
    return pair


def _proj_kernel(x_ref, g_ref, wtok_ref, wt_ref, bf_ref, qn_ref, kvn_ref, wuq_ref, wuqr_ref,
                 wukvk_ref, wukvv_ref, ctok_ref, stok_ref, ct_ref, st_ref, far_ref,
                 qat_ref, ka_ref, vat_ref, qft_ref, kf_ref, vft_ref, qmt_ref, km_ref, vmt_ref,
                 kmean_ref, fcarry_ref, *, tm):
    t = pl.program_id(1)
    n_sub = tm // MOBA_BLOCK

    @pl.when(t == 0)
    def _():
        kmean_ref[...] = jnp.zeros_like(kmean_ref)
        fcarry_ref[...] = jnp.zeros_like(fcarry_ref)

    h = _rms(x_ref[0], g_ref[...]).astype(BF16)
    lane = lax.broadcasted_iota(jnp.int32, (tm, LANES), 1)
    row = lax.broadcasted_iota(jnp.int32, (tm, LANES), 0)
    blk0 = t * n_sub
    is_head = lane < HEAD_DIM

    def w_tok(rng):
        return wtok_ref[:, rng[0]:rng[1]]

    def w_t(rng):
        return wt_ref[rng[0]:rng[1], :]

    sub8v = lax.broadcasted_iota(jnp.int32, (SUBLANES, tm), 0)
    ones_row = jnp.where(sub8v == 0, 1.0, 0.0)
    v_pad = jnp.zeros((V_ROWS - HEAD_DIM - SUBLANES, tm), F32)

    def store_vt(ref, vt):
        n_heads = vt.shape[0] // HEAD_DIM
        ref[0, 0] = jnp.concatenate(
            [piece for hd in range(n_heads)
             for piece in (vt[hd * HEAD_DIM:(hd + 1) * HEAD_DIM, :], ones_row, v_pad)], axis=0).astype(BF16)

    ak = _dot(h, w_tok(_TOK_AK))
    blk_row = blk0 + lax.shift_right_logical(row, int(math.log2(MOBA_BLOCK)))
    k_feat = jnp.where(lane == FEAT0 + blk_row, 1.0, 0.0)
    for hd in range(N_HEADS_MOBA):
        ka_ref[0, :, hd * LANES:(hd + 1) * LANES] = jnp.where(is_head, _head_slot(ak, hd), k_feat).astype(BF16)

    lane_k = lax.broadcasted_iota(jnp.int32, (1, W_MOBA), 1)
    for bi in range(n_sub):
        mean_row = jnp.mean(ak[bi * MOBA_BLOCK:(bi + 1) * MOBA_BLOCK, :], axis=0, keepdims=True)
        for hd in range(N_HEADS_MOBA):
            in_head = (lane_k >= hd * HEAD_DIM) & (lane_k < (hd + 1) * HEAD_DIM)
            kmean_ref[pl.ds(hd * NBLK_PAD + blk0 + bi, 1), :] = jnp.where(in_head, mean_row, 0.0)

    qat = _dot_nt(w_t(_T_AQ), h) * (HEAD_DIM ** -0.5 * LOG2E)
    gate = _dot_3pass(kmean_ref[...], qat)
    own = blk0 + lax.shift_right_logical(
        lax.broadcasted_iota(jnp.int32, (NBLK_PAD, tm), 1), int(math.log2(MOBA_BLOCK)))
    n_io = lax.broadcasted_iota(jnp.int32, (NBLK_PAD, tm), 0)
    past = n_io < own
    zeros_tail = jnp.zeros((LANES - HEAD_DIM - NBLK_PAD, tm), F32)
    for hd in range(N_HEADS_MOBA):
        g = jnp.where(past, gate[hd * NBLK_PAD:(hd + 1) * NBLK_PAD, :], NEG_INF)
        rank = jnp.zeros((NBLK_PAD, tm), jnp.int32)
        for n2 in range(NBLK_PAD):
            r = g[n2:n2 + 1, :]
            tie = jnp.where(n2 < n_io, 1, 0)
            rank = rank + jnp.where(r > g, 1, jnp.where(r == g, tie, 0))
        sel = jnp.where(past, rank, MOBA_TOPK) < MOBA_TOPK
        far = jnp.where(own - n_io >= N_NEAR, far_ref[hd] * LOG2E, 0.0)
        q_feat = jnp.where(sel | (n_io == own), far, NEG_INF)
        qat_ref[0, 0, hd * LANES:(hd + 1) * LANES, :] = jnp.concatenate(
            [qat[hd * HEAD_DIM:(hd + 1) * HEAD_DIM, :], q_feat, zeros_tail], axis=0).astype(BF16)
    store_vt(vat_ref, _dot_nt(w_t(_T_AV), h))

    fg = _dot(h, w_tok(_TOK_FG)) + bf_ref[...]
    logf = jnp.minimum(fg, 0.0) - jnp.log1p(jnp.exp(-jnp.abs(fg)))
    csum = jnp.where(lane < N_HEADS_FOX, logf, 0.0)
    sft = 1
    while sft < tm:
        csum = csum + jnp.where(row >= sft, pltpu.roll(csum, sft, axis=0), 0.0)
        sft *= 2
    decay = csum + fcarry_ref[0:1, :]
    fcarry_ref[0:1, :] = decay[tm - 1:tm, :]
    decay = decay * LOG2E
    decay_t = decay.T

    fk = _dot(h, w_tok(_TOK_FK))
    fqt = _dot_nt(w_t(_T_FQ), h) * (HEAD_DIM ** -0.5 * LOG2E)
    sub8 = lax.broadcasted_iota(jnp.int32, (SUBLANES, tm), 0)
    zeros_tail_f = jnp.zeros((LANES - HEAD_DIM - SUBLANES, tm), F32)
    for hd in range(N_HEADS_FOX):
        fcol = jnp.broadcast_to(decay[:, hd:hd + 1], (tm, LANES))
        hi, mid, lo = _bf16_pieces(fcol)
        k_feat_f = jnp.where(lane < FEAT0 + 3, 1.0,
                             jnp.where(lane == FEAT0 + 3, -hi,
                                       jnp.where(lane == FEAT0 + 4, -mid,
                                                 jnp.where(lane == FEAT0 + 5, -lo, 0.0))))
        kf_ref[0, :, hd * LANES:(hd + 1) * LANES] = jnp.where(is_head, _head_slot(fk, hd), k_feat_f).astype(BF16)
        hi, mid, lo = _bf16_pieces(decay_t[hd:hd + 1, :])
        q_feat_f = jnp.where(sub8 == 0, hi,
                             jnp.where(sub8 == 1, mid,
                                       jnp.where(sub8 == 2, lo,
                                                 jnp.where(sub8 < 6, 1.0, 0.0))))
        qft_ref[0, 0, hd * LANES:(hd + 1) * LANES, :] = jnp.concatenate(
            [fqt[hd * HEAD_DIM:(hd + 1) * HEAD_DIM, :], q_feat_f, zeros_tail_f], axis=0).astype(BF16)
    store_vt(vft_ref, _dot_nt(w_t(_T_FV), h))

    cqn = _rms(_dot(h, w_tok(_TOK_CQ)), qn_ref[...]).astype(BF16)
    qm = _dot_nt(wuq_ref[...], cqn)
    qmr = _dot_nt(wuqr_ref[...], cqn)
    kvn = _rms(_dot(h, w_tok(_TOK_CKV)), kvn_ref[...]).astype(BF16)
    k_nope = _dot(kvn, wukvk_ref[...])
    k_rope = _dot(h, w_tok(_TOK_KR)) * ctok_ref[...] + _dot(h, w_tok(_TOK_KRR)) * stok_ref[...]
    cos_t = ct_ref[...]
    sin_t = st_ref[...]
    for hd in range(N_HEADS_MLA):
        sl = slice(hd * LANES, (hd + 1) * LANES)
        qmt_ref[0, 0, sl, :] = ((qm[sl, :] * cos_t + qmr[sl, :] * sin_t) * (MLA_QK_DIM ** -0.5 * LOG2E)).astype(BF16)
        km_ref[0, :, sl] = (k_nope[:, sl] + k_rope).astype(BF16)
    store_vt(vmt_ref, _dot_nt(wukvv_ref[...], kvn))


def _proj_call(x, g, p, tables, far, tm):
    b, s, d = x.shape
    nt = s // tm
    n_sub = tm // MOBA_BLOCK
    nq = s // ATT_TILE

    def full(a):
        return pl.BlockSpec(a.shape, lambda bi, ti: (0,) * a.ndim)

    ctok, stok, ct, st = tables
    in_specs = [
        pl.BlockSpec((1, tm, d), lambda bi, ti: (bi, ti, 0)),
        full(g), full(p["w_tok"]), full(p["w_t"]), full(p["b_f"]), full(p["q_norm"]), full(p["kv_norm"]),
        full(p["wuq_t"]), full(p["wuq_rot_t"]), full(p["wukv_k"]), full(p["wukv_v_t"]),
        pl.BlockSpec((tm, LANES), lambda bi, ti: (ti, 0)),
        pl.BlockSpec((tm, LANES), lambda bi, ti: (ti, 0)),
        pl.BlockSpec((LANES, tm), lambda bi, ti: (0, ti)),
        pl.BlockSpec((LANES, tm), lambda bi, ti: (0, ti)),
        pl.BlockSpec(memory_space=pltpu.SMEM),
    ]

    def qt_spec(n_heads):
        return pl.BlockSpec((1, 1, n_heads * LANES, tm), lambda bi, ti: (bi, ti, 0, 0))

    def k_spec(n_heads):
        return pl.BlockSpec((1, tm, n_heads * LANES), lambda bi, ti: (bi, ti, 0))

    def vt_spec(width):
        return pl.BlockSpec((1, 1, width, tm), lambda bi, ti: (bi, ti, 0, 0))

    def qt_shape(n_heads):
        return jax.ShapeDtypeStruct((b, nt, n_heads * LANES, tm), BF16)

    def k_shape(n_heads):
        return jax.ShapeDtypeStruct((b, s, n_heads * LANES), BF16)

    def vt_shape(width):
        return jax.ShapeDtypeStruct((b, nt, width, tm), BF16)

    out_specs = [qt_spec(6), k_spec(6), vt_spec(6 * V_ROWS), qt_spec(6), k_spec(6), vt_spec(6 * V_ROWS),
                 qt_spec(4), k_spec(4), vt_spec(4 * V_ROWS)]
    out_shape = [qt_shape(6), k_shape(6), vt_shape(6 * V_ROWS), qt_shape(6), k_shape(6), vt_shape(6 * V_ROWS),
                 qt_shape(4), k_shape(4), vt_shape(4 * V_ROWS)]
    return pl.pallas_call(
        functools.partial(_proj_kernel, tm=tm),
        grid=(b, nt),
        in_specs=in_specs,
        out_specs=out_specs,
        out_shape=out_shape,
        scratch_shapes=[pltpu.VMEM((N_HEADS_MOBA * NBLK_PAD + 32, W_MOBA), F32),
                        pltpu.VMEM((SUBLANES, LANES), F32)],
        compiler_params=pltpu.CompilerParams(dimension_semantics=("arbitrary", "arbitrary"),
                                             vmem_limit_bytes=VMEM_LIMIT),
        name="proj",
    )(x, g, p["w_tok"], p["w_t"], p["b_f"], p["q_norm"], p["kv_norm"], p["wuq_t"], p["wuq_rot_t"],
      p["wukv_k"], p["wukv_v_t"], ctok, stok, ct, st, far)


def _attn_kernel(*refs, has_bias, tq, tk, nq, n_add, n_plain):
    if has_bias:
        (q_ref, j_ref, qt_ref, k_ref, vt_ref, bidx_ref, tab_ref, o_ref,
         acc_ref, m_ref, s0_ref, s1_ref, s2_ref, s3_ref, bias_ref) = refs
    else:
        (q_ref, j_ref, qt_ref, k_ref, vt_ref, o_ref,
         acc_ref, m_ref, s0_ref, s1_ref, s2_ref, s3_ref, bias_ref) = refs
    hp = pl.program_id(0)
    bi = pl.program_id(1)
    blk = MOBA_BLOCK
    n_near = N_NEAR if has_bias else 1

    @pl.when(bi == 0)
    def _():
        tri = (lax.broadcasted_iota(jnp.int32, (blk, blk), 0) <= lax.broadcasted_iota(jnp.int32, (blk, blk), 1))
        for hh in range(2):
            bias_ref[hh, 0] = jnp.full((blk, blk), NEG_INF, F32)
            bias_ref[hh, n_near + 1] = jnp.zeros((blk, blk), F32)
            if has_bias:
                def build(d, carry, hh=hh):
                    idx = bidx_ref[d]
                    bias_ref[hh, d + 1] = lax.fori_loop(
                        0, T5_BUCKETS,
                        lambda bk, tl: jnp.where(idx == bk, tab_ref[bk, hp * 2 + hh] * LOG2E, tl),
                        jnp.zeros((blk, blk), F32))
                    return carry
                lax.fori_loop(0, n_near, build, 0)
                bias_ref[hh, 1] = jnp.where(tri, bias_ref[hh, 1], NEG_INF)
            else:
                bias_ref[hh, 1] = jnp.where(tri, 0.0, NEG_INF)

    acc_ref[...] = jnp.zeros_like(acc_ref)
    m_ref[...] = jnp.full(m_ref.shape, NEG_INF, F32)
    bufs = (s0_ref, s1_ref, s2_ref, s3_ref)

    n_items = n_add + n_plain

    def scores(e, s_ref, additive):
        qi = q_ref[e]
        j = j_ref[e]
        start = pl.multiple_of(j * tk, tk)
        tmax = []
        for hh in range(2):
            k = k_ref[0, pl.ds(start, tk), hh * LANES:(hh + 1) * LANES]
            s = _dot(k, qt_ref[0, qi, hh * LANES:(hh + 1) * LANES, :])
            if additive:
                s = s + jnp.concatenate([
                    jnp.concatenate([
                        bias_ref[hh, jnp.clip((qi * (tq // blk) + c) - (j * (tk // blk) + r), -1, n_near) + 1]
                        for c in range(tq // blk)], axis=1)
                    for r in range(tk // blk)], axis=0)
            s_ref[hh] = s
            tmax.append(jnp.max(s, axis=0, keepdims=True))
        return tuple(tmax)

    def update(e, s_ref, tmax):
        qi = q_ref[e]
        j = j_ref[e]
        for hh in range(2):
            m_old = m_ref[qi, hh, 0:1, :]
            m_new = jnp.maximum(m_old, tmax[hh])
            alpha = jnp.exp2(m_old - m_new)
            p = jnp.exp2(s_ref[hh] - m_new).astype(BF16)
            v = vt_ref[0, j, hh * V_ROWS:(hh + 1) * V_ROWS, :]
            acc_ref[qi, hh] = alpha * acc_ref[qi, hh] + _dot(v, p)
            m_ref[qi, hh, 0:1, :] = m_new

    def step(e, r, carry, additive):
        tmax_cur, tmax_nxt = carry
        tmax_new = scores(e + 2, bufs[(r + 2) % 4], additive)
        update(e, bufs[r], tmax_cur)
        return tmax_nxt, tmax_new

    def steps(lo, hi, carry, additive):
        n_groups = max(hi - lo, 0) // 4

        def group(g, carry):
            for r in range(4):
                carry = step(lo + 4 * g + r, (lo + r) % 4, carry, additive)
            return carry

        carry = lax.fori_loop(0, n_groups, group, carry)
        for e in range(lo + 4 * n_groups, hi):
            carry = step(e, e % 4, carry, additive)
        return carry

    carry = (scores(0, bufs[0], 0 < n_add), scores(1, bufs[1], 1 < n_add))
    split = max(n_add - 2, 0)
    carry = steps(0, split, carry, True)
    carry = steps(split, n_items - 2, carry, False)
    update(n_items - 2, bufs[(n_items - 2) % 4], carry[0])
    update(n_items - 1, bufs[(n_items - 1) % 4], carry[1])

    def finish(qi, carry):
        o = [acc_ref[qi, hh, 0:HEAD_DIM, :] / acc_ref[qi, hh, HEAD_DIM:HEAD_DIM + 1, :] for hh in range(2)]
        o_ref[0, pl.ds(pl.multiple_of(qi * tq, tq), tq), :] = jnp.concatenate(o, axis=0).T.astype(BF16)
        return carry

    lax.fori_loop(0, nq, finish, 0)


def _attn_schedule(nq, tq, tk, n_near):
    blk = MOBA_BLOCK
    add, plain = [], []
    for q in range(nq):
        for j in range(q, -1, -1):
            min_dist = q * (tq // blk) - (j * (tk // blk) + tk // blk - 1)
            (add if min_dist < n_near else plain).append((q, j))
    items = add + plain
    return (jnp.asarray([q for q, _ in items], jnp.int32), jnp.asarray([j for _, j in items], jnp.int32),
            len(add), len(plain))


def _attn_call(qt, k, vt, n_pairs, bias_inputs=None):
    b, nq, _, tq = qt.shape
    _, s, _ = k.shape
    tk = vt.shape[-1]
    assert tq == tk, "the causal-triangle tile assumes equal query and key tiles"
    has_bias = bias_inputs is not None
    q_tab, j_tab, n_add, n_plain = _attn_schedule(nq, tq, tk, N_NEAR if has_bias else 1)
    assert n_add + n_plain >= 2
    smem = pl.BlockSpec(memory_space=pltpu.SMEM)
    in_specs = [
        smem, smem,
        pl.BlockSpec((1, nq, 2 * LANES, tq), lambda p, bi: (bi, 0, p, 0)),
        pl.BlockSpec((1, s, 2 * LANES), lambda p, bi: (bi, 0, p)),
        pl.BlockSpec((1, s // tk, 2 * V_ROWS, tk), lambda p, bi: (bi, 0, p, 0)),
    ]
    args = [q_tab, j_tab, qt, k, vt]
    n_tiles = (N_NEAR if has_bias else 1) + 2
    scratch = [pltpu.VMEM((nq, 2, V_ROWS, tq), F32), pltpu.VMEM((nq, 2, SUBLANES, tq), F32),
               pltpu.VMEM((2, tk, tq), F32), pltpu.VMEM((2, tk, tq), F32),
               pltpu.VMEM((2, tk, tq), F32), pltpu.VMEM((2, tk, tq), F32),
               pltpu.VMEM((2, n_tiles, MOBA_BLOCK, MOBA_BLOCK), F32)]
    if has_bias:
        bidx, tab = bias_inputs
        in_specs += [pl.BlockSpec(bidx.shape, lambda p, bi: (0, 0, 0)), smem]
        args += [bidx, tab]
    return pl.pallas_call(
        functools.partial(_attn_kernel, has_bias=has_bias, tq=tq, tk=tk, nq=nq, n_add=n_add, n_plain=n_plain),
        grid=(n_pairs, b),
        in_specs=in_specs,
        out_specs=pl.BlockSpec((1, s, 2 * HEAD_DIM), lambda p, bi: (bi, 0, p)),
        out_shape=jax.ShapeDtypeStruct((b, s, n_pairs * 2 * HEAD_DIM), BF16),
        scratch_shapes=scratch,
        compiler_params=pltpu.CompilerParams(dimension_semantics=("arbitrary",) * 2,
                                             vmem_limit_bytes=VMEM_LIMIT),
        name="attn_bias" if has_bias else "attn",
    )(*args)


def _oproj_kernel(x_ref, oa_ref, ob_ref, oc_ref, wa_ref, wb_ref, wc_ref, g_ref, o_ref):
    y = _dot(oa_ref[0], wa_ref[...]) + _dot(ob_ref[0], wb_ref[...]) + _dot(oc_ref[0], wc_ref[...])
    o_ref[0] = x_ref[0] + _rms(y, g_ref[...])


def _oproj_call(x, oa, ob, oc, p, g, tm):
    b, s, d = x.shape

    def rows(width):
        return pl.BlockSpec((1, tm, width), lambda bi, ti: (bi, ti, 0))

    def full(a):
        return pl.BlockSpec(a.shape, lambda bi, ti: (0,) * a.ndim)

    return pl.pallas_call(
        _oproj_kernel,
        grid=(b, s // tm),
        in_specs=[rows(d), rows(W_MOBA), rows(W_MLA), rows(W_FOX),
                  full(p["wo_a"]), full(p["wo_b"]), full(p["wo_c"]), full(g)],
        out_specs=rows(d),
        out_shape=jax.ShapeDtypeStruct(x.shape, F32),
        compiler_params=pltpu.CompilerParams(dimension_semantics=("arbitrary", "arbitrary"),
                                             vmem_limit_bytes=VMEM_LIMIT),
        name="oproj",
    )(x, oa, ob, oc, p["wo_a"], p["wo_b"], p["wo_c"], g)


def _ffn_kernel(x_ref, xp_ref, gpre_ref, wup_ref, cw_ref, cb_ref, wd_ref, gpost_ref,
                o_ref, h_ref, acc_ref, ua_ref, ub_ref, *, tm, nf):
    ti = pl.program_id(1)
    halo = FFN_HALO

    h_ref[halo:, :] = _rms(x_ref[0], gpre_ref[...]).astype(BF16)
    prev = _rms(xp_ref[0], gpre_ref[...])
    h_ref[0:halo, :] = jnp.where(ti == 0, 0.0, prev).astype(BF16)
    acc_ref[...] = jnp.zeros_like(acc_ref)

    def up(f, u_ref):
        h = h_ref[...]
        u_ref[0] = _dot(h, wup_ref[f])
        u_ref[1] = _dot(h, wup_ref[nf + f])

    def conv(u_ref, half, c):
        cw = cw_ref[c]
        return (cw[0:1, :] * u_ref[half, halo - 2:halo - 2 + tm, :]
                + cw[1:2, :] * u_ref[half, halo - 1:halo - 1 + tm, :]
                + cw[2:3, :] * u_ref[half, halo:halo + tm, :]) + cb_ref[c]

    def down(f, u_ref):
        act = (jax.nn.gelu(conv(u_ref, 0, f), approximate=True) * conv(u_ref, 1, nf + f)).astype(BF16)
        acc_ref[...] += _dot(act, wd_ref[f])

    up(0, ua_ref)

    def pair(i, carry):
        up(2 * i + 1, ub_ref)
        down(2 * i, ua_ref)
        up(2 * i + 2, ua_ref)
        down(2 * i + 1, ub_ref)
        return carry

    lax.fori_loop(0, (nf - 1) // 2, pair, 0)
    if nf % 2 == 0:
        up(nf - 1, ub_ref)
        down(nf - 2, ua_ref)
        down(nf - 1, ub_ref)
    else:
        down(nf - 1, ua_ref)

    o_ref[0] = x_ref[0] + _rms(acc_ref[...], gpost_ref[...])


def _ffn_call(x, p, gpre, gpost, tm, tf):
    b, s, d = x.shape
    nf = D_FF // tf
    halo = FFN_HALO
    blocks_per_tile = tm // halo

    def full(a):
        return pl.BlockSpec(a.shape, lambda bi, ti: (0,) * a.ndim)

    in_specs = [
        pl.BlockSpec((1, tm, d), lambda bi, ti: (bi, ti, 0)),
        pl.BlockSpec((1, halo, d), lambda bi, ti: (bi, jnp.maximum(ti * blocks_per_tile - 1, 0), 0)),
        full(gpre), full(p["w_up"]), full(p["conv_w"]), full(p["conv_b"]), full(p["w_down"]), full(gpost),
    ]
    return pl.pallas_call(
        functools.partial(_ffn_kernel, tm=tm, nf=nf),
        grid=(b, s // tm),
        in_specs=in_specs,
        out_specs=pl.BlockSpec((1, tm, d), lambda bi, ti: (bi, ti, 0)),
        out_shape=jax.ShapeDtypeStruct(x.shape, F32),
        scratch_shapes=[pltpu.VMEM((halo + tm, d), BF16), pltpu.VMEM((tm, d), F32),
                        pltpu.VMEM((2, halo + tm, tf), F32), pltpu.VMEM((2, halo + tm, tf), F32)],
        compiler_params=pltpu.CompilerParams(dimension_semantics=("arbitrary",) * 2,
                                             vmem_limit_bytes=VMEM_LIMIT),
        name="ffn",
    )(x, x, gpre, p["w_up"], p["conv_w"], p["conv_b"], p["w_down"], gpost)


def _pad_cols(w, left, total):
    return jnp.pad(w, ((0, 0), (left, total - left - w.shape[1])))


def _prep_layer(w_in, b_f, q_norm, kv_norm, w_uq, w_ukv, w_o, w_up, conv_w, conv_b, w_down):
    offs = [0]
    for sz in PROJ_SIZES:
        offs.append(offs[-1] + sz)
    a_q, a_k, a_v, c_q, c_kv, k_r, f_q, f_k, f_v, f_g = [w_in[:, offs[i]:offs[i + 1]] for i in range(10)]
    half = MLA_ROPE_DIM // 2

    def rot(w):
        return jnp.concatenate([-w[:, half:], w[:, :half]], axis=1)

    w_tok = jnp.concatenate([
        a_k, f_k, c_q, c_kv,
        _pad_cols(k_r, FEAT0, LANES), _pad_cols(rot(k_r), FEAT0, LANES), _pad_cols(f_g, 0, LANES)], axis=1)
    w_t = jnp.concatenate([a_q, f_q, a_v, f_v], axis=1).T
    uq = w_uq.reshape(MLA_Q_RANK, N_HEADS_MLA, MLA_QK_DIM)
    uq_main = jnp.pad(uq, ((0, 0), (0, 0), (0, LANES - MLA_QK_DIM)))
    uq_rope = uq[:, :, MLA_NOPE_DIM:]
    uq_rot = jnp.concatenate([-uq_rope[:, :, half:], uq_rope[:, :, :half]], axis=2)
    uq_rot = jnp.pad(uq_rot, ((0, 0), (0, 0), (MLA_NOPE_DIM, LANES - MLA_QK_DIM)))
    ukv = w_ukv.reshape(MLA_KV_RANK, N_HEADS_MLA, MLA_NOPE_DIM + MLA_V_DIM)
    ukv_k = jnp.pad(ukv[:, :, :MLA_NOPE_DIM], ((0, 0), (0, 0), (0, LANES - MLA_NOPE_DIM)))
    ukv_v = ukv[:, :, MLA_NOPE_DIM:]
    n_chunks = 2 * D_FF // FFN_COLS
    return {
        "w_tok": w_tok.astype(BF16),
        "w_t": w_t.astype(BF16),
        "b_f": _pad_cols(b_f[None, :], 0, LANES),
        "q_norm": q_norm[None, :],
        "kv_norm": kv_norm[None, :],
        "wuq_t": uq_main.reshape(MLA_Q_RANK, N_HEADS_MLA * LANES).T.astype(BF16),
        "wuq_rot_t": uq_rot.reshape(MLA_Q_RANK, N_HEADS_MLA * LANES).T.astype(BF16),
        "wukv_k": ukv_k.reshape(MLA_KV_RANK, N_HEADS_MLA * LANES).astype(BF16),
        "wukv_v_t": ukv_v.reshape(MLA_KV_RANK, W_MLA).T.astype(BF16),
        "wo_a": w_o[:W_MOBA].astype(BF16),
        "wo_b": w_o[W_MOBA:W_MOBA + W_MLA].astype(BF16),
        "wo_c": w_o[W_MOBA + W_MLA:].astype(BF16),
        "w_up": w_up.reshape(-1, n_chunks, FFN_COLS).transpose(1, 0, 2).astype(BF16),
        "conv_w": conv_w.reshape(CONV_WIDTH, n_chunks, FFN_COLS).transpose(1, 0, 2),
        "conv_b": conv_b.reshape(n_chunks, 1, FFN_COLS),
        "w_down": w_down.reshape(D_FF // FFN_COLS, FFN_COLS, -1).astype(BF16),
    }


def _rope_tables(s):
    half = MLA_ROPE_DIM // 2
    inv = ROPE_THETA ** (-jnp.arange(half, dtype=F32) / half)
    ang = jnp.arange(s).astype(F32)[:, None] * inv[None, :]
    cos = jnp.concatenate([jnp.cos(ang)] * 2, axis=1)
    sin = jnp.concatenate([jnp.sin(ang)] * 2, axis=1)
    ctok = _pad_cols(cos, FEAT0, LANES)
    stok = _pad_cols(sin, FEAT0, LANES)
    ones = jnp.ones((s, MLA_NOPE_DIM), F32)
    ct = jnp.concatenate([ones, cos, jnp.zeros((s, LANES - MLA_QK_DIM), F32)], axis=1).T
    st = stok.T
    return ctok, stok, ct, st


def _t5_bucket_tiles(tile):
    d = jnp.arange(N_NEAR)[:, None, None]
    j = jnp.arange(tile)[None, :, None]
    i = jnp.arange(tile)[None, None, :]
    n = jnp.maximum(d * tile + i - j, 0)
    exact = T5_BUCKETS // 2
    large = exact + (jnp.log(jnp.maximum(n, 1).astype(F32) / exact)
                     / math.log(T5_MAX_DIST / exact) * (T5_BUCKETS - exact)).astype(jnp.int32)
    return jnp.where(n < exact, n, jnp.minimum(large, T5_BUCKETS - 1)).astype(jnp.int32)


def kernel(x, rel_bias, ln_mix_pre, ln_mix_post, ln_ffn_pre, ln_ffn_post, w_in, b_f, q_norm, kv_norm, w_uq,
           w_ukv, w_o, w_up, conv_w, conv_b, w_down):
    b, s, d = x.shape
    depth = w_in.shape[0]
    assert s % PROJ_ROWS == 0 and s % FFN_ROWS == 0 and s % OPROJ_ROWS == 0
    assert s // MOBA_BLOCK <= NBLK_PAD and D_FF % FFN_COLS == 0
    tables = _rope_tables(s)
    bidx = _t5_bucket_tiles(MOBA_BLOCK)
    far = rel_bias[T5_BUCKETS - 1, :]
    for l in range(depth):
        p = _prep_layer(w_in[l], b_f[l], q_norm[l], kv_norm[l], w_uq[l], w_ukv[l], w_o[l], w_up[l],
                        conv_w[l], conv_b[l], w_down[l])
        qat, ka, vat, qft, kf, vft, qmt, km, vmt = _proj_call(x, ln_mix_pre[l][None, :], p, tables, far, PROJ_ROWS)
        oa = _attn_call(qat, ka, vat, N_HEADS_MOBA // 2, (bidx, rel_bias))
        ob = _attn_call(qmt, km, vmt, N_HEADS_MLA // 2)
        oc = _attn_call(qft, kf, vft, N_HEADS_FOX // 2)
        x = _oproj_call(x, oa, ob, oc, p, ln_mix_post[l][None, :], OPROJ_ROWS)
        x = _ffn_call(x, p, ln_ffn_pre[l][None, :], ln_ffn_post[l][None, :], FFN_ROWS, FFN_COLS)
    return x
```

```python
import functools
import math

import jax
import jax.numpy as jnp
from jax import lax
from jax.experimental import pallas as pl
from jax.experimental.pallas import tpu as pltpu

F32 = jnp.float32
BF16 = jnp.bfloat16

HEAD_DIM = 64
N_HEADS_MOBA = 6
N_HEADS_MLA = 4
N_HEADS_FOX = 6
MOBA_BLOCK = 256
MOBA_TOPK = 3
MLA_Q_RANK = 256
MLA_KV_RANK = 128
MLA_NOPE_DIM = 64
MLA_ROPE_DIM = 32
MLA_V_DIM = 64
MLA_QK_DIM = MLA_NOPE_DIM + MLA_ROPE_DIM
ROPE_THETA = 10000.0
T5_BUCKETS = 32
T5_MAX_DIST = 1024
D_FF = 2816
CONV_WIDTH = 3
NORM_EPS = 1e-6
NEG_INF = -1e30
W_MOBA = N_HEADS_MOBA * HEAD_DIM
W_MLA = N_HEADS_MLA * MLA_V_DIM
W_FOX = N_HEADS_FOX * HEAD_DIM
PROJ_SIZES = (W_MOBA, W_MOBA, W_MOBA, MLA_Q_RANK, MLA_KV_RANK, MLA_ROPE_DIM, W_FOX, W_FOX, W_FOX, N_HEADS_FOX)

LANES = 128
SUBLANES = 8
VMEM_LIMIT = 56 * 1024 * 1024

ATT_TILE = 512
KV_TILE = 512
PROJ_ROWS = KV_TILE
OPROJ_ROWS = 512
FFN_ROWS = 512
FFN_HALO = 16
FEAT0 = HEAD_DIM
V_ROWS = HEAD_DIM + 16
LOG2E = math.log2(math.e)
NBLK_PAD = 16
N_NEAR = (T5_MAX_DIST + MOBA_BLOCK - 1 + MOBA_BLOCK - 1) // MOBA_BLOCK

_TOK_AK = (0, 384)
_TOK_FK = (384, 768)
_TOK_CQ = (768, 1024)
_TOK_CKV = (1024, 1152)
_TOK_KR = (1152, 1280)
_TOK_KRR = (1280, 1408)
_TOK_FG = (1408, 1536)
_TOK_COLS = 1536
_T_AQ = (0, 384)
_T_FQ = (384, 768)
_T_AV = (768, 1152)
_T_FV = (1152, 1536)


def _rms(xf, g):
    return xf * lax.rsqrt(jnp.mean(xf * xf, axis=-1, keepdims=True) + NORM_EPS) * g


def _dot(a, b):
    return jnp.dot(a, b, preferred_element_type=F32)


def _dot_nt(a, b):
    return lax.dot_general(a, b, (((1,), (1,)), ((), ())), preferred_element_type=F32)


def _bf16_pieces(x):
    hi = x.astype(BF16).astype(F32)
    r = x - hi
    mid = r.astype(BF16).astype(F32)
    lo = (r - mid).astype(BF16).astype(F32)
    return hi, mid, lo


def _dot_3pass(a, b):
    a_hi = a.astype(BF16)
    a_lo = (a - a_hi.astype(F32)).astype(BF16)
    b_hi = b.astype(BF16)
    b_lo = (b - b_hi.astype(F32)).astype(BF16)
    return _dot(a_hi, b_hi) + (_dot(a_hi, b_lo) + _dot(a_lo, b_hi))


def _head_slot(arr, hd):
    pair = arr[:, (hd // 2) * LANES:(hd // 2 + 1) * LANES]
    if hd % 2:
        pair = pltpu.roll(pair, HEAD_DIM, axis=1)
    return pair


def _proj_kernel(x_ref, g_ref, wtok_ref, wt_ref, bf_ref, qn_ref, kvn_ref, wuq_ref, wuqr_ref,
                 wukvk_ref, wukvv_ref, ctok_ref, stok_ref, ct_ref, st_ref, far_ref,
                 qat_ref, ka_ref, vat_ref, qft_ref, kf_ref, vft_ref, qmt_ref, km_ref, vmt_ref,
                 kmean_ref, fcarry_ref, *, tm):
    t = pl.program_id(1)
    n_sub = tm // MOBA_BLOCK

    @pl.when(t == 0)
    def _():
        kmean_ref[...] = jnp.zeros_like(kmean_ref)
        fcarry_ref[...] = jnp.zeros_like(fcarry_ref)

    h = _rms(x_ref[0], g_ref[...]).astype(BF16)
    lane = lax.broadcasted_iota(jnp.int32, (tm, LANES), 1)
    row = lax.broadcasted_iota(jnp.int32, (tm, LANES), 0)
    blk0 = t * n_sub
    is_head = lane < HEAD_DIM

    def w_tok(rng):
        return wtok_ref[:, rng[0]:rng[1]]

    def w_t(rng):
        return wt_ref[rng[0]:rng[1], :]

    sub8v = lax.broadcasted_iota(jnp.int32, (SUBLANES, tm), 0)
    ones_row = jnp.where(sub8v == 0, 1.0, 0.0)
    v_pad = jnp.zeros((V_ROWS - HEAD_DIM - SUBLANES, tm), F32)

    def store_vt(ref, vt):
        n_heads = vt.shape[0] // HEAD_DIM
        ref[0, 0] = jnp.concatenate(
            [piece for hd in range(n_heads)
             for piece in (vt[hd * HEAD_DIM:(hd + 1) * HEAD_DIM, :], ones_row, v_pad)], axis=0).astype(BF16)

    ak = _dot(h, w_tok(_TOK_AK))
    blk_row = blk0 + lax.shift_right_logical(row, int(math.log2(MOBA_BLOCK)))
    k_feat = jnp.where(lane == FEAT0 + blk_row, 1.0, 0.0)
    for hd in range(N_HEADS_MOBA):
        ka_ref[0, :, hd * LANES:(hd + 1) * LANES] = jnp.where(is_head, _head_slot(ak, hd), k_feat).astype(BF16)

    lane_k = lax.broadcasted_iota(jnp.int32, (1, W_MOBA), 1)
    for bi in range(n_sub):
        mean_row = jnp.mean(ak[bi * MOBA_BLOCK:(bi + 1) * MOBA_BLOCK, :], axis=0, keepdims=True)
        for hd in range(N_HEADS_MOBA):
            in_head = (lane_k >= hd * HEAD_DIM) & (lane_k < (hd + 1) * HEAD_DIM)
            kmean_ref[pl.ds(hd * NBLK_PAD + blk0 + bi, 1), :] = jnp.where(in_head, mean_row, 0.0)

    qat = _dot_nt(w_t(_T_AQ), h) * (HEAD_DIM ** -0.5 * LOG2E)
    gate = _dot_3pass(kmean_ref[...], qat)
    own = blk0 + lax.shift_right_logical(
        lax.broadcasted_iota(jnp.int32, (NBLK_PAD, tm), 1), int(math.log2(MOBA_BLOCK)))
    n_io = lax.broadcasted_iota(jnp.int32, (NBLK_PAD, tm), 0)
    past = n_io < own
    zeros_tail = jnp.zeros((LANES - HEAD_DIM - NBLK_PAD, tm), F32)
    for hd in range(N_HEADS_MOBA):
        g = jnp.where(past, gate[hd * NBLK_PAD:(hd + 1) * NBLK_PAD, :], NEG_INF)
        rank = jnp.zeros((NBLK_PAD, tm), jnp.int32)
        for n2 in range(NBLK_PAD):
            r = g[n2:n2 + 1, :]
            tie = jnp.where(n2 < n_io, 1, 0)
            rank = rank + jnp.where(r > g, 1, jnp.where(r == g, tie, 0))
        sel = jnp.where(past, rank, MOBA_TOPK) < MOBA_TOPK
        far = jnp.where(own - n_io >= N_NEAR, far_ref[hd] * LOG2E, 0.0)
        q_feat = jnp.where(sel | (n_io == own), far, NEG_INF)
        qat_ref[0, 0, hd * LANES:(hd + 1) * LANES, :] = jnp.concatenate(
            [qat[hd * HEAD_DIM:(hd + 1) * HEAD_DIM, :], q_feat, zeros_tail], axis=0).astype(BF16)
    store_vt(vat_ref, _dot_nt(w_t(_T_AV), h))

    fg = _dot(h, w_tok(_TOK_FG)) + bf_ref[...]
    logf = jnp.minimum(fg, 0.0) - jnp.log1p(jnp.exp(-jnp.abs(fg)))
    csum = jnp.where(lane < N_HEADS_FOX, logf, 0.0)
    sft = 1
    while sft < tm:
        csum = csum + jnp.where(row >= sft, pltpu.roll(csum, sft, axis=0), 0.0)
        sft *= 2
    decay = csum + fcarry_ref[0:1, :]
    fcarry_ref[0:1, :] = decay[tm - 1:tm, :]
    decay = decay * LOG2E
    decay_t = decay.T

    fk = _dot(h, w_tok(_TOK_FK))
    fqt = _dot_nt(w_t(_T_FQ), h) * (HEAD_DIM ** -0.5 * LOG2E)
    sub8 = lax.broadcasted_iota(jnp.int32, (SUBLANES, tm), 0)
    zeros_tail_f = jnp.zeros((LANES - HEAD_DIM - SUBLANES, tm), F32)
    for hd in range(N_HEADS_FOX):
        fcol = jnp.broadcast_to(decay[:, hd:hd + 1], (tm, LANES))
        hi, mid, lo = _bf16_pieces(fcol)
        k_feat_f = jnp.where(lane < FEAT0 + 3, 1.0,
                             jnp.where(lane == FEAT0 + 3, -hi,
                                       jnp.where(lane == FEAT0 + 4, -mid,
                                                 jnp.where(lane == FEAT0 + 5, -lo, 0.0))))
        kf_ref[0, :, hd * LANES:(hd + 1) * LANES] = jnp.where(is_head, _head_slot(fk, hd), k_feat_f).astype(BF16)
        hi, mid, lo = _bf16_pieces(decay_t[hd:hd + 1, :])
        q_feat_f = jnp.where(sub8 == 0, hi,
                             jnp.where(sub8 == 1, mid,
                                       jnp.where(sub8 == 2, lo,
                                                 jnp.where(sub8 < 6, 1.0, 0.0))))
        qft_ref[0, 0, hd * LANES:(hd + 1) * LANES, :] = jnp.concatenate(
            [fqt[hd * HEAD_DIM:(hd + 1) * HEAD_DIM, :], q_feat_f, zeros_tail_f], axis=0).astype(BF16)
    store_vt(vft_ref, _dot_nt(w_t(_T_FV), h))

    cqn = _rms(_dot(h, w_tok(_TOK_CQ)), qn_ref[...]).astype(BF16)
    qm = _dot_nt(wuq_ref[...], cqn)
    qmr = _dot_nt(wuqr_ref[...], cqn)
    kvn = _rms(_dot(h, w_tok(_TOK_CKV)), kvn_ref[...]).astype(BF16)
    k_nope = _dot(kvn, wukvk_ref[...])
    k_rope = _dot(h, w_tok(_TOK_KR)) * ctok_ref[...] + _dot(h, w_tok(_TOK_KRR)) * stok_ref[...]
    cos_t = ct_ref[...]
    sin_t = st_ref[...]
    for hd in range(N_HEADS_MLA):
        sl = slice(hd * LANES, (hd + 1) * LANES)
        qmt_ref[0, 0, sl, :] = ((qm[sl, :] * cos_t + qmr[sl, :] * sin_t) * (MLA_QK_DIM ** -0.5 * LOG2E)).astype(BF16)
        km_ref[0, :, sl] = (k_nope[:, sl] + k_rope).astype(BF16)
    store_vt(vmt_ref, _dot_nt(wukvv_ref[...], kvn))


def _proj_call(x, g, p, tables, far, tm):
    b, s, d = x.shape
    nt = s // tm
    n_sub = tm // MOBA_BLOCK
    nq = s // ATT_TILE

    def full(a):
        return pl.BlockSpec(a.shape, lambda bi, ti: (0,) * a.ndim)

    ctok, stok, ct, st = tables
    in_specs = [
        pl.BlockSpec((1, tm, d), lambda bi, ti: (bi, ti, 0)),
        full(g), full(p["w_tok"]), full(p["w_t"]), full(p["b_f"]), full(p["q_norm"]), full(p["kv_norm"]),
        full(p["wuq_t"]), full(p["wuq_rot_t"]), full(p["wukv_k"]), full(p["wukv_v_t"]),
        pl.BlockSpec((tm, LANES), lambda bi, ti: (ti, 0)),
        pl.BlockSpec((tm, LANES), lambda bi, ti: (ti, 0)),
        pl.BlockSpec((LANES, tm), lambda bi, ti: (0, ti)),
        pl.BlockSpec((LANES, tm), lambda bi, ti: (0, ti)),
        pl.BlockSpec(memory_space=pltpu.SMEM),
    ]

    def qt_spec(n_heads):
        return pl.BlockSpec((1, 1, n_heads * LANES, tm), lambda bi, ti: (bi, ti, 0, 0))

    def k_spec(n_heads):
        return pl.BlockSpec((1, tm, n_heads * LANES), lambda bi, ti: (bi, ti, 0))

    def vt_spec(width):
        return pl.BlockSpec((1, 1, width, tm), lambda bi, ti: (bi, ti, 0, 0))

    def qt_shape(n_heads):
        return jax.ShapeDtypeStruct((b, nt, n_heads * LANES, tm), BF16)

    def k_shape(n_heads):
        return jax.ShapeDtypeStruct((b, s, n_heads * LANES), BF16)

    def vt_shape(width):
        return jax.ShapeDtypeStruct((b, nt, width, tm), BF16)

    out_specs = [qt_spec(6), k_spec(6), vt_spec(6 * V_ROWS), qt_spec(6), k_spec(6), vt_spec(6 * V_ROWS),
                 qt_spec(4), k_spec(4), vt_spec(4 * V_ROWS)]
    out_shape = [qt_shape(6), k_shape(6), vt_shape(6 * V_ROWS), qt_shape(6), k_shape(6), vt_shape(6 * V_ROWS),
                 qt_shape(4), k_shape(4), vt_shape(4 * V_ROWS)]
    return pl.pallas_call(
        functools.partial(_proj_kernel, tm=tm),
        grid=(b, nt),
        in_specs=in_specs,
        out_specs=out_specs,
        out_shape=out_shape,
        scratch_shapes=[pltpu.VMEM((N_HEADS_MOBA * NBLK_PAD + 32, W_MOBA), F32),
                        pltpu.VMEM((SUBLANES, LANES), F32)],
        compiler_params=pltpu.CompilerParams(dimension_semantics=("arbitrary", "arbitrary"),
                                             vmem_limit_bytes=VMEM_LIMIT),
        name="proj",
    )(x, g, p["w_tok"], p["w_t"], p["b_f"], p["q_norm"], p["kv_norm"], p["wuq_t"], p["wuq_rot_t"],
      p["wukv_k"], p["wukv_v_t"], ctok, stok, ct, st, far)


def _attn_kernel(*refs, has_bias, tq, tk, nq, n_add, n_plain):
    if has_bias:
        (q_ref, j_ref, qt_ref, k_ref, vt_ref, bidx_ref, tab_ref, o_ref,
         acc_ref, m_ref, s0_ref, s1_ref, s2_ref, s3_ref, bias_ref) = refs
    else:
        (q_ref, j_ref, qt_ref, k_ref, vt_ref, o_ref,
         acc_ref, m_ref, s0_ref, s1_ref, s2_ref, s3_ref, bias_ref) = refs
    hp = pl.program_id(0)
    bi = pl.program_id(1)
    blk = MOBA_BLOCK
    n_near = N_NEAR if has_bias else 1

    @pl.when(bi == 0)
    def _():
        tri = (lax.broadcasted_iota(jnp.int32, (blk, blk), 0) <= lax.broadcasted_iota(jnp.int32, (blk, blk), 1))
        for hh in range(2):
            bias_ref[hh, 0] = jnp.full((blk, blk), NEG_INF, F32)
            bias_ref[hh, n_near + 1] = jnp.zeros((blk, blk), F32)
            if has_bias:
                def build(d, carry, hh=hh):
                    idx = bidx_ref[d]
                    bias_ref[hh, d + 1] = lax.fori_loop(
                        0, T5_BUCKETS,
                        lambda bk, tl: jnp.where(idx == bk, tab_ref[bk, hp * 2 + hh] * LOG2E, tl),
                        jnp.zeros((blk, blk), F32))
                    return carry
                lax.fori_loop(0, n_near, build, 0)
                bias_ref[hh, 1] = jnp.where(tri, bias_ref[hh, 1], NEG_INF)
            else:
                bias_ref[hh, 1] = jnp.where(tri, 0.0, NEG_INF)

    acc_ref[...] = jnp.zeros_like(acc_ref)
    m_ref[...] = jnp.full(m_ref.shape, NEG_INF, F32)
    bufs = (s0_ref, s1_ref, s2_ref, s3_ref)

    n_items = n_add + n_plain

    def scores(e, s_ref, additive):
        qi = q_ref[e]
        j = j_ref[e]
        start = pl.multiple_of(j * tk, tk)
        tmax = []
        for hh in range(2):
            k = k_ref[0, pl.ds(start, tk), hh * LANES:(hh + 1) * LANES]
            s = _dot(k, qt_ref[0, qi, hh * LANES:(hh + 1) * LANES, :])
            if additive:
                s = s + jnp.concatenate([
                    jnp.concatenate([
                        bias_ref[hh, jnp.clip((qi * (tq // blk) + c) - (j * (tk // blk) + r), -1, n_near) + 1]
                        for c in range(tq // blk)], axis=1)
                    for r in range(tk // blk)], axis=0)
            s_ref[hh] = s
            tmax.append(jnp.max(s, axis=0, keepdims=True))
        return tuple(tmax)

    def update(e, s_ref, tmax):
        qi = q_ref[e]
        j = j_ref[e]
        for hh in range(2):
            m_old = m_ref[qi, hh, 0:1, :]
            m_new = jnp.maximum(m_old, tmax[hh])
            alpha = jnp.exp2(m_old - m_new)
            p = jnp.exp2(s_ref[hh] - m_new).astype(BF16)
            v = vt_ref[0, j, hh * V_ROWS:(hh + 1) * V_ROWS, :]
            acc_ref[qi, hh] = alpha * acc_ref[qi, hh] + _dot(v, p)
            m_ref[qi, hh, 0:1, :] = m_new

    def step(e, r, carry, additive):
        tmax_cur, tmax_nxt = carry
        tmax_new = scores(e + 2, bufs[(r + 2) % 4], additive)
        update(e, bufs[r], tmax_cur)
        return tmax_nxt, tmax_new

    def steps(lo, hi, carry, additive):
        n_groups = max(hi - lo, 0) // 4

        def group(g, carry):
            for r in range(4):
                carry = step(lo + 4 * g + r, (lo + r) % 4, carry, additive)
            return carry

        carry = lax.fori_loop(0, n_groups, group, carry)
        for e in range(lo + 4 * n_groups, hi):
            carry = step(e, e % 4, carry, additive)
        return carry

    carry = (scores(0, bufs[0], 0 < n_add), scores(1, bufs[1], 1 < n_add))
    split = max(n_add - 2, 0)
    carry = steps(0, split, carry, True)
    carry = steps(split, n_items - 2, carry, False)
    update(n_items - 2, bufs[(n_items - 2) % 4], carry[0])
    update(n_items - 1, bufs[(n_items - 1) % 4], carry[1])

    def finish(qi, carry):
        o = [acc_ref[qi, hh, 0:HEAD_DIM, :] / acc_ref[qi, hh, HEAD_DIM:HEAD_DIM + 1, :] for hh in range(2)]
        o_ref[0, pl.ds(pl.multiple_of(qi * tq, tq), tq), :] = jnp.concatenate(o, axis=0).T.astype(BF16)
        return carry

    lax.fori_loop(0, nq, finish, 0)


def _attn_schedule(nq, tq, tk, n_near):
    blk = MOBA_BLOCK
    add, plain = [], []
    for q in range(nq):
        for j in range(q, -1, -1):
            min_dist = q * (tq // blk) - (j * (tk // blk) + tk // blk - 1)
            (add if min_dist < n_near else plain).append((q, j))
    items = add + plain
    return (jnp.asarray([q for q, _ in items], jnp.int32), jnp.asarray([j for _, j in items], jnp.int32),
            len(add), len(plain))


def _attn_call(qt, k, vt, n_pairs, bias_inputs=None):
    b, nq, _, tq = qt.shape
    _, s, _ = k.shape
    tk = vt.shape[-1]
    assert tq == tk, "the causal-triangle tile assumes equal query and key tiles"
    has_bias = bias_inputs is not None
    q_tab, j_tab, n_add, n_plain = _attn_schedule(nq, tq, tk, N_NEAR if has_bias else 1)
    assert n_add + n_plain >= 2
    smem = pl.BlockSpec(memory_space=pltpu.SMEM)
    in_specs = [
        smem, smem,
        pl.BlockSpec((1, nq, 2 * LANES, tq), lambda p, bi: (bi, 0, p, 0)),
        pl.BlockSpec((1, s, 2 * LANES), lambda p, bi: (bi, 0, p)),
        pl.BlockSpec((1, s // tk, 2 * V_ROWS, tk), lambda p, bi: (bi, 0, p, 0)),
    ]
    args = [q_tab, j_tab, qt, k, vt]
    n_tiles = (N_NEAR if has_bias else 1) + 2
    scratch = [pltpu.VMEM((nq, 2, V_ROWS, tq), F32), pltpu.VMEM((nq, 2, SUBLANES, tq), F32),
               pltpu.VMEM((2, tk, tq), F32), pltpu.VMEM((2, tk, tq), F32),
               pltpu.VMEM((2, tk, tq), F32), pltpu.VMEM((2, tk, tq), F32),
               pltpu.VMEM((2, n_tiles, MOBA_BLOCK, MOBA_BLOCK), F32)]
    if has_bias:
        bidx, tab = bias_inputs
        in_specs += [pl.BlockSpec(bidx.shape, lambda p, bi: (0, 0, 0)), smem]
        args += [bidx, tab]
    return pl.pallas_call(
        functools.partial(_attn_kernel, has_bias=has_bias, tq=tq, tk=tk, nq=nq, n_add=n_add, n_plain=n_plain),
        grid=(n_pairs, b),
        in_specs=in_specs,
        out_specs=pl.BlockSpec((1, s, 2 * HEAD_DIM), lambda p, bi: (bi, 0, p)),
        out_shape=jax.ShapeDtypeStruct((b, s, n_pairs * 2 * HEAD_DIM), BF16),
        scratch_shapes=scratch,
        compiler_params=pltpu.CompilerParams(dimension_semantics=("arbitrary",) * 2,
                                             vmem_limit_bytes=VMEM_LIMIT),
        name="attn_bias" if has_bias else "attn",
    )(*args)


def _oproj_kernel(x_ref, oa_ref, ob_ref, oc_ref, wa_ref, wb_ref, wc_ref, g_ref, o_ref):
    y = _dot(oa_ref[0], wa_ref[...]) + _dot(ob_ref[0], wb_ref[...]) + _dot(oc_ref[0], wc_ref[...])
    o_ref[0] = x_ref[0] + _rms(y, g_ref[...])


def _oproj_call(x, oa, ob, oc, p, g, tm):
    b, s, d = x.shape

    def rows(width):
        return pl.BlockSpec((1, tm, width), lambda bi, ti: (bi, ti, 0))

    def full(a):
        return pl.BlockSpec(a.shape, lambda bi, ti: (0,) * a.ndim)

    return pl.pallas_call(
        _oproj_kernel,
        grid=(b, s // tm),
        in_specs=[rows(d), rows(W_MOBA), rows(W_MLA), rows(W_FOX),
                  full(p["wo_a"]), full(p["wo_b"]), full(p["wo_c"]), full(g)],
        out_specs=rows(d),
        out_shape=jax.ShapeDtypeStruct(x.shape, F32),
        compiler_params=pltpu.CompilerParams(dimension_semantics=("arbitrary", "arbitrary"),
                                             vmem_limit_bytes=VMEM_LIMIT),
        name="oproj",
    )(x, oa, ob, oc, p["wo_a"], p["wo_b"], p["wo_c"], g)


def _ffn_kernel(x_ref, xp_ref, gpre_ref, wup_ref, cw_ref, cb_ref, wd_ref, gpost_ref, o_ref, h_ref, u_ref, *, tm):
    ti = pl.program_id(1)
    halo = FFN_HALO

    h_ref[halo:, :] = _rms(x_ref[0], gpre_ref[...]).astype(BF16)
    prev = _rms(xp_ref[0], gpre_ref[...])
    h_ref[0:halo, :] = jnp.where(ti == 0, 0.0, prev).astype(BF16)

    h = h_ref[...]
    for half in range(2):
        u_ref[half] = _dot(h, wup_ref[half])

    def conv(half):
        cw = cw_ref[half]
        return (cw[0:1, :] * u_ref[half, halo - 2:halo - 2 + tm, :]
                + cw[1:2, :] * u_ref[half, halo - 1:halo - 1 + tm, :]
                + cw[2:3, :] * u_ref[half, halo:halo + tm, :]) + cb_ref[half]

    act = (jax.nn.gelu(conv(0), approximate=True) * conv(1)).astype(BF16)
    o_ref[0] = x_ref[0] + _rms(_dot(act, wd_ref[...]), gpost_ref[...])


def _ffn_call(x, p, gpre, gpost, tm):
    b, s, d = x.shape
    halo = FFN_HALO
    blocks_per_tile = tm // halo

    def full(a):
        return pl.BlockSpec(a.shape, lambda bi, ti: (0,) * a.ndim)

    in_specs = [
        pl.BlockSpec((1, tm, d), lambda bi, ti: (bi, ti, 0)),
        pl.BlockSpec((1, halo, d), lambda bi, ti: (bi, jnp.maximum(ti * blocks_per_tile - 1, 0), 0)),
        full(gpre), full(p["w_up"]), full(p["conv_w"]), full(p["conv_b"]), full(p["w_down"]), full(gpost),
    ]
    return pl.pallas_call(
        functools.partial(_ffn_kernel, tm=tm),
        grid=(b, s // tm),
        in_specs=in_specs,
        out_specs=pl.BlockSpec((1, tm, d), lambda bi, ti: (bi, ti, 0)),
        out_shape=jax.ShapeDtypeStruct(x.shape, F32),
        scratch_shapes=[pltpu.VMEM((halo + tm, d), BF16), pltpu.VMEM((2, halo + tm, D_FF), F32)],
        compiler_params=pltpu.CompilerParams(dimension_semantics=("arbitrary",) * 2,
                                             vmem_limit_bytes=VMEM_LIMIT),
        name="ffn",
    )(x, x, gpre, p["w_up"], p["conv_w"], p["conv_b"], p["w_down"], gpost)


def _pad_cols(w, left, total):
    return jnp.pad(w, ((0, 0), (left, total - left - w.shape[1])))


def _prep_layer(w_in, b_f, q_norm, kv_norm, w_uq, w_ukv, w_o, w_up, conv_w, conv_b, w_down):
    offs = [0]
    for sz in PROJ_SIZES:
        offs.append(offs[-1] + sz)
    a_q, a_k, a_v, c_q, c_kv, k_r, f_q, f_k, f_v, f_g = [w_in[:, offs[i]:offs[i + 1]] for i in range(10)]
    half = MLA_ROPE_DIM // 2

    def rot(w):
        return jnp.concatenate([-w[:, half:], w[:, :half]], axis=1)

    w_tok = jnp.concatenate([
        a_k, f_k, c_q, c_kv,
        _pad_cols(k_r, FEAT0, LANES), _pad_cols(rot(k_r), FEAT0, LANES), _pad_cols(f_g, 0, LANES)], axis=1)
    w_t = jnp.concatenate([a_q, f_q, a_v, f_v], axis=1).T
    uq = w_uq.reshape(MLA_Q_RANK, N_HEADS_MLA, MLA_QK_DIM)
    uq_main = jnp.pad(uq, ((0, 0), (0, 0), (0, LANES - MLA_QK_DIM)))
    uq_rope = uq[:, :, MLA_NOPE_DIM:]
    uq_rot = jnp.concatenate([-uq_rope[:, :, half:], uq_rope[:, :, :half]], axis=2)
    uq_rot = jnp.pad(uq_rot, ((0, 0), (0, 0), (MLA_NOPE_DIM, LANES - MLA_QK_DIM)))
    ukv = w_ukv.reshape(MLA_KV_RANK, N_HEADS_MLA, MLA_NOPE_DIM + MLA_V_DIM)
    ukv_k = jnp.pad(ukv[:, :, :MLA_NOPE_DIM], ((0, 0), (0, 0), (0, LANES - MLA_NOPE_DIM)))
    ukv_v = ukv[:, :, MLA_NOPE_DIM:]
    return {
        "w_tok": w_tok.astype(BF16),
        "w_t": w_t.astype(BF16),
        "b_f": _pad_cols(b_f[None, :], 0, LANES),
        "q_norm": q_norm[None, :],
        "kv_norm": kv_norm[None, :],
        "wuq_t": uq_main.reshape(MLA_Q_RANK, N_HEADS_MLA * LANES).T.astype(BF16),
        "wuq_rot_t": uq_rot.reshape(MLA_Q_RANK, N_HEADS_MLA * LANES).T.astype(BF16),
        "wukv_k": ukv_k.reshape(MLA_KV_RANK, N_HEADS_MLA * LANES).astype(BF16),
        "wukv_v_t": ukv_v.reshape(MLA_KV_RANK, W_MLA).T.astype(BF16),
        "wo_a": w_o[:W_MOBA].astype(BF16),
        "wo_b": w_o[W_MOBA:W_MOBA + W_MLA].astype(BF16),
        "wo_c": w_o[W_MOBA + W_MLA:].astype(BF16),
        "w_up": w_up.reshape(-1, 2, D_FF).transpose(1, 0, 2).astype(BF16),
        "conv_w": conv_w.reshape(CONV_WIDTH, 2, D_FF).transpose(1, 0, 2),
        "conv_b": conv_b.reshape(2, 1, D_FF),
        "w_down": w_down.astype(BF16),
    }


def _rope_tables(s):
    half = MLA_ROPE_DIM // 2
    inv = ROPE_THETA ** (-jnp.arange(half, dtype=F32) / half)
    ang = jnp.arange(s).astype(F32)[:, None] * inv[None, :]
    cos = jnp.concatenate([jnp.cos(ang)] * 2, axis=1)
    sin = jnp.concatenate([jnp.sin(ang)] * 2, axis=1)
    ctok = _pad_cols(cos, FEAT0, LANES)
    stok = _pad_cols(sin, FEAT0, LANES)
    ones = jnp.ones((s, MLA_NOPE_DIM), F32)
    ct = jnp.concatenate([ones, cos, jnp.zeros((s, LANES - MLA_QK_DIM), F32)], axis=1).T
    st = stok.T
    return ctok, stok, ct, st


def _t5_bucket_tiles(tile):
    d = jnp.arange(N_NEAR)[:, None, None]
    j = jnp.arange(tile)[None, :, None]
    i = jnp.arange(tile)[None, None, :]
    n = jnp.maximum(d * tile + i - j, 0)
    exact = T5_BUCKETS // 2
    large = exact + (jnp.log(jnp.maximum(n, 1).astype(F32) / exact)
                     / math.log(T5_MAX_DIST / exact) * (T5_BUCKETS - exact)).astype(jnp.int32)
    return jnp.where(n < exact, n, jnp.minimum(large, T5_BUCKETS - 1)).astype(jnp.int32)


def kernel(x, rel_bias, ln_mix_pre, ln_mix_post, ln_ffn_pre, ln_ffn_post, w_in, b_f, q_norm, kv_norm, w_uq,
           w_ukv, w_o, w_up, conv_w, conv_b, w_down):
    b, s, d = x.shape
    depth = w_in.shape[0]
    assert s % PROJ_ROWS == 0 and s % FFN_ROWS == 0 and s % OPROJ_ROWS == 0
    assert s // MOBA_BLOCK <= NBLK_PAD
    tables = _rope_tables(s)
    bidx = _t5_bucket_tiles(MOBA_BLOCK)
    far = rel_bias[T5_BUCKETS - 1, :]
    for l in range(depth):
        p = _prep_layer(w_in[l], b_f[l], q_norm[l], kv_norm[l], w_uq[l], w_ukv[l], w_o[l], w_up[l],
                        conv_w[l], conv_b[l], w_down[l])
        qat, ka, vat, qft, kf, vft, qmt, km, vmt = _proj_call(x, ln_mix_pre[l][None, :], p, tables, far, PROJ_ROWS)
        oa = _attn_call(qat, ka, vat, N_HEADS_MOBA // 2, (bidx, rel_bias))
        ob = _attn_call(qmt, km, vmt, N_HEADS_MLA // 2)
        oc = _attn_call(qft, kf, vft, N_HEADS_FOX // 2)
        x = _oproj_call(x, oa, ob, oc, p, ln_mix_post[l][None, :], OPROJ_ROWS)
        x = _ffn_call(x, p, ln_ffn_pre[l][None, :], ln_ffn_post[l][None, :], FFN_ROWS)
    return x
```

```python
import functools
import math

import jax
import jax.numpy as jnp
from jax import lax
from jax.experimental import pallas as pl
from jax.experimental.pallas import tpu as pltpu

F32 = jnp.float32
BF16 = jnp.bfloat16

HEAD_DIM = 64
N_HEADS_MOBA = 6
N_HEADS_MLA = 4
N_HEADS_FOX = 6
MOBA_BLOCK = 256
MOBA_TOPK = 3
MLA_Q_RANK = 256
MLA_KV_RANK = 128
MLA_NOPE_DIM = 64
MLA_ROPE_DIM = 32
MLA_V_DIM = 64
MLA_QK_DIM = MLA_NOPE_DIM + MLA_ROPE_DIM
ROPE_THETA = 10000.0
T5_BUCKETS = 32
T5_MAX_DIST = 1024
D_FF = 2816
CONV_WIDTH = 3
NORM_EPS = 1e-6
NEG_INF = -1e30
W_MOBA = N_HEADS_MOBA * HEAD_DIM
W_MLA = N_HEADS_MLA * MLA_V_DIM
W_FOX = N_HEADS_FOX * HEAD_DIM
PROJ_SIZES = (W_MOBA, W_MOBA, W_MOBA, MLA_Q_RANK, MLA_KV_RANK, MLA_ROPE_DIM, W_FOX, W_FOX, W_FOX, N_HEADS_FOX)

LANES = 128
SUBLANES = 8
VMEM_LIMIT = 56 * 1024 * 1024

ATT_TILE = 512
KV_TILE = 512
PROJ_ROWS = KV_TILE
OPROJ_ROWS = 1024
FFN_ROWS = 512
FFN_HALO = 16
FEAT0 = HEAD_DIM
V_ROWS = HEAD_DIM + 16
LOG2E = math.log2(math.e)
NBLK_PAD = 16
N_NEAR = (T5_MAX_DIST + MOBA_BLOCK - 1 + MOBA_BLOCK - 1) // MOBA_BLOCK

_TOK_AK = (0, 384)
_TOK_FK = (384, 768)
_TOK_CQ = (768, 1024)
_TOK_CKV = (1024, 1152)
_TOK_KR = (1152, 1280)
_TOK_KRR = (1280, 1408)
_TOK_FG = (1408, 1536)
_TOK_COLS = 1536
_T_AQ = (0, 384)
_T_FQ = (384, 768)
_T_AV = (768, 1152)
_T_FV = (1152, 1536)


def _rms(xf, g):
    return xf * lax.rsqrt(jnp.mean(xf * xf, axis=-1, keepdims=True) + NORM_EPS) * g


def _dot(a, b):
    return jnp.dot(a, b, preferred_element_type=F32)


def _dot_nt(a, b):
    return lax.dot_general(a, b, (((1,), (1,)), ((), ())), preferred_element_type=F32)


def _bf16_pieces(x):
    hi = x.astype(BF16).astype(F32)
    r = x - hi
    mid = r.astype(BF16).astype(F32)
    lo = (r - mid).astype(BF16).astype(F32)
    return hi, mid, lo


def _dot_3pass(a, b):
    a_hi = a.astype(BF16)
    a_lo = (a - a_hi.astype(F32)).astype(BF16)
    b_hi = b.astype(BF16)
    b_lo = (b - b_hi.astype(F32)).astype(BF16)
    return _dot(jnp.concatenate([a_hi, a_hi, a_lo], axis=1), jnp.concatenate([b_hi, b_lo, b_hi], axis=0))


def _head_slot(arr, hd):
    pair = arr[:, (hd // 2) * LANES:(hd // 2 + 1) * LANES]
    if hd % 2:
        pair = pltpu.roll(pair, HEAD_DIM, axis=1)
    return pair


def _proj_kernel(x_ref, g_ref, wtok_ref, wt_ref, bf_ref, qn_ref, kvn_ref, wuq_ref,
                 wukvk_ref, wukvv_ref, ctok_ref, stok_ref, ct_ref, st_ref, far_ref,
                 qat_ref, ka_ref, vat_ref, qft_ref, kf_ref, vft_ref, qmt_ref, km_ref, vmt_ref,
                 kmean_ref, fcarry_ref, *, tm):
    t = pl.program_id(1)
    n_sub = tm // MOBA_BLOCK

    @pl.when(t == 0)
    def _():
        kmean_ref[...] = jnp.zeros_like(kmean_ref)
        fcarry_ref[...] = jnp.zeros_like(fcarry_ref)

    h = _rms(x_ref[0], g_ref[...]).astype(BF16)
    lane = lax.broadcasted_iota(jnp.int32, (tm, LANES), 1)
    row = lax.broadcasted_iota(jnp.int32, (tm, LANES), 0)
    blk0 = t * n_sub
    is_head = lane < HEAD_DIM

    tok_all = _dot(h, wtok_ref[...])
    t_all = _dot_nt(wt_ref[...], h)

    def tok(rng):
        return tok_all[:, rng[0]:rng[1]]

    def tr(rng):
        return t_all[rng[0]:rng[1], :]

    sub8v = lax.broadcasted_iota(jnp.int32, (SUBLANES, tm), 0)
    ones_row = jnp.where(sub8v == 0, 1.0, 0.0)
    v_pad = jnp.zeros((V_ROWS - HEAD_DIM - SUBLANES, tm), F32)

    def store_vt(ref, vt):
        n_heads = vt.shape[0] // HEAD_DIM
        ref[0, 0] = jnp.concatenate(
            [piece for hd in range(n_heads)
             for piece in (vt[hd * HEAD_DIM:(hd + 1) * HEAD_DIM, :], ones_row, v_pad)], axis=0).astype(BF16)

    ak = tok(_TOK_AK)
    blk_row = blk0 + lax.shift_right_logical(row, int(math.log2(MOBA_BLOCK)))
    k_feat = jnp.where(lane == FEAT0 + blk_row, 1.0, 0.0)
    for hd in range(N_HEADS_MOBA):
        ka_ref[0, :, hd * LANES:(hd + 1) * LANES] = jnp.where(is_head, _head_slot(ak, hd), k_feat).astype(BF16)

    lane_k = lax.broadcasted_iota(jnp.int32, (1, W_MOBA), 1)
    for bi in range(n_sub):
        mean_row = jnp.mean(ak[bi * MOBA_BLOCK:(bi + 1) * MOBA_BLOCK, :], axis=0, keepdims=True)
        for hd in range(N_HEADS_MOBA):
            in_head = (lane_k >= hd * HEAD_DIM) & (lane_k < (hd + 1) * HEAD_DIM)
            kmean_ref[pl.ds(hd * NBLK_PAD + blk0 + bi, 1), :] = jnp.where(in_head, mean_row, 0.0)

    qat = tr(_T_AQ) * (HEAD_DIM ** -0.5 * LOG2E)
    gate = _dot_3pass(kmean_ref[...], qat)
    own = blk0 + lax.shift_right_logical(
        lax.broadcasted_iota(jnp.int32, (NBLK_PAD, tm), 1), int(math.log2(MOBA_BLOCK)))
    n_io = lax.broadcasted_iota(jnp.int32, (NBLK_PAD, tm), 0)
    past = n_io < own
    zeros_tail = jnp.zeros((LANES - HEAD_DIM - NBLK_PAD, tm), F32)
    for hd in range(N_HEADS_MOBA):
        g = jnp.where(past, gate[hd * NBLK_PAD:(hd + 1) * NBLK_PAD, :], NEG_INF)
        rank = jnp.zeros((NBLK_PAD, tm), jnp.int32)
        for n2 in range(NBLK_PAD):
            r = g[n2:n2 + 1, :]
            tie = jnp.where(n2 < n_io, 1, 0)
            rank = rank + jnp.where(r > g, 1, jnp.where(r == g, tie, 0))
        sel = jnp.where(past, rank, MOBA_TOPK) < MOBA_TOPK
        far = jnp.where(own - n_io >= N_NEAR, far_ref[hd] * LOG2E, 0.0)
        q_feat = jnp.where(sel | (n_io == own), far, NEG_INF)
        qat_ref[0, 0, hd * LANES:(hd + 1) * LANES, :] = jnp.concatenate(
            [qat[hd * HEAD_DIM:(hd + 1) * HEAD_DIM, :], q_feat, zeros_tail], axis=0).astype(BF16)
    store_vt(vat_ref, tr(_T_AV))

    fg = tok(_TOK_FG) + bf_ref[...]
    logf = jnp.minimum(fg, 0.0) - jnp.log1p(jnp.exp(-jnp.abs(fg)))
    csum = jnp.where(lane < N_HEADS_FOX, logf, 0.0)
    sft = 1
    while sft < tm:
        csum = csum + jnp.where(row >= sft, pltpu.roll(csum, sft, axis=0), 0.0)
        sft *= 2
    decay = csum + fcarry_ref[0:1, :]
    fcarry_ref[0:1, :] = decay[tm - 1:tm, :]
    decay = decay * LOG2E
    decay_t = decay.T

    fk = tok(_TOK_FK)
    fqt = tr(_T_FQ) * (HEAD_DIM ** -0.5 * LOG2E)
    sub8 = lax.broadcasted_iota(jnp.int32, (SUBLANES, tm), 0)
    zeros_tail_f = jnp.zeros((LANES - HEAD_DIM - SUBLANES, tm), F32)
    for hd in range(N_HEADS_FOX):
        fcol = jnp.broadcast_to(decay[:, hd:hd + 1], (tm, LANES))
        hi, mid, lo = _bf16_pieces(fcol)
        k_feat_f = jnp.where(lane < FEAT0 + 3, 1.0,
                             jnp.where(lane == FEAT0 + 3, -hi,
                                       jnp.where(lane == FEAT0 + 4, -mid,
                                                 jnp.where(lane == FEAT0 + 5, -lo, 0.0))))
        kf_ref[0, :, hd * LANES:(hd + 1) * LANES] = jnp.where(is_head, _head_slot(fk, hd), k_feat_f).astype(BF16)
        hi, mid, lo = _bf16_pieces(decay_t[hd:hd + 1, :])
        q_feat_f = jnp.where(sub8 == 0, hi,
                             jnp.where(sub8 == 1, mid,
                                       jnp.where(sub8 == 2, lo,
                                                 jnp.where(sub8 < 6, 1.0, 0.0))))
        qft_ref[0, 0, hd * LANES:(hd + 1) * LANES, :] = jnp.concatenate(
            [fqt[hd * HEAD_DIM:(hd + 1) * HEAD_DIM, :], q_feat_f, zeros_tail_f], axis=0).astype(BF16)
    store_vt(vft_ref, tr(_T_FV))

    cqn = _rms(tok(_TOK_CQ), qn_ref[...]).astype(BF16)
    qm_all = _dot_nt(wuq_ref[...], cqn)
    qm = qm_all[0:N_HEADS_MLA * LANES, :]
    qmr = qm_all[N_HEADS_MLA * LANES:, :]
    kvn = _rms(tok(_TOK_CKV), kvn_ref[...]).astype(BF16)
    k_nope = _dot(kvn, wukvk_ref[...])
    k_rope = tok(_TOK_KR) * ctok_ref[...] + tok(_TOK_KRR) * stok_ref[...]
    cos_t = ct_ref[...]
    sin_t = st_ref[...]
    for hd in range(N_HEADS_MLA):
        sl = slice(hd * LANES, (hd + 1) * LANES)
        qmt_ref[0, 0, sl, :] = ((qm[sl, :] * cos_t + qmr[sl, :] * sin_t) * (MLA_QK_DIM ** -0.5 * LOG2E)).astype(BF16)
        km_ref[0, :, sl] = (k_nope[:, sl] + k_rope).astype(BF16)
    store_vt(vmt_ref, _dot_nt(wukvv_ref[...], kvn))


def _proj_call(x, g, p, tables, far, tm):
    b, s, d = x.shape
    nt = s // tm
    n_sub = tm // MOBA_BLOCK
    nq = s // ATT_TILE

    def full(a):
        return pl.BlockSpec(a.shape, lambda bi, ti: (0,) * a.ndim)

    ctok, stok, ct, st = tables
    in_specs = [
        pl.BlockSpec((1, tm, d), lambda bi, ti: (bi, ti, 0)),
        full(g), full(p["w_tok"]), full(p["w_t"]), full(p["b_f"]), full(p["q_norm"]), full(p["kv_norm"]),
        full(p["wuq_t"]), full(p["wukv_k"]), full(p["wukv_v_t"]),
        pl.BlockSpec((tm, LANES), lambda bi, ti: (ti, 0)),
        pl.BlockSpec((tm, LANES), lambda bi, ti: (ti, 0)),
        pl.BlockSpec((LANES, tm), lambda bi, ti: (0, ti)),
        pl.BlockSpec((LANES, tm), lambda bi, ti: (0, ti)),
        pl.BlockSpec(memory_space=pltpu.SMEM),
    ]

    def qt_spec(n_heads):
        return pl.BlockSpec((1, 1, n_heads * LANES, tm), lambda bi, ti: (bi, ti, 0, 0))

    def k_spec(n_heads):
        return pl.BlockSpec((1, tm, n_heads * LANES), lambda bi, ti: (bi, ti, 0))

    def vt_spec(width):
        return pl.BlockSpec((1, 1, width, tm), lambda bi, ti: (bi, ti, 0, 0))

    def qt_shape(n_heads):
        return jax.ShapeDtypeStruct((b, nt, n_heads * LANES, tm), BF16)

    def k_shape(n_heads):
        return jax.ShapeDtypeStruct((b, s, n_heads * LANES), BF16)

    def vt_shape(width):
        return jax.ShapeDtypeStruct((b, nt, width, tm), BF16)

    out_specs = [qt_spec(6), k_spec(6), vt_spec(6 * V_ROWS), qt_spec(6), k_spec(6), vt_spec(6 * V_ROWS),
                 qt_spec(4), k_spec(4), vt_spec(4 * V_ROWS)]
    out_shape = [qt_shape(6), k_shape(6), vt_shape(6 * V_ROWS), qt_shape(6), k_shape(6), vt_shape(6 * V_ROWS),
                 qt_shape(4), k_shape(4), vt_shape(4 * V_ROWS)]
    return pl.pallas_call(
        functools.partial(_proj_kernel, tm=tm),
        grid=(b, nt),
        in_specs=in_specs,
        out_specs=out_specs,
        out_shape=out_shape,
        scratch_shapes=[pltpu.VMEM((N_HEADS_MOBA * NBLK_PAD + 32, W_MOBA), F32),
                        pltpu.VMEM((SUBLANES, LANES), F32)],
        compiler_params=pltpu.CompilerParams(dimension_semantics=("arbitrary", "arbitrary"),
                                             vmem_limit_bytes=VMEM_LIMIT),
        name="proj",
    )(x, g, p["w_tok"], p["w_t"], p["b_f"], p["q_norm"], p["kv_norm"], p["wuq_t"],
      p["wukv_k"], p["wukv_v_t"], ctok, stok, ct, st, far)


def _attn_kernel(*refs, has_bias, tq, tk, nq, n_add, n_plain):
    if has_bias:
        (q_ref, j_ref, qt_ref, k_ref, vt_ref, bidx_ref, tab_ref, o_ref,
         acc_ref, m_ref, s0_ref, s1_ref, s2_ref, s3_ref, bias_ref) = refs
    else:
        (q_ref, j_ref, qt_ref, k_ref, vt_ref, o_ref,
         acc_ref, m_ref, s0_ref, s1_ref, s2_ref, s3_ref, bias_ref) = refs
    hp = pl.program_id(0)
    bi = pl.program_id(1)
    blk = MOBA_BLOCK
    n_near = N_NEAR if has_bias else 1

    @pl.when(bi == 0)
    def _():
        tri = (lax.broadcasted_iota(jnp.int32, (blk, blk), 0) <= lax.broadcasted_iota(jnp.int32, (blk, blk), 1))
        for hh in range(2):
            bias_ref[hh, 0] = jnp.full((blk, blk), NEG_INF, F32)
            bias_ref[hh, n_near + 1] = jnp.zeros((blk, blk), F32)
            if has_bias:
                def build(d, carry, hh=hh):
                    idx = bidx_ref[d]
                    bias_ref[hh, d + 1] = lax.fori_loop(
                        0, T5_BUCKETS,
                        lambda bk, tl: jnp.where(idx == bk, tab_ref[bk, hp * 2 + hh] * LOG2E, tl),
                        jnp.zeros((blk, blk), F32))
                    return carry
                lax.fori_loop(0, n_near, build, 0)
                bias_ref[hh, 1] = jnp.where(tri, bias_ref[hh, 1], NEG_INF)
            else:
                bias_ref[hh, 1] = jnp.where(tri, 0.0, NEG_INF)

    acc_ref[...] = jnp.zeros_like(acc_ref)
    m_ref[...] = jnp.full(m_ref.shape, NEG_INF, F32)
    bufs = (s0_ref, s1_ref, s2_ref, s3_ref)

    n_items = n_add + n_plain

    def scores(e, s_ref, additive):
        qi = q_ref[e]
        j = j_ref[e]
        start = pl.multiple_of(j * tk, tk)
        tmax = []
        for hh in range(2):
            k = k_ref[0, pl.ds(start, tk), hh * LANES:(hh + 1) * LANES]
            s = _dot(k, qt_ref[0, qi, hh * LANES:(hh + 1) * LANES, :])
            if additive:
                s = s + jnp.concatenate([
                    jnp.concatenate([
                        bias_ref[hh, jnp.clip((qi * (tq // blk) + c) - (j * (tk // blk) + r), -1, n_near) + 1]
                        for c in range(tq // blk)], axis=1)
                    for r in range(tk // blk)], axis=0)
            s_ref[hh] = s
            tmax.append(jnp.max(s, axis=0, keepdims=True))
        return tuple(tmax)

    def update(e, s_ref, tmax):
        qi = q_ref[e]
        j = j_ref[e]
        for hh in range(2):
            m_old = m_ref[qi, hh, 0:1, :]
            m_new = jnp.maximum(m_old, tmax[hh])
            alpha = jnp.exp2(m_old - m_new)
            p = jnp.exp2(s_ref[hh] - m_new).astype(BF16)
            v = vt_ref[0, j, hh * V_ROWS:(hh + 1) * V_ROWS, :]
            acc_ref[qi, hh] = alpha * acc_ref[qi, hh] + _dot(v, p)
            m_ref[qi, hh, 0:1, :] = m_new

    def step(e, r, carry, additive):
        tmax_cur, tmax_nxt = carry
        tmax_new = scores(e + 2, bufs[(r + 2) % 4], additive)
        update(e, bufs[r], tmax_cur)
        return tmax_nxt, tmax_new

    def steps(lo, hi, carry, additive):
        n_groups = max(hi - lo, 0) // 4

        def group(g, carry):
            for r in range(4):
                carry = step(lo + 4 * g + r, (lo + r) % 4, carry, additive)
            return carry

        carry = lax.fori_loop(0, n_groups, group, carry)
        for e in range(lo + 4 * n_groups, hi):
            carry = step(e, e % 4, carry, additive)
        return carry

    carry = (scores(0, bufs[0], 0 < n_add), scores(1, bufs[1], 1 < n_add))
    split = max(n_add - 2, 0)
    carry = steps(0, split, carry, True)
    carry = steps(split, n_items - 2, carry, False)
    update(n_items - 2, bufs[(n_items - 2) % 4], carry[0])
    update(n_items - 1, bufs[(n_items - 1) % 4], carry[1])

    def finish(qi, carry):
        o = [acc_ref[qi, hh, 0:HEAD_DIM, :] / acc_ref[qi, hh, HEAD_DIM:HEAD_DIM + 1, :] for hh in range(2)]
        o_ref[0, pl.ds(pl.multiple_of(qi * tq, tq), tq), :] = jnp.concatenate(o, axis=0).T.astype(BF16)
        return carry

    lax.fori_loop(0, nq, finish, 0)


def _attn_schedule(nq, tq, tk, n_near):
    blk = MOBA_BLOCK
    add, plain = [], []
    for q in range(nq):
        for j in range(q, -1, -1):
            min_dist = q * (tq // blk) - (j * (tk // blk) + tk // blk - 1)
            (add if min_dist < n_near else plain).append((q, j))
    items = add + plain
    return (jnp.asarray([q for q, _ in items], jnp.int32), jnp.asarray([j for _, j in items], jnp.int32),
            len(add), len(plain))


def _attn_call(qt, k, vt, n_pairs, bias_inputs=None):
    b, nq, _, tq = qt.shape
    _, s, _ = k.shape
    tk = vt.shape[-1]
    assert tq == tk, "the causal-triangle tile assumes equal query and key tiles"
    has_bias = bias_inputs is not None
    q_tab, j_tab, n_add, n_plain = _attn_schedule(nq, tq, tk, N_NEAR if has_bias else 1)
    assert n_add + n_plain >= 2
    smem = pl.BlockSpec(memory_space=pltpu.SMEM)
    in_specs = [
        smem, smem,
        pl.BlockSpec((1, nq, 2 * LANES, tq), lambda p, bi: (bi, 0, p, 0)),
        pl.BlockSpec((1, s, 2 * LANES), lambda p, bi: (bi, 0, p)),
        pl.BlockSpec((1, s // tk, 2 * V_ROWS, tk), lambda p, bi: (bi, 0, p, 0)),
    ]
    args = [q_tab, j_tab, qt, k, vt]
    n_tiles = (N_NEAR if has_bias else 1) + 2
    scratch = [pltpu.VMEM((nq, 2, V_ROWS, tq), F32), pltpu.VMEM((nq, 2, SUBLANES, tq), F32),
               pltpu.VMEM((2, tk, tq), F32), pltpu.VMEM((2, tk, tq), F32),
               pltpu.VMEM((2, tk, tq), F32), pltpu.VMEM((2, tk, tq), F32),
               pltpu.VMEM((2, n_tiles, MOBA_BLOCK, MOBA_BLOCK), F32)]
    if has_bias:
        bidx, tab = bias_inputs
        in_specs += [pl.BlockSpec(bidx.shape, lambda p, bi: (0, 0, 0)), smem]
        args += [bidx, tab]
    return pl.pallas_call(
        functools.partial(_attn_kernel, has_bias=has_bias, tq=tq, tk=tk, nq=nq, n_add=n_add, n_plain=n_plain),
        grid=(n_pairs, b),
        in_specs=in_specs,
        out_specs=pl.BlockSpec((1, s, 2 * HEAD_DIM), lambda p, bi: (bi, 0, p)),
        out_shape=jax.ShapeDtypeStruct((b, s, n_pairs * 2 * HEAD_DIM), BF16),
        scratch_shapes=scratch,
        compiler_params=pltpu.CompilerParams(dimension_semantics=("arbitrary",) * 2,
                                             vmem_limit_bytes=VMEM_LIMIT),
        name="attn_bias" if has_bias else "attn",
    )(*args)


def _oproj_kernel(x_ref, oa_ref, ob_ref, oc_ref, wo_ref, g_ref, o_ref):
    merged = jnp.concatenate([oa_ref[0], ob_ref[0], oc_ref[0]], axis=1)
    o_ref[0] = x_ref[0] + _rms(_dot(merged, wo_ref[...]), g_ref[...])


def _oproj_call(x, oa, ob, oc, p, g, tm):
    b, s, d = x.shape

    def rows(width):
        return pl.BlockSpec((1, tm, width), lambda bi, ti: (bi, ti, 0))

    def full(a):
        return pl.BlockSpec(a.shape, lambda bi, ti: (0,) * a.ndim)

    return pl.pallas_call(
        _oproj_kernel,
        grid=(b, s // tm),
        in_specs=[rows(d), rows(W_MOBA), rows(W_MLA), rows(W_FOX),
                  full(p["w_o"]), full(g)],
        out_specs=rows(d),
        out_shape=jax.ShapeDtypeStruct(x.shape, F32),
        compiler_params=pltpu.CompilerParams(dimension_semantics=("arbitrary", "arbitrary"),
                                             vmem_limit_bytes=VMEM_LIMIT),
        name="oproj",
    )(x, oa, ob, oc, p["w_o"], g)


def _ffn_kernel(x_ref, xp_ref, gpre_ref, wup_ref, cw_ref, cb_ref, wd_ref, gpost_ref, o_ref, h_ref, u_ref, *, tm):
    ti = pl.program_id(1)
    halo = FFN_HALO

    h_ref[halo:, :] = _rms(x_ref[0], gpre_ref[...]).astype(BF16)
    prev = _rms(xp_ref[0], gpre_ref[...])
    h_ref[0:halo, :] = jnp.where(ti == 0, 0.0, prev).astype(BF16)

    h = h_ref[...]
    for half in range(2):
        u_ref[half] = _dot(h, wup_ref[:, half * D_FF:(half + 1) * D_FF])

    def conv(half):
        cols = slice(half * D_FF, (half + 1) * D_FF)
        return (cw_ref[0:1, cols] * u_ref[half, halo - 2:halo - 2 + tm, :]
                + cw_ref[1:2, cols] * u_ref[half, halo - 1:halo - 1 + tm, :]
                + cw_ref[2:3, cols] * u_ref[half, halo:halo + tm, :]) + cb_ref[:, cols]

    act = (jax.nn.gelu(conv(0), approximate=True) * conv(1)).astype(BF16)
    o_ref[0] = x_ref[0] + _rms(_dot(act, wd_ref[...]), gpost_ref[...])


def _ffn_call(x, p, gpre, gpost, tm):
    b, s, d = x.shape
    halo = FFN_HALO
    blocks_per_tile = tm // halo

    def full(a):
        return pl.BlockSpec(a.shape, lambda bi, ti: (0,) * a.ndim)

    in_specs = [
        pl.BlockSpec((1, tm, d), lambda bi, ti: (bi, ti, 0)),
        pl.BlockSpec((1, halo, d), lambda bi, ti: (bi, jnp.maximum(ti * blocks_per_tile - 1, 0), 0)),
        full(gpre), full(p["w_up"]), full(p["conv_w"]), full(p["conv_b"]), full(p["w_down"]), full(gpost),
    ]
    return pl.pallas_call(
        functools.partial(_ffn_kernel, tm=tm),
        grid=(b, s // tm),
        in_specs=in_specs,
        out_specs=pl.BlockSpec((1, tm, d), lambda bi, ti: (bi, ti, 0)),
        out_shape=jax.ShapeDtypeStruct(x.shape, F32),
        scratch_shapes=[pltpu.VMEM((halo + tm, d), BF16), pltpu.VMEM((2, halo + tm, D_FF), F32)],
        compiler_params=pltpu.CompilerParams(dimension_semantics=("arbitrary",) * 2,
                                             vmem_limit_bytes=VMEM_LIMIT),
        name="ffn",
    )(x, x, gpre, p["w_up"], p["conv_w"], p["conv_b"], p["w_down"], gpost)


def _pad_cols(w, left, total):
    return jnp.pad(w, ((0, 0), (left, total - left - w.shape[1])))


def _prep_layer(w_in, b_f, q_norm, kv_norm, w_uq, w_ukv, w_o, w_up, conv_w, conv_b, w_down):
    offs = [0]
    for sz in PROJ_SIZES:
        offs.append(offs[-1] + sz)
    a_q, a_k, a_v, c_q, c_kv, k_r, f_q, f_k, f_v, f_g = [w_in[:, offs[i]:offs[i + 1]] for i in range(10)]
    half = MLA_ROPE_DIM // 2

    def rot(w):
        return jnp.concatenate([-w[:, half:], w[:, :half]], axis=1)

    w_tok = jnp.concatenate([
        a_k, f_k, c_q, c_kv,
        _pad_cols(k_r, FEAT0, LANES), _pad_cols(rot(k_r), FEAT0, LANES), _pad_cols(f_g, 0, LANES)], axis=1)
    w_t = jnp.concatenate([a_q, f_q, a_v, f_v], axis=1).T
    uq = w_uq.reshape(MLA_Q_RANK, N_HEADS_MLA, MLA_QK_DIM)
    uq_main = jnp.pad(uq, ((0, 0), (0, 0), (0, LANES - MLA_QK_DIM)))
    uq_rope = uq[:, :, MLA_NOPE_DIM:]
    uq_rot = jnp.concatenate([-uq_rope[:, :, half:], uq_rope[:, :, :half]], axis=2)
    uq_rot = jnp.pad(uq_rot, ((0, 0), (0, 0), (MLA_NOPE_DIM, LANES - MLA_QK_DIM)))
    ukv = w_ukv.reshape(MLA_KV_RANK, N_HEADS_MLA, MLA_NOPE_DIM + MLA_V_DIM)
    ukv_k = jnp.pad(ukv[:, :, :MLA_NOPE_DIM], ((0, 0), (0, 0), (0, LANES - MLA_NOPE_DIM)))
    ukv_v = ukv[:, :, MLA_NOPE_DIM:]
    return {
        "w_tok": w_tok.astype(BF16),
        "w_t": w_t.astype(BF16),
        "b_f": _pad_cols(b_f[None, :], 0, LANES),
        "q_norm": q_norm[None, :],
        "kv_norm": kv_norm[None, :],
        "wuq_t": jnp.concatenate([uq_main.reshape(MLA_Q_RANK, N_HEADS_MLA * LANES),
                                  uq_rot.reshape(MLA_Q_RANK, N_HEADS_MLA * LANES)], axis=1).T.astype(BF16),
        "wukv_k": ukv_k.reshape(MLA_KV_RANK, N_HEADS_MLA * LANES).astype(BF16),
        "wukv_v_t": ukv_v.reshape(MLA_KV_RANK, W_MLA).T.astype(BF16),
        "w_o": w_o.astype(BF16),
        "w_up": w_up.astype(BF16),
        "conv_w": conv_w,
        "conv_b": conv_b[None, :],
        "w_down": w_down.astype(BF16),
    }


def _rope_tables(s):
    half = MLA_ROPE_DIM // 2
    inv = ROPE_THETA ** (-jnp.arange(half, dtype=F32) / half)
    ang = jnp.arange(s).astype(F32)[:, None] * inv[None, :]
    cos = jnp.concatenate([jnp.cos(ang)] * 2, axis=1)
    sin = jnp.concatenate([jnp.sin(ang)] * 2, axis=1)
    ctok = _pad_cols(cos, FEAT0, LANES)
    stok = _pad_cols(sin, FEAT0, LANES)
    ones = jnp.ones((s, MLA_NOPE_DIM), F32)
    ct = jnp.concatenate([ones, cos, jnp.zeros((s, LANES - MLA_QK_DIM), F32)], axis=1).T
    st = stok.T
    return ctok, stok, ct, st


def _t5_bucket_tiles(tile):
    d = jnp.arange(N_NEAR)[:, None, None]
    j = jnp.arange(tile)[None, :, None]
    i = jnp.arange(tile)[None, None, :]
    n = jnp.maximum(d * tile + i - j, 0)
    exact = T5_BUCKETS // 2
    large = exact + (jnp.log(jnp.maximum(n, 1).astype(F32) / exact)
                     / math.log(T5_MAX_DIST / exact) * (T5_BUCKETS - exact)).astype(jnp.int32)
    return jnp.where(n < exact, n, jnp.minimum(large, T5_BUCKETS - 1)).astype(jnp.int32)


def kernel(x, rel_bias, ln_mix_pre, ln_mix_post, ln_ffn_pre, ln_ffn_post, w_in, b_f, q_norm, kv_norm, w_uq,
           w_ukv, w_o, w_up, conv_w, conv_b, w_down):
    b, s, d = x.shape
    depth = w_in.shape[0]
    assert s % PROJ_ROWS == 0 and s % FFN_ROWS == 0 and s % OPROJ_ROWS == 0
    assert s // MOBA_BLOCK <= NBLK_PAD
    tables = _rope_tables(s)
    bidx = _t5_bucket_tiles(MOBA_BLOCK)
    far = rel_bias[T5_BUCKETS - 1, :]
    for l in range(depth):
        p = _prep_layer(w_in[l], b_f[l], q_norm[l], kv_norm[l], w_uq[l], w_ukv[l], w_o[l], w_up[l],
                        conv_w[l], conv_b[l], w_down[l])
        qat, ka, vat, qft, kf, vft, qmt, km, vmt = _proj_call(x, ln_mix_pre[l][None, :], p, tables, far, PROJ_ROWS)
        oa = _attn_call(qat, ka, vat, N_HEADS_MOBA // 2, (bidx, rel_bias))
        ob = _attn_call(qmt, km, vmt, N_HEADS_MLA // 2)
        oc = _attn_call(qft, kf, vft, N_HEADS_FOX // 2)
        x = _oproj_call(x, oa, ob, oc, p, ln_mix_post[l][None, :], OPROJ_ROWS)
        x = _ffn_call(x, p, ln_ffn_pre[l][None, :], ln_ffn_post[l][None, :], FFN_ROWS)
    return x
```

```python
import functools
import math

import jax
import jax.numpy as jnp
from jax import lax
from jax.experimental import pallas as pl
from jax.experimental.pallas import tpu as pltpu

F32 = jnp.float32
BF16 = jnp.bfloat16

HEAD_DIM = 64
N_HEADS_MOBA = 6
N_HEADS_MLA = 4
N_HEADS_FOX = 6
MOBA_BLOCK = 256
MOBA_TOPK = 3
MLA_Q_RANK = 256
MLA_KV_RANK = 128
MLA_NOPE_DIM = 64
MLA_ROPE_DIM = 32
MLA_V_DIM = 64
MLA_QK_DIM = MLA_NOPE_DIM + MLA_ROPE_DIM
ROPE_THETA = 10000.0
T5_BUCKETS = 32
T5_MAX_DIST = 1024
D_FF = 2816
CONV_WIDTH = 3
NORM_EPS = 1e-6
NEG_INF = -1e30
W_MOBA = N_HEADS_MOBA * HEAD_DIM
W_MLA = N_HEADS_MLA * MLA_V_DIM
W_FOX = N_HEADS_FOX * HEAD_DIM
PROJ_SIZES = (W_MOBA, W_MOBA, W_MOBA, MLA_Q_RANK, MLA_KV_RANK, MLA_ROPE_DIM, W_FOX, W_FOX, W_FOX, N_HEADS_FOX)

LANES = 128
SUBLANES = 8
VMEM_LIMIT = 56 * 1024 * 1024

ATT_TILE = 512
KV_TILE = 512
PROJ_ROWS = KV_TILE
ATT_GROUP = 8
OPROJ_ROWS = 1024
FFN_ROWS = 512
FFN_HALO = 16
FEAT0 = HEAD_DIM
V_ROWS = HEAD_DIM + 16
LOG2E = math.log2(math.e)
NBLK_PAD = 16
N_NEAR = (T5_MAX_DIST + MOBA_BLOCK - 1 + MOBA_BLOCK - 1) // MOBA_BLOCK

_TOK_AK = (0, 384)
_TOK_FK = (384, 768)
_TOK_CQ = (768, 1024)
_TOK_CKV = (1024, 1152)
_TOK_KR = (1152, 1280)
_TOK_KRR = (1280, 1408)
_TOK_FG = (1408, 1536)
_TOK_COLS = 1536
_T_AQ = (0, 384)
_T_FQ = (384, 768)
_T_AV = (768, 1152)
_T_FV = (1152, 1536)


def _rms(xf, g):
    return xf * lax.rsqrt(jnp.mean(xf * xf, axis=-1, keepdims=True) + NORM_EPS) * g


def _dot(a, b):
    return jnp.dot(a, b, preferred_element_type=F32)


def _dot_nt(a, b):
    return lax.dot_general(a, b, (((1,), (1,)), ((), ())), preferred_element_type=F32)


def _bf16_pieces(x):
    hi = x.astype(BF16).astype(F32)
    r = x - hi
    mid = r.astype(BF16).astype(F32)
    lo = (r - mid).astype(BF16).astype(F32)
    return hi, mid, lo


def _dot_3pass(a, b):
    a_hi = a.astype(BF16)
    a_lo = (a - a_hi.astype(F32)).astype(BF16)
    b_hi = b.astype(BF16)
    b_lo = (b - b_hi.astype(F32)).astype(BF16)
    return _dot(jnp.concatenate([a_hi, a_hi, a_lo], axis=1), jnp.concatenate([b_hi, b_lo, b_hi], axis=0))


def _head_slot(arr, hd):
    pair = arr[:, (hd // 2) * LANES:(hd // 2 + 1) * LANES]
    if hd % 2:
        pair = pltpu.roll(pair, HEAD_DIM, axis=1)
    return pair


def _proj_kernel(x_ref, g_ref, wtok_ref, wt_ref, bf_ref, qn_ref, kvn_ref, wuq_ref,
                 wukvk_ref, wukvv_ref, ctok_ref, stok_ref, ct_ref, st_ref, far_ref,
                 qat_ref, ka_ref, vat_ref, qft_ref, kf_ref, vft_ref, qmt_ref, km_ref, vmt_ref,
                 kmean_ref, fcarry_ref, *, tm):
    t = pl.program_id(1)
    n_sub = tm // MOBA_BLOCK

    @pl.when(t == 0)
    def _():
        kmean_ref[...] = jnp.zeros_like(kmean_ref)
        fcarry_ref[...] = jnp.zeros_like(fcarry_ref)

    h = _rms(x_ref[0], g_ref[...]).astype(BF16)
    lane = lax.broadcasted_iota(jnp.int32, (tm, LANES), 1)
    row = lax.broadcasted_iota(jnp.int32, (tm, LANES), 0)
    blk0 = t * n_sub
    is_head = lane < HEAD_DIM

    tok_all = _dot(h, wtok_ref[...])
    t_all = _dot_nt(wt_ref[...], h)

    def tok(rng):
        return tok_all[:, rng[0]:rng[1]]

    def tr(rng):
        return t_all[rng[0]:rng[1], :]

    sub8v = lax.broadcasted_iota(jnp.int32, (SUBLANES, tm), 0)
    ones_row = jnp.where(sub8v == 0, 1.0, 0.0)
    v_pad = jnp.zeros((V_ROWS - HEAD_DIM - SUBLANES, tm), F32)

    def store_vt(ref, vt):
        n_heads = vt.shape[0] // HEAD_DIM
        ref[0, 0] = jnp.concatenate(
            [piece for hd in range(n_heads)
             for piece in (vt[hd * HEAD_DIM:(hd + 1) * HEAD_DIM, :], ones_row, v_pad)], axis=0).astype(BF16)

    ak = tok(_TOK_AK)
    blk_row = blk0 + lax.shift_right_logical(row, int(math.log2(MOBA_BLOCK)))
    k_feat = jnp.where(lane == FEAT0 + blk_row, 1.0, 0.0)
    for hd in range(N_HEADS_MOBA):
        ka_ref[0, :, hd * LANES:(hd + 1) * LANES] = jnp.where(is_head, _head_slot(ak, hd), k_feat).astype(BF16)

    lane_k = lax.broadcasted_iota(jnp.int32, (1, W_MOBA), 1)
    for bi in range(n_sub):
        mean_row = jnp.mean(ak[bi * MOBA_BLOCK:(bi + 1) * MOBA_BLOCK, :], axis=0, keepdims=True)
        for hd in range(N_HEADS_MOBA):
            in_head = (lane_k >= hd * HEAD_DIM) & (lane_k < (hd + 1) * HEAD_DIM)
            kmean_ref[pl.ds(hd * NBLK_PAD + blk0 + bi, 1), :] = jnp.where(in_head, mean_row, 0.0)

    qat = tr(_T_AQ) * (HEAD_DIM ** -0.5 * LOG2E)
    gate = _dot_3pass(kmean_ref[...], qat)
    own = blk0 + lax.shift_right_logical(
        lax.broadcasted_iota(jnp.int32, (NBLK_PAD, tm), 1), int(math.log2(MOBA_BLOCK)))
    n_io = lax.broadcasted_iota(jnp.int32, (NBLK_PAD, tm), 0)
    past = n_io < own
    zeros_tail = jnp.zeros((LANES - HEAD_DIM - NBLK_PAD, tm), F32)
    for hd in range(N_HEADS_MOBA):
        g = jnp.where(past, gate[hd * NBLK_PAD:(hd + 1) * NBLK_PAD, :], NEG_INF)
        rank = jnp.zeros((NBLK_PAD, tm), jnp.int32)
        for n2 in range(NBLK_PAD):
            r = g[n2:n2 + 1, :]
            tie = jnp.where(n2 < n_io, 1, 0)
            rank = rank + jnp.where(r > g, 1, jnp.where(r == g, tie, 0))
        sel = jnp.where(past, rank, MOBA_TOPK) < MOBA_TOPK
        far = jnp.where(own - n_io >= N_NEAR, far_ref[hd] * LOG2E, 0.0)
        q_feat = jnp.where(sel | (n_io == own), far, NEG_INF)
        qat_ref[0, 0, hd * LANES:(hd + 1) * LANES, :] = jnp.concatenate(
            [qat[hd * HEAD_DIM:(hd + 1) * HEAD_DIM, :], q_feat, zeros_tail], axis=0).astype(BF16)
    store_vt(vat_ref, tr(_T_AV))

    fg = tok(_TOK_FG) + bf_ref[...]
    logf = jnp.minimum(fg, 0.0) - jnp.log1p(jnp.exp(-jnp.abs(fg)))
    csum = jnp.where(lane < N_HEADS_FOX, logf, 0.0)
    sft = 1
    while sft < tm:
        csum = csum + jnp.where(row >= sft, pltpu.roll(csum, sft, axis=0), 0.0)
        sft *= 2
    decay = csum + fcarry_ref[0:1, :]
    fcarry_ref[0:1, :] = decay[tm - 1:tm, :]
    decay = decay * LOG2E
    decay_t = decay.T

    fk = tok(_TOK_FK)
    fqt = tr(_T_FQ) * (HEAD_DIM ** -0.5 * LOG2E)
    sub8 = lax.broadcasted_iota(jnp.int32, (SUBLANES, tm), 0)
    zeros_tail_f = jnp.zeros((LANES - HEAD_DIM - SUBLANES, tm), F32)
    for hd in range(N_HEADS_FOX):
        fcol = jnp.broadcast_to(decay[:, hd:hd + 1], (tm, LANES))
        hi, mid, lo = _bf16_pieces(fcol)
        k_feat_f = jnp.where(lane < FEAT0 + 3, 1.0,
                             jnp.where(lane == FEAT0 + 3, -hi,
                                       jnp.where(lane == FEAT0 + 4, -mid,
                                                 jnp.where(lane == FEAT0 + 5, -lo, 0.0))))
        kf_ref[0, :, hd * LANES:(hd + 1) * LANES] = jnp.where(is_head, _head_slot(fk, hd), k_feat_f).astype(BF16)
        hi, mid, lo = _bf16_pieces(decay_t[hd:hd + 1, :])
        q_feat_f = jnp.where(sub8 == 0, hi,
                             jnp.where(sub8 == 1, mid,
                                       jnp.where(sub8 == 2, lo,
                                                 jnp.where(sub8 < 6, 1.0, 0.0))))
        qft_ref[0, 0, hd * LANES:(hd + 1) * LANES, :] = jnp.concatenate(
            [fqt[hd * HEAD_DIM:(hd + 1) * HEAD_DIM, :], q_feat_f, zeros_tail_f], axis=0).astype(BF16)
    store_vt(vft_ref, tr(_T_FV))

    cqn = _rms(tok(_TOK_CQ), qn_ref[...]).astype(BF16)
    qm_all = _dot_nt(wuq_ref[...], cqn)
    qm = qm_all[0:N_HEADS_MLA * LANES, :]
    qmr = qm_all[N_HEADS_MLA * LANES:, :]
    kvn = _rms(tok(_TOK_CKV), kvn_ref[...]).astype(BF16)
    k_nope = _dot(kvn, wukvk_ref[...])
    k_rope = tok(_TOK_KR) * ctok_ref[...] + tok(_TOK_KRR) * stok_ref[...]
    cos_t = ct_ref[...]
    sin_t = st_ref[...]
    for hd in range(N_HEADS_MLA):
        sl = slice(hd * LANES, (hd + 1) * LANES)
        qmt_ref[0, 0, sl, :] = ((qm[sl, :] * cos_t + qmr[sl, :] * sin_t) * (MLA_QK_DIM ** -0.5 * LOG2E)).astype(BF16)
        km_ref[0, :, sl] = (k_nope[:, sl] + k_rope).astype(BF16)
    store_vt(vmt_ref, _dot_nt(wukvv_ref[...], kvn))


def _proj_call(x, g, p, tables, far, tm):
    b, s, d = x.shape
    nt = s // tm
    n_sub = tm // MOBA_BLOCK
    nq = s // ATT_TILE

    def full(a):
        return pl.BlockSpec(a.shape, lambda bi, ti: (0,) * a.ndim)

    ctok, stok, ct, st = tables
    in_specs = [
        pl.BlockSpec((1, tm, d), lambda bi, ti: (bi, ti, 0)),
        full(g), full(p["w_tok"]), full(p["w_t"]), full(p["b_f"]), full(p["q_norm"]), full(p["kv_norm"]),
        full(p["wuq_t"]), full(p["wukv_k"]), full(p["wukv_v_t"]),
        pl.BlockSpec((tm, LANES), lambda bi, ti: (ti, 0)),
        pl.BlockSpec((tm, LANES), lambda bi, ti: (ti, 0)),
        pl.BlockSpec((LANES, tm), lambda bi, ti: (0, ti)),
        pl.BlockSpec((LANES, tm), lambda bi, ti: (0, ti)),
        pl.BlockSpec(memory_space=pltpu.SMEM),
    ]

    def qt_spec(n_heads):
        return pl.BlockSpec((1, 1, n_heads * LANES, tm), lambda bi, ti: (bi, ti, 0, 0))

    def k_spec(n_heads):
        return pl.BlockSpec((1, tm, n_heads * LANES), lambda bi, ti: (bi, ti, 0))

    def vt_spec(width):
        return pl.BlockSpec((1, 1, width, tm), lambda bi, ti: (bi, ti, 0, 0))

    def qt_shape(n_heads):
        return jax.ShapeDtypeStruct((b, nt, n_heads * LANES, tm), BF16)

    def k_shape(n_heads):
        return jax.ShapeDtypeStruct((b, s, n_heads * LANES), BF16)

    def vt_shape(width):
        return jax.ShapeDtypeStruct((b, nt, width, tm), BF16)

    out_specs = [qt_spec(6), k_spec(6), vt_spec(6 * V_ROWS), qt_spec(6), k_spec(6), vt_spec(6 * V_ROWS),
                 qt_spec(4), k_spec(4), vt_spec(4 * V_ROWS)]
    out_shape = [qt_shape(6), k_shape(6), vt_shape(6 * V_ROWS), qt_shape(6), k_shape(6), vt_shape(6 * V_ROWS),
                 qt_shape(4), k_shape(4), vt_shape(4 * V_ROWS)]
    return pl.pallas_call(
        functools.partial(_proj_kernel, tm=tm),
        grid=(b, nt),
        in_specs=in_specs,
        out_specs=out_specs,
        out_shape=out_shape,
        scratch_shapes=[pltpu.VMEM((N_HEADS_MOBA * NBLK_PAD + 32, W_MOBA), F32),
                        pltpu.VMEM((SUBLANES, LANES), F32)],
        compiler_params=pltpu.CompilerParams(dimension_semantics=("arbitrary", "arbitrary"),
                                             vmem_limit_bytes=VMEM_LIMIT),
        name="proj",
    )(x, g, p["w_tok"], p["w_t"], p["b_f"], p["q_norm"], p["kv_norm"], p["wuq_t"],
      p["wukv_k"], p["wukv_v_t"], ctok, stok, ct, st, far)


def _attn_kernel(*refs, has_bias, tq, tk, nq, n_add, n_plain):
    if has_bias:
        (q_ref, j_ref, qt_ref, k_ref, vt_ref, bidx_ref, tab_ref, o_ref,
         acc_ref, m_ref, s0_ref, s1_ref, s2_ref, s3_ref, bias_ref) = refs
    else:
        (q_ref, j_ref, qt_ref, k_ref, vt_ref, o_ref,
         acc_ref, m_ref, s0_ref, s1_ref, s2_ref, s3_ref, bias_ref) = refs
    hp = pl.program_id(0)
    bi = pl.program_id(1)
    blk = MOBA_BLOCK
    n_near = N_NEAR if has_bias else 1

    @pl.when(bi == 0)
    def _():
        tri = (lax.broadcasted_iota(jnp.int32, (blk, blk), 0) <= lax.broadcasted_iota(jnp.int32, (blk, blk), 1))
        for hh in range(2):
            bias_ref[hh, 0] = jnp.full((blk, blk), NEG_INF, F32)
            bias_ref[hh, n_near + 1] = jnp.zeros((blk, blk), F32)
            if has_bias:
                def build(d, carry, hh=hh):
                    idx = bidx_ref[d]
                    bias_ref[hh, d + 1] = lax.fori_loop(
                        0, T5_BUCKETS,
                        lambda bk, tl: jnp.where(idx == bk, tab_ref[bk, hp * 2 + hh] * LOG2E, tl),
                        jnp.zeros((blk, blk), F32))
                    return carry
                lax.fori_loop(0, n_near, build, 0)
                bias_ref[hh, 1] = jnp.where(tri, bias_ref[hh, 1], NEG_INF)
            else:
                bias_ref[hh, 1] = jnp.where(tri, 0.0, NEG_INF)

    acc_ref[...] = jnp.zeros_like(acc_ref)
    m_ref[...] = jnp.full(m_ref.shape, NEG_INF, F32)
    bufs = (s0_ref, s1_ref, s2_ref, s3_ref)

    n_items = n_add + n_plain

    def scores(e, s_ref, additive):
        qi = q_ref[e]
        j = j_ref[e]
        start = pl.multiple_of(j * tk, tk)
        tmax = []
        for hh in range(2):
            k = k_ref[0, pl.ds(start, tk), hh * LANES:(hh + 1) * LANES]
            s = _dot(k, qt_ref[0, qi, hh * LANES:(hh + 1) * LANES, :])
            if additive:
                s = s + jnp.concatenate([
                    jnp.concatenate([
                        bias_ref[hh, jnp.clip((qi * (tq // blk) + c) - (j * (tk // blk) + r), -1, n_near) + 1]
                        for c in range(tq // blk)], axis=1)
                    for r in range(tk // blk)], axis=0)
            s_ref[hh] = s
            tmax.append(jnp.max(s, axis=0, keepdims=True))
        return tuple(tmax)

    def update(e, s_ref, tmax):
        qi = q_ref[e]
        j = j_ref[e]
        for hh in range(2):
            m_old = m_ref[qi, hh, 0:1, :]
            m_new = jnp.maximum(m_old, tmax[hh])
            alpha = jnp.exp2(m_old - m_new)
            p = jnp.exp2(s_ref[hh] - m_new).astype(BF16)
            v = vt_ref[0, j, hh * V_ROWS:(hh + 1) * V_ROWS, :]
            acc_ref[qi, hh] = alpha * acc_ref[qi, hh] + _dot(v, p)
            m_ref[qi, hh, 0:1, :] = m_new

    def step(e, r, carry, additive):
        tmax_cur, tmax_nxt = carry
        tmax_new = scores(e + 2, bufs[(r + 2) % 4], additive)
        update(e, bufs[r], tmax_cur)
        return tmax_nxt, tmax_new

    def steps(lo, hi, carry, additive):
        n_groups = max(hi - lo, 0) // ATT_GROUP

        def group(g, carry):
            for r in range(ATT_GROUP):
                carry = step(lo + ATT_GROUP * g + r, (lo + r) % 4, carry, additive)
            return carry

        carry = lax.fori_loop(0, n_groups, group, carry)
        for e in range(lo + ATT_GROUP * n_groups, hi):
            carry = step(e, e % 4, carry, additive)
        return carry

    carry = (scores(0, bufs[0], 0 < n_add), scores(1, bufs[1], 1 < n_add))
    split = max(n_add - 2, 0)
    carry = steps(0, split, carry, True)
    carry = steps(split, n_items - 2, carry, False)
    update(n_items - 2, bufs[(n_items - 2) % 4], carry[0])
    update(n_items - 1, bufs[(n_items - 1) % 4], carry[1])

    def finish(qi, carry):
        o = [acc_ref[qi, hh, 0:HEAD_DIM, :] / acc_ref[qi, hh, HEAD_DIM:HEAD_DIM + 1, :] for hh in range(2)]
        o_ref[0, pl.ds(pl.multiple_of(qi * tq, tq), tq), :] = jnp.concatenate(o, axis=0).T.astype(BF16)
        return carry

    lax.fori_loop(0, nq, finish, 0)


def _attn_schedule(nq, tq, tk, n_near):
    blk = MOBA_BLOCK
    add, plain = [], []
    for q in range(nq):
        for j in range(q, -1, -1):
            min_dist = q * (tq // blk) - (j * (tk // blk) + tk // blk - 1)
            (add if min_dist < n_near else plain).append((q, j))
    items = add + plain
    return (jnp.asarray([q for q, _ in items], jnp.int32), jnp.asarray([j for _, j in items], jnp.int32),
            len(add), len(plain))


def _attn_call(qt, k, vt, n_pairs, bias_inputs=None):
    b, nq, _, tq = qt.shape
    _, s, _ = k.shape
    tk = vt.shape[-1]
    assert tq == tk, "the causal-triangle tile assumes equal query and key tiles"
    has_bias = bias_inputs is not None
    q_tab, j_tab, n_add, n_plain = _attn_schedule(nq, tq, tk, N_NEAR if has_bias else 1)
    assert n_add + n_plain >= 2
    smem = pl.BlockSpec(memory_space=pltpu.SMEM)
    in_specs = [
        smem, smem,
        pl.BlockSpec((1, nq, 2 * LANES, tq), lambda p, bi: (bi, 0, p, 0)),
        pl.BlockSpec((1, s, 2 * LANES), lambda p, bi: (bi, 0, p)),
        pl.BlockSpec((1, s // tk, 2 * V_ROWS, tk), lambda p, bi: (bi, 0, p, 0)),
    ]
    args = [q_tab, j_tab, qt, k, vt]
    n_tiles = (N_NEAR if has_bias else 1) + 2
    scratch = [pltpu.VMEM((nq, 2, V_ROWS, tq), F32), pltpu.VMEM((nq, 2, SUBLANES, tq), F32),
               pltpu.VMEM((2, tk, tq), F32), pltpu.VMEM((2, tk, tq), F32),
               pltpu.VMEM((2, tk, tq), F32), pltpu.VMEM((2, tk, tq), F32),
               pltpu.VMEM((2, n_tiles, MOBA_BLOCK, MOBA_BLOCK), F32)]
    if has_bias:
        bidx, tab = bias_inputs
        in_specs += [pl.BlockSpec(bidx.shape, lambda p, bi: (0, 0, 0)), smem]
        args += [bidx, tab]
    return pl.pallas_call(
        functools.partial(_attn_kernel, has_bias=has_bias, tq=tq, tk=tk, nq=nq, n_add=n_add, n_plain=n_plain),
        grid=(n_pairs, b),
        in_specs=in_specs,
        out_specs=pl.BlockSpec((1, s, 2 * HEAD_DIM), lambda p, bi: (bi, 0, p)),
        out_shape=jax.ShapeDtypeStruct((b, s, n_pairs * 2 * HEAD_DIM), BF16),
        scratch_shapes=scratch,
        compiler_params=pltpu.CompilerParams(dimension_semantics=("arbitrary",) * 2,
                                             vmem_limit_bytes=VMEM_LIMIT),
        name="attn_bias" if has_bias else "attn",
    )(*args)


def _oproj_kernel(x_ref, oa_ref, ob_ref, oc_ref, wo_ref, g_ref, o_ref):
    merged = jnp.concatenate([oa_ref[0], ob_ref[0], oc_ref[0]], axis=1)
    o_ref[0] = x_ref[0] + _rms(_dot(merged, wo_ref[...]), g_ref[...])


def _oproj_call(x, oa, ob, oc, p, g, tm):
    b, s, d = x.shape

    def rows(width):
        return pl.BlockSpec((1, tm, width), lambda bi, ti: (bi, ti, 0))

    def full(a):
        return pl.BlockSpec(a.shape, lambda bi, ti: (0,) * a.ndim)

    return pl.pallas_call(
        _oproj_kernel,
        grid=(b, s // tm),
        in_specs=[rows(d), rows(W_MOBA), rows(W_MLA), rows(W_FOX),
                  full(p["w_o"]), full(g)],
        out_specs=rows(d),
        out_shape=jax.ShapeDtypeStruct(x.shape, F32),
        compiler_params=pltpu.CompilerParams(dimension_semantics=("arbitrary", "arbitrary"),
                                             vmem_limit_bytes=VMEM_LIMIT),
        name="oproj",
    )(x, oa, ob, oc, p["w_o"], g)


def _ffn_kernel(x_ref, xp_ref, gpre_ref, wup_ref, cw_ref, cb_ref, wd_ref, gpost_ref, o_ref, h_ref, u_ref, *, tm):
    ti = pl.program_id(1)
    halo = FFN_HALO

    h_ref[halo:, :] = _rms(x_ref[0], gpre_ref[...]).astype(BF16)
    prev = _rms(xp_ref[0], gpre_ref[...])
    h_ref[0:halo, :] = jnp.where(ti == 0, 0.0, prev).astype(BF16)

    h = h_ref[...]
    for half in range(2):
        u_ref[half] = _dot(h, wup_ref[:, half * D_FF:(half + 1) * D_FF])

    def conv(half):
        cols = slice(half * D_FF, (half + 1) * D_FF)
        return (cw_ref[0:1, cols] * u_ref[half, halo - 2:halo - 2 + tm, :]
                + cw_ref[1:2, cols] * u_ref[half, halo - 1:halo - 1 + tm, :]
                + cw_ref[2:3, cols] * u_ref[half, halo:halo + tm, :]) + cb_ref[:, cols]

    act = (jax.nn.gelu(conv(0), approximate=True) * conv(1)).astype(BF16)
    o_ref[0] = x_ref[0] + _rms(_dot(act, wd_ref[...]), gpost_ref[...])


def _ffn_call(x, p, gpre, gpost, tm):
    b, s, d = x.shape
    halo = FFN_HALO
    blocks_per_tile = tm // halo

    def full(a):
        return pl.BlockSpec(a.shape, lambda bi, ti: (0,) * a.ndim)

    in_specs = [
        pl.BlockSpec((1, tm, d), lambda bi, ti: (bi, ti, 0)),
        pl.BlockSpec((1, halo, d), lambda bi, ti: (bi, jnp.maximum(ti * blocks_per_tile - 1, 0), 0)),
        full(gpre), full(p["w_up"]), full(p["conv_w"]), full(p["conv_b"]), full(p["w_down"]), full(gpost),
    ]
    return pl.pallas_call(
        functools.partial(_ffn_kernel, tm=tm),
        grid=(b, s // tm),
        in_specs=in_specs,
        out_specs=pl.BlockSpec((1, tm, d), lambda bi, ti: (bi, ti, 0)),
        out_shape=jax.ShapeDtypeStruct(x.shape, F32),
        scratch_shapes=[pltpu.VMEM((halo + tm, d), BF16), pltpu.VMEM((2, halo + tm, D_FF), F32)],
        compiler_params=pltpu.CompilerParams(dimension_semantics=("arbitrary",) * 2,
                                             vmem_limit_bytes=VMEM_LIMIT),
        name="ffn",
    )(x, x, gpre, p["w_up"], p["conv_w"], p["conv_b"], p["w_down"], gpost)


def _pad_cols(w, left, total):
    return jnp.pad(w, ((0, 0), (left, total - left - w.shape[1])))


def _prep_layer(w_in, b_f, q_norm, kv_norm, w_uq, w_ukv, w_o, w_up, conv_w, conv_b, w_down):
    offs = [0]
    for sz in PROJ_SIZES:
        offs.append(offs[-1] + sz)
    a_q, a_k, a_v, c_q, c_kv, k_r, f_q, f_k, f_v, f_g = [w_in[:, offs[i]:offs[i + 1]] for i in range(10)]
    half = MLA_ROPE_DIM // 2

    def rot(w):
        return jnp.concatenate([-w[:, half:], w[:, :half]], axis=1)

    w_tok = jnp.concatenate([
        a_k, f_k, c_q, c_kv,
        _pad_cols(k_r, FEAT0, LANES), _pad_cols(rot(k_r), FEAT0, LANES), _pad_cols(f_g, 0, LANES)], axis=1)
    w_t = jnp.concatenate([a_q, f_q, a_v, f_v], axis=1).T
    uq = w_uq.reshape(MLA_Q_RANK, N_HEADS_MLA, MLA_QK_DIM)
    uq_main = jnp.pad(uq, ((0, 0), (0, 0), (0, LANES - MLA_QK_DIM)))
    uq_rope = uq[:, :, MLA_NOPE_DIM:]
    uq_rot = jnp.concatenate([-uq_rope[:, :, half:], uq_rope[:, :, :half]], axis=2)
    uq_rot = jnp.pad(uq_rot, ((0, 0), (0, 0), (MLA_NOPE_DIM, LANES - MLA_QK_DIM)))
    ukv = w_ukv.reshape(MLA_KV_RANK, N_HEADS_MLA, MLA_NOPE_DIM + MLA_V_DIM)
    ukv_k = jnp.pad(ukv[:, :, :MLA_NOPE_DIM], ((0, 0), (0, 0), (0, LANES - MLA_NOPE_DIM)))
    ukv_v = ukv[:, :, MLA_NOPE_DIM:]
    return {
        "w_tok": w_tok.astype(BF16),
        "w_t": w_t.astype(BF16),
        "b_f": _pad_cols(b_f[None, :], 0, LANES),
        "q_norm": q_norm[None, :],
        "kv_norm": kv_norm[None, :],
        "wuq_t": jnp.concatenate([uq_main.reshape(MLA_Q_RANK, N_HEADS_MLA * LANES),
                                  uq_rot.reshape(MLA_Q_RANK, N_HEADS_MLA * LANES)], axis=1).T.astype(BF16),
        "wukv_k": ukv_k.reshape(MLA_KV_RANK, N_HEADS_MLA * LANES).astype(BF16),
        "wukv_v_t": ukv_v.reshape(MLA_KV_RANK, W_MLA).T.astype(BF16),
        "w_o": w_o.astype(BF16),
        "w_up": w_up.astype(BF16),
        "conv_w": conv_w,
        "conv_b": conv_b[None, :],
        "w_down": w_down.astype(BF16),
    }


def _rope_tables(s):
    half = MLA_ROPE_DIM // 2
    inv = ROPE_THETA ** (-jnp.arange(half, dtype=F32) / half)
    ang = jnp.arange(s).astype(F32)[:, None] * inv[None, :]
    cos = jnp.concatenate([jnp.cos(ang)] * 2, axis=1)
    sin = jnp.concatenate([jnp.sin(ang)] * 2, axis=1)
    ctok = _pad_cols(cos, FEAT0, LANES)
    stok = _pad_cols(sin, FEAT0, LANES)
    ones = jnp.ones((s, MLA_NOPE_DIM), F32)
    ct = jnp.concatenate([ones, cos, jnp.zeros((s, LANES - MLA_QK_DIM), F32)], axis=1).T
    st = stok.T
    return ctok, stok, ct, st


def _t5_bucket_tiles(tile):
    d = jnp.arange(N_NEAR)[:, None, None]
    j = jnp.arange(tile)[None, :, None]
    i = jnp.arange(tile)[None, None, :]
    n = jnp.maximum(d * tile + i - j, 0)
    exact = T5_BUCKETS // 2
    large = exact + (jnp.log(jnp.maximum(n, 1).astype(F32) / exact)
                     / math.log(T5_MAX_DIST / exact) * (T5_BUCKETS - exact)).astype(jnp.int32)
    return jnp.where(n < exact, n, jnp.minimum(large, T5_BUCKETS - 1)).astype(jnp.int32)


def kernel(x, rel_bias, ln_mix_pre, ln_mix_post, ln_ffn_pre, ln_ffn_post, w_in, b_f, q_norm, kv_norm, w_uq,
           w_ukv, w_o, w_up, conv_w, conv_b, w_down):
    b, s, d = x.shape
    depth = w_in.shape[0]
    assert s % PROJ_ROWS == 0 and s % FFN_ROWS == 0 and s % OPROJ_ROWS == 0
    assert s // MOBA_BLOCK <= NBLK_PAD
    tables = _rope_tables(s)
    bidx = _t5_bucket_tiles(MOBA_BLOCK)
    far = rel_bias[T5_BUCKETS - 1, :]
    for l in range(depth):
        p = _prep_layer(w_in[l], b_f[l], q_norm[l], kv_norm[l], w_uq[l], w_ukv[l], w_o[l], w_up[l],
                        conv_w[l], conv_b[l], w_down[l])
        qat, ka, vat, qft, kf, vft, qmt, km, vmt = _proj_call(x, ln_mix_pre[l][None, :], p, tables, far, PROJ_ROWS)
        oa = _attn_call(qat, ka, vat, N_HEADS_MOBA // 2, (bidx, rel_bias))
        ob = _attn_call(qmt, km, vmt, N_HEADS_MLA // 2)
        oc = _attn_call(qft, kf, vft, N_HEADS_FOX // 2)
        x = _oproj_call(x, oa, ob, oc, p, ln_mix_post[l][None, :], OPROJ_ROWS)
        x = _ffn_call(x, p, ln_ffn_pre[l][None, :], ln_ffn_post[l][None, :], FFN_ROWS)
    return x
```

```python
import functools
import math

import jax
import jax.numpy as jnp
from jax import lax
from jax.experimental import pallas as pl
from jax.experimental.pallas import tpu as pltpu

F32 = jnp.float32
BF16 = jnp.bfloat16

HEAD_DIM = 64
N_HEADS_MOBA = 6
N_HEADS_MLA = 4
N_HEADS_FOX = 6
MOBA_BLOCK = 256
MOBA_TOPK = 3
MLA_Q_RANK = 256
MLA_KV_RANK = 128
MLA_NOPE_DIM = 64
MLA_ROPE_DIM = 32
MLA_V_DIM = 64
MLA_QK_DIM = MLA_NOPE_DIM + MLA_ROPE_DIM
ROPE_THETA = 10000.0
T5_BUCKETS = 32
T5_MAX_DIST = 1024
D_FF = 2816
CONV_WIDTH = 3
NORM_EPS = 1e-6
NEG_INF = -1e30
W_MOBA = N_HEADS_MOBA * HEAD_DIM
W_MLA = N_HEADS_MLA * MLA_V_DIM
W_FOX = N_HEADS_FOX * HEAD_DIM
PROJ_SIZES = (W_MOBA, W_MOBA, W_MOBA, MLA_Q_RANK, MLA_KV_RANK, MLA_ROPE_DIM, W_FOX, W_FOX, W_FOX, N_HEADS_FOX)

LANES = 128
SUBLANES = 8
VMEM_LIMIT = 56 * 1024 * 1024

ATT_TILE = 512
PROJ_ROWS = ATT_TILE
ATT_GROUP = 8
OPROJ_ROWS = 1024
FFN_ROWS = 512
FFN_HALO = 16
FEAT0 = HEAD_DIM
V_ROWS = HEAD_DIM + 16
LOG2E = math.log2(math.e)
NBLK_PAD = 16
N_NEAR = (T5_MAX_DIST + MOBA_BLOCK - 1 + MOBA_BLOCK - 1) // MOBA_BLOCK

def _ranges(sizes):
    out, start = [], 0
    for size in sizes:
        out.append((start, start + size))
        start += size
    return out


_TOK_AK, _TOK_FK, _TOK_CQ, _TOK_CKV, _TOK_KR, _TOK_KRR, _TOK_FG = _ranges(
    (W_MOBA, W_FOX, MLA_Q_RANK, MLA_KV_RANK, LANES, LANES, LANES))
_T_AQ, _T_FQ, _T_AV, _T_FV = _ranges((W_MOBA, W_FOX, W_MOBA, W_FOX))


def _rms(xf, g):
    return xf * lax.rsqrt(jnp.mean(xf * xf, axis=-1, keepdims=True) + NORM_EPS) * g


def _dot(a, b):
    return jnp.dot(a, b, preferred_element_type=F32)


def _dot_nt(a, b):
    return lax.dot_general(a, b, (((1,), (1,)), ((), ())), preferred_element_type=F32)


def _bf16_pieces(x):
    hi = x.astype(BF16).astype(F32)
    r = x - hi
    mid = r.astype(BF16).astype(F32)
    lo = (r - mid).astype(BF16).astype(F32)
    return hi, mid, lo


def _dot_3pass(a, b):
    a_hi = a.astype(BF16)
    a_lo = (a - a_hi.astype(F32)).astype(BF16)
    b_hi = b.astype(BF16)
    b_lo = (b - b_hi.astype(F32)).astype(BF16)
    return _dot(jnp.concatenate([a_hi, a_hi, a_lo], axis=1), jnp.concatenate([b_hi, b_lo, b_hi], axis=0))


def _head_slot(arr, hd):
    pair = arr[:, (hd // 2) * LANES:(hd // 2 + 1) * LANES]
    if hd % 2:
        pair = pltpu.roll(pair, HEAD_DIM, axis=1)
    return pair


def _proj_kernel(x_ref, g_ref, wtok_ref, wt_ref, bf_ref, qn_ref, kvn_ref, wuq_ref,
                 wukvk_ref, wukvv_ref, ctok_ref, stok_ref, ct_ref, st_ref, far_ref,
                 qat_ref, ka_ref, vat_ref, qft_ref, kf_ref, vft_ref, qmt_ref, km_ref, vmt_ref,
                 kmean_ref, fcarry_ref, *, tm):
    t = pl.program_id(1)
    n_sub = tm // MOBA_BLOCK

    @pl.when(t == 0)
    def _():
        kmean_ref[...] = jnp.zeros_like(kmean_ref)
        fcarry_ref[...] = jnp.zeros_like(fcarry_ref)

    h = _rms(x_ref[0], g_ref[...]).astype(BF16)
    lane = lax.broadcasted_iota(jnp.int32, (tm, LANES), 1)
    row = lax.broadcasted_iota(jnp.int32, (tm, LANES), 0)
    blk0 = t * n_sub
    is_head = lane < HEAD_DIM

    tok_all = _dot(h, wtok_ref[...])
    t_all = _dot_nt(wt_ref[...], h)

    def tok(rng):
        return tok_all[:, rng[0]:rng[1]]

    def tr(rng):
        return t_all[rng[0]:rng[1], :]

    sub8v = lax.broadcasted_iota(jnp.int32, (SUBLANES, tm), 0)
    ones_row = jnp.where(sub8v == 0, 1.0, 0.0)
    v_pad = jnp.zeros((V_ROWS - HEAD_DIM - SUBLANES, tm), F32)

    def store_vt(ref, vt):
        n_heads = vt.shape[0] // HEAD_DIM
        ref[0, 0] = jnp.concatenate(
            [piece for hd in range(n_heads)
             for piece in (vt[hd * HEAD_DIM:(hd + 1) * HEAD_DIM, :], ones_row, v_pad)], axis=0).astype(BF16)

    ak = tok(_TOK_AK)
    blk_row = blk0 + lax.shift_right_logical(row, int(math.log2(MOBA_BLOCK)))
    k_feat = jnp.where(lane == FEAT0 + blk_row, 1.0, 0.0)
    for hd in range(N_HEADS_MOBA):
        ka_ref[0, :, hd * LANES:(hd + 1) * LANES] = jnp.where(is_head, _head_slot(ak, hd), k_feat).astype(BF16)

    lane_k = lax.broadcasted_iota(jnp.int32, (1, W_MOBA), 1)
    for bi in range(n_sub):
        mean_row = jnp.mean(ak[bi * MOBA_BLOCK:(bi + 1) * MOBA_BLOCK, :], axis=0, keepdims=True)
        for hd in range(N_HEADS_MOBA):
            in_head = (lane_k >= hd * HEAD_DIM) & (lane_k < (hd + 1) * HEAD_DIM)
            kmean_ref[pl.ds(hd * NBLK_PAD + blk0 + bi, 1), :] = jnp.where(in_head, mean_row, 0.0)

    qat = tr(_T_AQ) * (HEAD_DIM ** -0.5 * LOG2E)
    gate = _dot_3pass(kmean_ref[...], qat)
    own = blk0 + lax.shift_right_logical(
        lax.broadcasted_iota(jnp.int32, (NBLK_PAD, tm), 1), int(math.log2(MOBA_BLOCK)))
    n_io = lax.broadcasted_iota(jnp.int32, (NBLK_PAD, tm), 0)
    past = n_io < own
    zeros_tail = jnp.zeros((LANES - HEAD_DIM - NBLK_PAD, tm), F32)
    for hd in range(N_HEADS_MOBA):
        g = jnp.where(past, gate[hd * NBLK_PAD:(hd + 1) * NBLK_PAD, :], NEG_INF)
        rank = jnp.zeros((NBLK_PAD, tm), jnp.int32)
        for n2 in range(NBLK_PAD):
            r = g[n2:n2 + 1, :]
            tie = jnp.where(n2 < n_io, 1, 0)
            rank = rank + jnp.where(r > g, 1, jnp.where(r == g, tie, 0))
        sel = jnp.where(past, rank, MOBA_TOPK) < MOBA_TOPK
        far = jnp.where(own - n_io >= N_NEAR, far_ref[hd] * LOG2E, 0.0)
        q_feat = jnp.where(sel | (n_io == own), far, NEG_INF)
        qat_ref[0, 0, hd * LANES:(hd + 1) * LANES, :] = jnp.concatenate(
            [qat[hd * HEAD_DIM:(hd + 1) * HEAD_DIM, :], q_feat, zeros_tail], axis=0).astype(BF16)
    store_vt(vat_ref, tr(_T_AV))

    fg = tok(_TOK_FG) + bf_ref[...]
    logf = jnp.minimum(fg, 0.0) - jnp.log1p(jnp.exp(-jnp.abs(fg)))
    csum = jnp.where(lane < N_HEADS_FOX, logf, 0.0)
    sft = 1
    while sft < tm:
        csum = csum + jnp.where(row >= sft, pltpu.roll(csum, sft, axis=0), 0.0)
        sft *= 2
    decay = csum + fcarry_ref[0:1, :]
    fcarry_ref[0:1, :] = decay[tm - 1:tm, :]
    decay = decay * LOG2E
    decay_t = decay.T

    fk = tok(_TOK_FK)
    fqt = tr(_T_FQ) * (HEAD_DIM ** -0.5 * LOG2E)
    sub8 = lax.broadcasted_iota(jnp.int32, (SUBLANES, tm), 0)
    zeros_tail_f = jnp.zeros((LANES - HEAD_DIM - SUBLANES, tm), F32)
    for hd in range(N_HEADS_FOX):
        fcol = jnp.broadcast_to(decay[:, hd:hd + 1], (tm, LANES))
        hi, mid, lo = _bf16_pieces(fcol)
        k_feat_f = jnp.where(lane < FEAT0 + 3, 1.0,
                             jnp.where(lane == FEAT0 + 3, -hi,
                                       jnp.where(lane == FEAT0 + 4, -mid,
                                                 jnp.where(lane == FEAT0 + 5, -lo, 0.0))))
        kf_ref[0, :, hd * LANES:(hd + 1) * LANES] = jnp.where(is_head, _head_slot(fk, hd), k_feat_f).astype(BF16)
        hi, mid, lo = _bf16_pieces(decay_t[hd:hd + 1, :])
        q_feat_f = jnp.where(sub8 == 0, hi,
                             jnp.where(sub8 == 1, mid,
                                       jnp.where(sub8 == 2, lo,
                                                 jnp.where(sub8 < 6, 1.0, 0.0))))
        qft_ref[0, 0, hd * LANES:(hd + 1) * LANES, :] = jnp.concatenate(
            [fqt[hd * HEAD_DIM:(hd + 1) * HEAD_DIM, :], q_feat_f, zeros_tail_f], axis=0).astype(BF16)
    store_vt(vft_ref, tr(_T_FV))

    cqn = _rms(tok(_TOK_CQ), qn_ref[...]).astype(BF16)
    qm_all = _dot_nt(wuq_ref[...], cqn)
    qm = qm_all[0:N_HEADS_MLA * LANES, :]
    qmr = qm_all[N_HEADS_MLA * LANES:, :]
    kvn = _rms(tok(_TOK_CKV), kvn_ref[...]).astype(BF16)
    k_nope = _dot(kvn, wukvk_ref[...])
    k_rope = tok(_TOK_KR) * ctok_ref[...] + tok(_TOK_KRR) * stok_ref[...]
    cos_t = ct_ref[...]
    sin_t = st_ref[...]
    for hd in range(N_HEADS_MLA):
        sl = slice(hd * LANES, (hd + 1) * LANES)
        qmt_ref[0, 0, sl, :] = ((qm[sl, :] * cos_t + qmr[sl, :] * sin_t) * (MLA_QK_DIM ** -0.5 * LOG2E)).astype(BF16)
        km_ref[0, :, sl] = (k_nope[:, sl] + k_rope).astype(BF16)
    store_vt(vmt_ref, _dot_nt(wukvv_ref[...], kvn))


def _proj_call(x, g, p, tables, far, tm):
    b, s, d = x.shape
    nt = s // tm

    def full(a):
        return pl.BlockSpec(a.shape, lambda bi, ti: (0,) * a.ndim)

    ctok, stok, ct, st = tables
    in_specs = [
        pl.BlockSpec((1, tm, d), lambda bi, ti: (bi, ti, 0)),
        full(g), full(p["w_tok"]), full(p["w_t"]), full(p["b_f"]), full(p["q_norm"]), full(p["kv_norm"]),
        full(p["wuq_t"]), full(p["wukv_k"]), full(p["wukv_v_t"]),
        pl.BlockSpec((tm, LANES), lambda bi, ti: (ti, 0)),
        pl.BlockSpec((tm, LANES), lambda bi, ti: (ti, 0)),
        pl.BlockSpec((LANES, tm), lambda bi, ti: (0, ti)),
        pl.BlockSpec((LANES, tm), lambda bi, ti: (0, ti)),
        pl.BlockSpec(memory_space=pltpu.SMEM),
    ]

    def qt_spec(n_heads):
        return pl.BlockSpec((1, 1, n_heads * LANES, tm), lambda bi, ti: (bi, ti, 0, 0))

    def k_spec(n_heads):
        return pl.BlockSpec((1, tm, n_heads * LANES), lambda bi, ti: (bi, ti, 0))

    def vt_spec(width):
        return pl.BlockSpec((1, 1, width, tm), lambda bi, ti: (bi, ti, 0, 0))

    def qt_shape(n_heads):
        return jax.ShapeDtypeStruct((b, nt, n_heads * LANES, tm), BF16)

    def k_shape(n_heads):
        return jax.ShapeDtypeStruct((b, s, n_heads * LANES), BF16)

    def vt_shape(width):
        return jax.ShapeDtypeStruct((b, nt, width, tm), BF16)

    heads = (N_HEADS_MOBA, N_HEADS_FOX, N_HEADS_MLA)
    out_specs = [spec for n in heads for spec in (qt_spec(n), k_spec(n), vt_spec(n * V_ROWS))]
    out_shape = [shp for n in heads for shp in (qt_shape(n), k_shape(n), vt_shape(n * V_ROWS))]
    return pl.pallas_call(
        functools.partial(_proj_kernel, tm=tm),
        grid=(b, nt),
        in_specs=in_specs,
        out_specs=out_specs,
        out_shape=out_shape,
        scratch_shapes=[pltpu.VMEM((N_HEADS_MOBA * NBLK_PAD, W_MOBA), F32),
                        pltpu.VMEM((SUBLANES, LANES), F32)],
        compiler_params=pltpu.CompilerParams(dimension_semantics=("arbitrary", "arbitrary"),
                                             vmem_limit_bytes=VMEM_LIMIT),
        name="proj",
    )(x, g, p["w_tok"], p["w_t"], p["b_f"], p["q_norm"], p["kv_norm"], p["wuq_t"],
      p["wukv_k"], p["wukv_v_t"], ctok, stok, ct, st, far)


def _attn_kernel(*refs, has_bias, tq, tk, nq, n_add, n_plain):
    if has_bias:
        (q_ref, j_ref, qt_ref, k_ref, vt_ref, bidx_ref, tab_ref, o_ref,
         acc_ref, m_ref, s0_ref, s1_ref, s2_ref, s3_ref, bias_ref) = refs
    else:
        (q_ref, j_ref, qt_ref, k_ref, vt_ref, o_ref,
         acc_ref, m_ref, s0_ref, s1_ref, s2_ref, s3_ref, bias_ref) = refs
    hp = pl.program_id(0)
    bi = pl.program_id(1)
    blk = MOBA_BLOCK
    n_near = N_NEAR if has_bias else 1

    @pl.when(bi == 0)
    def _():
        tri = (lax.broadcasted_iota(jnp.int32, (blk, blk), 0) <= lax.broadcasted_iota(jnp.int32, (blk, blk), 1))
        for hh in range(2):
            bias_ref[hh, 0] = jnp.full((blk, blk), NEG_INF, F32)
            bias_ref[hh, n_near + 1] = jnp.zeros((blk, blk), F32)
            if has_bias:
                def build(d, carry, hh=hh):
                    idx = bidx_ref[d]
                    bias_ref[hh, d + 1] = lax.fori_loop(
                        0, T5_BUCKETS,
                        lambda bk, tl: jnp.where(idx == bk, tab_ref[bk, hp * 2 + hh] * LOG2E, tl),
                        jnp.zeros((blk, blk), F32))
                    return carry
                lax.fori_loop(0, n_near, build, 0)
                bias_ref[hh, 1] = jnp.where(tri, bias_ref[hh, 1], NEG_INF)
            else:
                bias_ref[hh, 1] = jnp.where(tri, 0.0, NEG_INF)

    acc_ref[...] = jnp.zeros_like(acc_ref)
    m_ref[...] = jnp.full(m_ref.shape, NEG_INF, F32)
    bufs = (s0_ref, s1_ref, s2_ref, s3_ref)

    n_items = n_add + n_plain

    def scores(e, s_ref, additive):
        qi = q_ref[e]
        j = j_ref[e]
        start = pl.multiple_of(j * tk, tk)
        tmax = []
        for hh in range(2):
            k = k_ref[0, pl.ds(start, tk), hh * LANES:(hh + 1) * LANES]
            s = _dot(k, qt_ref[0, qi, hh * LANES:(hh + 1) * LANES, :])
            if additive:
                s = s + jnp.concatenate([
                    jnp.concatenate([
                        bias_ref[hh, jnp.clip((qi * (tq // blk) + c) - (j * (tk // blk) + r), -1, n_near) + 1]
                        for c in range(tq // blk)], axis=1)
                    for r in range(tk // blk)], axis=0)
            s_ref[hh] = s
            tmax.append(jnp.max(s, axis=0, keepdims=True))
        return tuple(tmax)

    def update(e, s_ref, tmax):
        qi = q_ref[e]
        j = j_ref[e]
        for hh in range(2):
            m_old = m_ref[qi, hh, 0:1, :]
            m_new = jnp.maximum(m_old, tmax[hh])
            alpha = jnp.exp2(m_old - m_new)
            p = jnp.exp2(s_ref[hh] - m_new).astype(BF16)
            v = vt_ref[0, j, hh * V_ROWS:(hh + 1) * V_ROWS, :]
            acc_ref[qi, hh] = alpha * acc_ref[qi, hh] + _dot(v, p)
            m_ref[qi, hh, 0:1, :] = m_new

    def step(e, r, carry, additive):
        tmax_cur, tmax_nxt = carry
        tmax_new = scores(e + 2, bufs[(r + 2) % 4], additive)
        update(e, bufs[r], tmax_cur)
        return tmax_nxt, tmax_new

    def steps(lo, hi, carry, additive):
        n_groups = max(hi - lo, 0) // ATT_GROUP

        def group(g, carry):
            for r in range(ATT_GROUP):
                carry = step(lo + ATT_GROUP * g + r, (lo + r) % 4, carry, additive)
            return carry

        carry = lax.fori_loop(0, n_groups, group, carry)
        for e in range(lo + ATT_GROUP * n_groups, hi):
            carry = step(e, e % 4, carry, additive)
        return carry

    carry = (scores(0, bufs[0], 0 < n_add), scores(1, bufs[1], 1 < n_add))
    split = max(n_add - 2, 0)
    carry = steps(0, split, carry, True)
    carry = steps(split, n_items - 2, carry, False)
    update(n_items - 2, bufs[(n_items - 2) % 4], carry[0])
    update(n_items - 1, bufs[(n_items - 1) % 4], carry[1])

    def finish(qi, carry):
        o = [acc_ref[qi, hh, 0:HEAD_DIM, :] / acc_ref[qi, hh, HEAD_DIM:HEAD_DIM + 1, :] for hh in range(2)]
        o_ref[0, pl.ds(pl.multiple_of(qi * tq, tq), tq), :] = jnp.concatenate(o, axis=0).T.astype(BF16)
        return carry

    lax.fori_loop(0, nq, finish, 0)


def _attn_schedule(nq, tq, tk, n_near):
    blk = MOBA_BLOCK
    add, plain = [], []
    for q in range(nq):
        for j in range(q, -1, -1):
            min_dist = q * (tq // blk) - (j * (tk // blk) + tk // blk - 1)
            (add if min_dist < n_near else plain).append((q, j))
    items = add + plain
    return (jnp.asarray([q for q, _ in items], jnp.int32), jnp.asarray([j for _, j in items], jnp.int32),
            len(add), len(plain))


def _attn_call(qt, k, vt, n_pairs, bias_inputs=None):
    b, nq, _, tq = qt.shape
    _, s, _ = k.shape
    tk = vt.shape[-1]
    assert tq == tk, "the causal-triangle tile assumes equal query and key tiles"
    has_bias = bias_inputs is not None
    q_tab, j_tab, n_add, n_plain = _attn_schedule(nq, tq, tk, N_NEAR if has_bias else 1)
    assert n_add + n_plain >= 2
    smem = pl.BlockSpec(memory_space=pltpu.SMEM)
    in_specs = [
        smem, smem,
        pl.BlockSpec((1, nq, 2 * LANES, tq), lambda p, bi: (bi, 0, p, 0)),
        pl.BlockSpec((1, s, 2 * LANES), lambda p, bi: (bi, 0, p)),
        pl.BlockSpec((1, s // tk, 2 * V_ROWS, tk), lambda p, bi: (bi, 0, p, 0)),
    ]
    args = [q_tab, j_tab, qt, k, vt]
    n_tiles = (N_NEAR if has_bias else 1) + 2
    scratch = [pltpu.VMEM((nq, 2, V_ROWS, tq), F32), pltpu.VMEM((nq, 2, SUBLANES, tq), F32),
               pltpu.VMEM((2, tk, tq), F32), pltpu.VMEM((2, tk, tq), F32),
               pltpu.VMEM((2, tk, tq), F32), pltpu.VMEM((2, tk, tq), F32),
               pltpu.VMEM((2, n_tiles, MOBA_BLOCK, MOBA_BLOCK), F32)]
    if has_bias:
        bidx, tab = bias_inputs
        in_specs += [pl.BlockSpec(bidx.shape, lambda p, bi: (0, 0, 0)), smem]
        args += [bidx, tab]
    return pl.pallas_call(
        functools.partial(_attn_kernel, has_bias=has_bias, tq=tq, tk=tk, nq=nq, n_add=n_add, n_plain=n_plain),
        grid=(n_pairs, b),
        in_specs=in_specs,
        out_specs=pl.BlockSpec((1, s, 2 * HEAD_DIM), lambda p, bi: (bi, 0, p)),
        out_shape=jax.ShapeDtypeStruct((b, s, n_pairs * 2 * HEAD_DIM), BF16),
        scratch_shapes=scratch,
        compiler_params=pltpu.CompilerParams(dimension_semantics=("arbitrary",) * 2,
                                             vmem_limit_bytes=VMEM_LIMIT),
        name="attn_bias" if has_bias else "attn",
    )(*args)


def _oproj_kernel(x_ref, oa_ref, ob_ref, oc_ref, wo_ref, g_ref, o_ref):
    merged = jnp.concatenate([oa_ref[0], ob_ref[0], oc_ref[0]], axis=1)
    o_ref[0] = x_ref[0] + _rms(_dot(merged, wo_ref[...]), g_ref[...])


def _oproj_call(x, oa, ob, oc, p, g, tm):
    b, s, d = x.shape

    def rows(width):
        return pl.BlockSpec((1, tm, width), lambda bi, ti: (bi, ti, 0))

    def full(a):
        return pl.BlockSpec(a.shape, lambda bi, ti: (0,) * a.ndim)

    return pl.pallas_call(
        _oproj_kernel,
        grid=(b, s // tm),
        in_specs=[rows(d), rows(W_MOBA), rows(W_MLA), rows(W_FOX),
                  full(p["w_o"]), full(g)],
        out_specs=rows(d),
        out_shape=jax.ShapeDtypeStruct(x.shape, F32),
        compiler_params=pltpu.CompilerParams(dimension_semantics=("arbitrary", "arbitrary"),
                                             vmem_limit_bytes=VMEM_LIMIT),
        name="oproj",
    )(x, oa, ob, oc, p["w_o"], g)


def _ffn_kernel(x_ref, xp_ref, gpre_ref, wup_ref, cw_ref, cb_ref, wd_ref, gpost_ref, o_ref, h_ref, u_ref, *, tm):
    ti = pl.program_id(1)
    halo = FFN_HALO

    h_ref[halo:, :] = _rms(x_ref[0], gpre_ref[...]).astype(BF16)
    prev = _rms(xp_ref[0], gpre_ref[...])
    h_ref[0:halo, :] = jnp.where(ti == 0, 0.0, prev).astype(BF16)

    h = h_ref[...]
    for half in range(2):
        u_ref[half] = _dot(h, wup_ref[:, half * D_FF:(half + 1) * D_FF])

    def conv(half):
        cols = slice(half * D_FF, (half + 1) * D_FF)
        out = cb_ref[:, cols]
        for tap in range(CONV_WIDTH):
            first = halo - (CONV_WIDTH - 1) + tap
            out = out + cw_ref[tap:tap + 1, cols] * u_ref[half, first:first + tm, :]
        return out

    act = (jax.nn.gelu(conv(0), approximate=True) * conv(1)).astype(BF16)
    o_ref[0] = x_ref[0] + _rms(_dot(act, wd_ref[...]), gpost_ref[...])


def _ffn_call(x, p, gpre, gpost, tm):
    b, s, d = x.shape
    halo = FFN_HALO
    blocks_per_tile = tm // halo

    def full(a):
        return pl.BlockSpec(a.shape, lambda bi, ti: (0,) * a.ndim)

    in_specs = [
        pl.BlockSpec((1, tm, d), lambda bi, ti: (bi, ti, 0)),
        pl.BlockSpec((1, halo, d), lambda bi, ti: (bi, jnp.maximum(ti * blocks_per_tile - 1, 0), 0)),
        full(gpre), full(p["w_up"]), full(p["conv_w"]), full(p["conv_b"]), full(p["w_down"]), full(gpost),
    ]
    return pl.pallas_call(
        functools.partial(_ffn_kernel, tm=tm),
        grid=(b, s // tm),
        in_specs=in_specs,
        out_specs=pl.BlockSpec((1, tm, d), lambda bi, ti: (bi, ti, 0)),
        out_shape=jax.ShapeDtypeStruct(x.shape, F32),
        scratch_shapes=[pltpu.VMEM((halo + tm, d), BF16), pltpu.VMEM((2, halo + tm, D_FF), F32)],
        compiler_params=pltpu.CompilerParams(dimension_semantics=("arbitrary",) * 2,
                                             vmem_limit_bytes=VMEM_LIMIT),
        name="ffn",
    )(x, x, gpre, p["w_up"], p["conv_w"], p["conv_b"], p["w_down"], gpost)


def _pad_cols(w, left, total):
    return jnp.pad(w, ((0, 0), (left, total - left - w.shape[1])))


def _prep_layer(w_in, b_f, q_norm, kv_norm, w_uq, w_ukv, w_o, w_up, conv_w, conv_b, w_down):
    offs = [0]
    for sz in PROJ_SIZES:
        offs.append(offs[-1] + sz)
    a_q, a_k, a_v, c_q, c_kv, k_r, f_q, f_k, f_v, f_g = [w_in[:, offs[i]:offs[i + 1]] for i in range(10)]
    half = MLA_ROPE_DIM // 2

    def rot(w):
        return jnp.concatenate([-w[:, half:], w[:, :half]], axis=1)

    w_tok = jnp.concatenate([
        a_k, f_k, c_q, c_kv,
        _pad_cols(k_r, FEAT0, LANES), _pad_cols(rot(k_r), FEAT0, LANES), _pad_cols(f_g, 0, LANES)], axis=1)
    w_t = jnp.concatenate([a_q, f_q, a_v, f_v], axis=1).T
    uq = w_uq.reshape(MLA_Q_RANK, N_HEADS_MLA, MLA_QK_DIM)
    uq_main = jnp.pad(uq, ((0, 0), (0, 0), (0, LANES - MLA_QK_DIM)))
    uq_rope = uq[:, :, MLA_NOPE_DIM:]
    uq_rot = jnp.concatenate([-uq_rope[:, :, half:], uq_rope[:, :, :half]], axis=2)
    uq_rot = jnp.pad(uq_rot, ((0, 0), (0, 0), (MLA_NOPE_DIM, LANES - MLA_QK_DIM)))
    ukv = w_ukv.reshape(MLA_KV_RANK, N_HEADS_MLA, MLA_NOPE_DIM + MLA_V_DIM)
    ukv_k = jnp.pad(ukv[:, :, :MLA_NOPE_DIM], ((0, 0), (0, 0), (0, LANES - MLA_NOPE_DIM)))
    ukv_v = ukv[:, :, MLA_NOPE_DIM:]
    return {
        "w_tok": w_tok.astype(BF16),
        "w_t": w_t.astype(BF16),
        "b_f": _pad_cols(b_f[None, :], 0, LANES),
        "q_norm": q_norm[None, :],
        "kv_norm": kv_norm[None, :],
        "wuq_t": jnp.concatenate([uq_main.reshape(MLA_Q_RANK, N_HEADS_MLA * LANES),
                                  uq_rot.reshape(MLA_Q_RANK, N_HEADS_MLA * LANES)], axis=1).T.astype(BF16),
        "wukv_k": ukv_k.reshape(MLA_KV_RANK, N_HEADS_MLA * LANES).astype(BF16),
        "wukv_v_t": ukv_v.reshape(MLA_KV_RANK, W_MLA).T.astype(BF16),
        "w_o": w_o.astype(BF16),
        "w_up": w_up.astype(BF16),
        "conv_w": conv_w,
        "conv_b": conv_b[None, :],
        "w_down": w_down.astype(BF16),
    }


def _rope_tables(s):
    half = MLA_ROPE_DIM // 2
    inv = ROPE_THETA ** (-jnp.arange(half, dtype=F32) / half)
    ang = jnp.arange(s).astype(F32)[:, None] * inv[None, :]
    cos = jnp.concatenate([jnp.cos(ang)] * 2, axis=1)
    sin = jnp.concatenate([jnp.sin(ang)] * 2, axis=1)
    ctok = _pad_cols(cos, FEAT0, LANES)
    stok = _pad_cols(sin, FEAT0, LANES)
    ones = jnp.ones((s, MLA_NOPE_DIM), F32)
    ct = jnp.concatenate([ones, cos, jnp.zeros((s, LANES - MLA_QK_DIM), F32)], axis=1).T
    st = stok.T
    return ctok, stok, ct, st


def _t5_bucket_tiles(tile):
    d = jnp.arange(N_NEAR)[:, None, None]
    j = jnp.arange(tile)[None, :, None]
    i = jnp.arange(tile)[None, None, :]
    n = jnp.maximum(d * tile + i - j, 0)
    exact = T5_BUCKETS // 2
    large = exact + (jnp.log(jnp.maximum(n, 1).astype(F32) / exact)
                     / math.log(T5_MAX_DIST / exact) * (T5_BUCKETS - exact)).astype(jnp.int32)
    return jnp.where(n < exact, n, jnp.minimum(large, T5_BUCKETS - 1)).astype(jnp.int32)


def kernel(x, rel_bias, ln_mix_pre, ln_mix_post, ln_ffn_pre, ln_ffn_post, w_in, b_f, q_norm, kv_norm, w_uq,
           w_ukv, w_o, w_up, conv_w, conv_b, w_down):
    b, s, d = x.shape
    depth = w_in.shape[0]
    assert s % PROJ_ROWS == 0 and s % FFN_ROWS == 0 and s % OPROJ_ROWS == 0
    assert s // MOBA_BLOCK <= NBLK_PAD
    tables = _rope_tables(s)
    bidx = _t5_bucket_tiles(MOBA_BLOCK)
    far = rel_bias[T5_BUCKETS - 1, :]
    for l in range(depth):
        p = _prep_layer(w_in[l], b_f[l], q_norm[l], kv_norm[l], w_uq[l], w_ukv[l], w_o[l], w_up[l],
                        conv_w[l], conv_b[l], w_down[l])
        qat, ka, vat, qft, kf, vft, qmt, km, vmt = _proj_call(x, ln_mix_pre[l][None, :], p, tables, far, PROJ_ROWS)
        oa = _attn_call(qat, ka, vat, N_HEADS_MOBA // 2, (bidx, rel_bias))
        ob = _attn_call(qmt, km, vmt, N_HEADS_MLA // 2)
        oc = _attn_call(qft, kf, vft, N_HEADS_FOX // 2)
        x = _oproj_call(x, oa, ob, oc, p, ln_mix_post[l][None, :], OPROJ_ROWS)
        x = _ffn_call(x, p, ln_ffn_pre[l][None, :], ln_ffn_post[l][None, :], FFN_ROWS)
    return x
```

```python
import functools
import math

import jax
import jax.numpy as jnp
from jax import lax
from jax.experimental import pallas as pl
from jax.experimental.pallas import tpu as pltpu

F32 = jnp.float32
BF16 = jnp.bfloat16

HEAD_DIM = 64
N_HEADS_MOBA = 6
N_HEADS_MLA = 4
N_HEADS_FOX = 6
MOBA_BLOCK = 256
MOBA_TOPK = 3
MLA_Q_RANK = 256
MLA_KV_RANK = 128
MLA_NOPE_DIM = 64
MLA_ROPE_DIM = 32
MLA_V_DIM = 64
MLA_QK_DIM = MLA_NOPE_DIM + MLA_ROPE_DIM
ROPE_THETA = 10000.0
T5_BUCKETS = 32
T5_MAX_DIST = 1024
D_FF = 2816
CONV_WIDTH = 3
NORM_EPS = 1e-6
NEG_INF = -1e30
W_MOBA = N_HEADS_MOBA * HEAD_DIM
W_MLA = N_HEADS_MLA * MLA_V_DIM
W_FOX = N_HEADS_FOX * HEAD_DIM
PROJ_SIZES = (W_MOBA, W_MOBA, W_MOBA, MLA_Q_RANK, MLA_KV_RANK, MLA_ROPE_DIM, W_FOX, W_FOX, W_FOX, N_HEADS_FOX)

LANES = 128
SUBLANES = 8
BF16_ROWS = 16
VMEM_LIMIT = 56 * 1024 * 1024

ATT_TILE = 512
PROJ_ROWS = ATT_TILE
ATT_STAGES = 4
ATT_GROUP = 2 * ATT_STAGES
OPROJ_ROWS = 1024
FFN_ROWS = 512
FFN_HALO = BF16_ROWS
FEAT0 = HEAD_DIM
V_ROWS = HEAD_DIM + BF16_ROWS
LOG2E = math.log2(math.e)
NBLK_PAD = 16
N_NEAR = (T5_MAX_DIST + MOBA_BLOCK - 1 + MOBA_BLOCK - 1) // MOBA_BLOCK


def _ranges(sizes):
    out, start = [], 0
    for size in sizes:
        out.append((start, start + size))
        start += size
    return out


_TOK_AK, _TOK_FK, _TOK_CQ, _TOK_CKV, _TOK_KR, _TOK_KRR, _TOK_FG = _ranges(
    (W_MOBA, W_FOX, MLA_Q_RANK, MLA_KV_RANK, LANES, LANES, LANES))
_T_AQ, _T_FQ, _T_AV, _T_FV = _ranges((W_MOBA, W_FOX, W_MOBA, W_FOX))


def _rms(xf, g):
    return xf * lax.rsqrt(jnp.mean(xf * xf, axis=-1, keepdims=True) + NORM_EPS) * g


def _dot(a, b):
    return jnp.dot(a, b, preferred_element_type=F32)


def _dot_nt(a, b):
    return lax.dot_general(a, b, (((1,), (1,)), ((), ())), preferred_element_type=F32)


def _bf16_pieces(x):
    hi = x.astype(BF16).astype(F32)
    r = x - hi
    mid = r.astype(BF16).astype(F32)
    lo = (r - mid).astype(BF16).astype(F32)
    return hi, mid, lo


def _dot_3pass(a, b):
    a_hi = a.astype(BF16)
    a_lo = (a - a_hi.astype(F32)).astype(BF16)
    b_hi = b.astype(BF16)
    b_lo = (b - b_hi.astype(F32)).astype(BF16)
    return _dot(jnp.concatenate([a_hi, a_hi, a_lo], axis=1), jnp.concatenate([b_hi, b_lo, b_hi], axis=0))


def _head_slot(arr, hd):
    pair = arr[:, (hd // 2) * LANES:(hd // 2 + 1) * LANES]
    if hd % 2:
        pair = pltpu.roll(pair, HEAD_DIM, axis=1)
    return pair


def _proj_kernel(x_ref, g_ref, wtok_ref, wt_ref, bf_ref, qn_ref, kvn_ref, wuq_ref,
                 wukvk_ref, wukvv_ref, ctok_ref, stok_ref, ct_ref, st_ref, far_ref,
                 qat_ref, ka_ref, vat_ref, qft_ref, kf_ref, vft_ref, qmt_ref, km_ref, vmt_ref,
                 kmean_ref, fcarry_ref, *, tm):
    t = pl.program_id(1)
    n_sub = tm // MOBA_BLOCK

    @pl.when(t == 0)
    def _():
        kmean_ref[...] = jnp.zeros_like(kmean_ref)
        fcarry_ref[...] = jnp.zeros_like(fcarry_ref)

    h = _rms(x_ref[0], g_ref[...]).astype(BF16)
    lane = lax.broadcasted_iota(jnp.int32, (tm, LANES), 1)
    row = lax.broadcasted_iota(jnp.int32, (tm, LANES), 0)
    blk0 = t * n_sub
    is_head = lane < HEAD_DIM

    tok_all = _dot(h, wtok_ref[...])
    t_all = _dot_nt(wt_ref[...], h)

    def tok(rng):
        return tok_all[:, rng[0]:rng[1]]

    def tr(rng):
        return t_all[rng[0]:rng[1], :]

    sub8v = lax.broadcasted_iota(jnp.int32, (SUBLANES, tm), 0)
    ones_row = jnp.where(sub8v == 0, 1.0, 0.0)
    v_pad = jnp.zeros((V_ROWS - HEAD_DIM - SUBLANES, tm), F32)

    def store_vt(ref, vt):
        n_heads = vt.shape[0] // HEAD_DIM
        ref[0, 0] = jnp.concatenate(
            [piece for hd in range(n_heads)
             for piece in (vt[hd * HEAD_DIM:(hd + 1) * HEAD_DIM, :], ones_row, v_pad)], axis=0).astype(BF16)

    ak = tok(_TOK_AK)
    blk_row = blk0 + lax.shift_right_logical(row, int(math.log2(MOBA_BLOCK)))
    k_feat = jnp.where(lane == FEAT0 + blk_row, 1.0, 0.0)
    for hd in range(N_HEADS_MOBA):
        ka_ref[0, :, hd * LANES:(hd + 1) * LANES] = jnp.where(is_head, _head_slot(ak, hd), k_feat).astype(BF16)

    lane_k = lax.broadcasted_iota(jnp.int32, (1, W_MOBA), 1)
    for bi in range(n_sub):
        mean_row = jnp.mean(ak[bi * MOBA_BLOCK:(bi + 1) * MOBA_BLOCK, :], axis=0, keepdims=True)
        for hd in range(N_HEADS_MOBA):
            in_head = (lane_k >= hd * HEAD_DIM) & (lane_k < (hd + 1) * HEAD_DIM)
            kmean_ref[pl.ds(hd * NBLK_PAD + blk0 + bi, 1), :] = jnp.where(in_head, mean_row, 0.0)

    qat = tr(_T_AQ) * (HEAD_DIM ** -0.5 * LOG2E)
    gate = _dot_3pass(kmean_ref[...], qat)
    own = blk0 + lax.shift_right_logical(
        lax.broadcasted_iota(jnp.int32, (NBLK_PAD, tm), 1), int(math.log2(MOBA_BLOCK)))
    n_io = lax.broadcasted_iota(jnp.int32, (NBLK_PAD, tm), 0)
    past = n_io < own
    zeros_tail = jnp.zeros((LANES - HEAD_DIM - NBLK_PAD, tm), F32)
    for hd in range(N_HEADS_MOBA):
        g = jnp.where(past, gate[hd * NBLK_PAD:(hd + 1) * NBLK_PAD, :], NEG_INF)
        rank = jnp.zeros((NBLK_PAD, tm), jnp.int32)
        for n2 in range(NBLK_PAD):
            r = g[n2:n2 + 1, :]
            tie = jnp.where(n2 < n_io, 1, 0)
            rank = rank + jnp.where(r > g, 1, jnp.where(r == g, tie, 0))
        sel = jnp.where(past, rank, MOBA_TOPK) < MOBA_TOPK
        far = jnp.where(own - n_io >= N_NEAR, far_ref[hd] * LOG2E, 0.0)
        q_feat = jnp.where(sel | (n_io == own), far, NEG_INF)
        qat_ref[0, 0, hd * LANES:(hd + 1) * LANES, :] = jnp.concatenate(
            [qat[hd * HEAD_DIM:(hd + 1) * HEAD_DIM, :], q_feat, zeros_tail], axis=0).astype(BF16)
    store_vt(vat_ref, tr(_T_AV))

    fg = tok(_TOK_FG) + bf_ref[...]
    logf = jnp.minimum(fg, 0.0) - jnp.log1p(jnp.exp(-jnp.abs(fg)))
    csum = jnp.where(lane < N_HEADS_FOX, logf, 0.0)
    sft = 1
    while sft < tm:
        csum = csum + jnp.where(row >= sft, pltpu.roll(csum, sft, axis=0), 0.0)
        sft *= 2
    decay = csum + fcarry_ref[0:1, :]
    fcarry_ref[0:1, :] = decay[tm - 1:tm, :]
    decay = decay * LOG2E
    decay_t = decay.T

    fk = tok(_TOK_FK)
    fqt = tr(_T_FQ) * (HEAD_DIM ** -0.5 * LOG2E)
    sub8 = lax.broadcasted_iota(jnp.int32, (SUBLANES, tm), 0)
    zeros_tail_f = jnp.zeros((LANES - HEAD_DIM - SUBLANES, tm), F32)
    for hd in range(N_HEADS_FOX):
        fcol = jnp.broadcast_to(decay[:, hd:hd + 1], (tm, LANES))
        hi, mid, lo = _bf16_pieces(fcol)
        k_feat_f = jnp.where(lane < FEAT0 + 3, 1.0,
                             jnp.where(lane == FEAT0 + 3, -hi,
                                       jnp.where(lane == FEAT0 + 4, -mid,
                                                 jnp.where(lane == FEAT0 + 5, -lo, 0.0))))
        kf_ref[0, :, hd * LANES:(hd + 1) * LANES] = jnp.where(is_head, _head_slot(fk, hd), k_feat_f).astype(BF16)
        hi, mid, lo = _bf16_pieces(decay_t[hd:hd + 1, :])
        q_feat_f = jnp.where(sub8 == 0, hi,
                             jnp.where(sub8 == 1, mid,
                                       jnp.where(sub8 == 2, lo,
                                                 jnp.where(sub8 < 6, 1.0, 0.0))))
        qft_ref[0, 0, hd * LANES:(hd + 1) * LANES, :] = jnp.concatenate(
            [fqt[hd * HEAD_DIM:(hd + 1) * HEAD_DIM, :], q_feat_f, zeros_tail_f], axis=0).astype(BF16)
    store_vt(vft_ref, tr(_T_FV))

    cqn = _rms(tok(_TOK_CQ), qn_ref[...]).astype(BF16)
    qm_all = _dot_nt(wuq_ref[...], cqn)
    qm = qm_all[0:N_HEADS_MLA * LANES, :]
    qmr = qm_all[N_HEADS_MLA * LANES:, :]
    kvn = _rms(tok(_TOK_CKV), kvn_ref[...]).astype(BF16)
    k_nope = _dot(kvn, wukvk_ref[...])
    k_rope = tok(_TOK_KR) * ctok_ref[...] + tok(_TOK_KRR) * stok_ref[...]
    cos_t = ct_ref[...]
    sin_t = st_ref[...]
    for hd in range(N_HEADS_MLA):
        sl = slice(hd * LANES, (hd + 1) * LANES)
        qmt_ref[0, 0, sl, :] = ((qm[sl, :] * cos_t + qmr[sl, :] * sin_t) * (MLA_QK_DIM ** -0.5 * LOG2E)).astype(BF16)
        km_ref[0, :, sl] = (k_nope[:, sl] + k_rope).astype(BF16)
    store_vt(vmt_ref, _dot_nt(wukvv_ref[...], kvn))


def _proj_call(x, g, p, tables, far, tm):
    b, s, d = x.shape
    nt = s // tm

    def full(a):
        return pl.BlockSpec(a.shape, lambda bi, ti: (0,) * a.ndim)

    ctok, stok, ct, st = tables
    in_specs = [
        pl.BlockSpec((1, tm, d), lambda bi, ti: (bi, ti, 0)),
        full(g), full(p["w_tok"]), full(p["w_t"]), full(p["b_f"]), full(p["q_norm"]), full(p["kv_norm"]),
        full(p["wuq_t"]), full(p["wukv_k"]), full(p["wukv_v_t"]),
        pl.BlockSpec((tm, LANES), lambda bi, ti: (ti, 0)),
        pl.BlockSpec((tm, LANES), lambda bi, ti: (ti, 0)),
        pl.BlockSpec((LANES, tm), lambda bi, ti: (0, ti)),
        pl.BlockSpec((LANES, tm), lambda bi, ti: (0, ti)),
        pl.BlockSpec(memory_space=pltpu.SMEM),
    ]

    def qt_spec(n_heads):
        return pl.BlockSpec((1, 1, n_heads * LANES, tm), lambda bi, ti: (bi, ti, 0, 0))

    def k_spec(n_heads):
        return pl.BlockSpec((1, tm, n_heads * LANES), lambda bi, ti: (bi, ti, 0))

    def vt_spec(width):
        return pl.BlockSpec((1, 1, width, tm), lambda bi, ti: (bi, ti, 0, 0))

    def qt_shape(n_heads):
        return jax.ShapeDtypeStruct((b, nt, n_heads * LANES, tm), BF16)

    def k_shape(n_heads):
        return jax.ShapeDtypeStruct((b, s, n_heads * LANES), BF16)

    def vt_shape(width):
        return jax.ShapeDtypeStruct((b, nt, width, tm), BF16)

    heads = (N_HEADS_MOBA, N_HEADS_FOX, N_HEADS_MLA)
    out_specs = [spec for n in heads for spec in (qt_spec(n), k_spec(n), vt_spec(n * V_ROWS))]
    out_shape = [shp for n in heads for shp in (qt_shape(n), k_shape(n), vt_shape(n * V_ROWS))]
    return pl.pallas_call(
        functools.partial(_proj_kernel, tm=tm),
        grid=(b, nt),
        in_specs=in_specs,
        out_specs=out_specs,
        out_shape=out_shape,
        scratch_shapes=[pltpu.VMEM((N_HEADS_MOBA * NBLK_PAD, W_MOBA), F32),
                        pltpu.VMEM((SUBLANES, LANES), F32)],
        compiler_params=pltpu.CompilerParams(dimension_semantics=("arbitrary", "arbitrary"),
                                             vmem_limit_bytes=VMEM_LIMIT),
        name="proj",
    )(x, g, p["w_tok"], p["w_t"], p["b_f"], p["q_norm"], p["kv_norm"], p["wuq_t"],
      p["wukv_k"], p["wukv_v_t"], ctok, stok, ct, st, far)


def _attn_kernel(*refs, has_bias, tq, tk, nq, n_add, n_plain):
    if has_bias:
        (q_ref, j_ref, qt_ref, k_ref, vt_ref, bidx_ref, tab_ref, o_ref,
         acc_ref, m_ref, *bufs, bias_ref) = refs
    else:
        (q_ref, j_ref, qt_ref, k_ref, vt_ref, o_ref,
         acc_ref, m_ref, *bufs, bias_ref) = refs
    hp = pl.program_id(0)
    bi = pl.program_id(1)
    blk = MOBA_BLOCK
    n_near = N_NEAR if has_bias else 1

    @pl.when(bi == 0)
    def _():
        tri = (lax.broadcasted_iota(jnp.int32, (blk, blk), 0) <= lax.broadcasted_iota(jnp.int32, (blk, blk), 1))
        for hh in range(2):
            bias_ref[hh, 0] = jnp.full((blk, blk), NEG_INF, F32)
            bias_ref[hh, n_near + 1] = jnp.zeros((blk, blk), F32)
            if has_bias:
                def build(d, carry, hh=hh):
                    idx = bidx_ref[d]
                    bias_ref[hh, d + 1] = lax.fori_loop(
                        0, T5_BUCKETS,
                        lambda bk, tl: jnp.where(idx == bk, tab_ref[bk, hp * 2 + hh] * LOG2E, tl),
                        jnp.zeros((blk, blk), F32))
                    return carry
                lax.fori_loop(0, n_near, build, 0)
                bias_ref[hh, 1] = jnp.where(tri, bias_ref[hh, 1], NEG_INF)
            else:
                bias_ref[hh, 1] = jnp.where(tri, 0.0, NEG_INF)

    acc_ref[...] = jnp.zeros_like(acc_ref)
    m_ref[...] = jnp.full(m_ref.shape, NEG_INF, F32)

    n_items = n_add + n_plain

    def scores(e, s_ref, additive):
        qi = q_ref[e]
        j = j_ref[e]
        start = pl.multiple_of(j * tk, tk)
        tmax = []
        for hh in range(2):
            k = k_ref[0, pl.ds(start, tk), hh * LANES:(hh + 1) * LANES]
            s = _dot(k, qt_ref[0, qi, hh * LANES:(hh + 1) * LANES, :])
            if additive:
                s = s + jnp.concatenate([
                    jnp.concatenate([
                        bias_ref[hh, jnp.clip((qi * (tq // blk) + c) - (j * (tk // blk) + r), -1, n_near) + 1]
                        for c in range(tq // blk)], axis=1)
                    for r in range(tk // blk)], axis=0)
            s_ref[hh] = s
            tmax.append(jnp.max(s, axis=0, keepdims=True))
        return tuple(tmax)

    def update(e, s_ref, tmax):
        qi = q_ref[e]
        j = j_ref[e]
        for hh in range(2):
            m_old = m_ref[qi, hh, 0:1, :]
            m_new = jnp.maximum(m_old, tmax[hh])
            alpha = jnp.exp2(m_old - m_new)
            p = jnp.exp2(s_ref[hh] - m_new).astype(BF16)
            v = vt_ref[0, j, hh * V_ROWS:(hh + 1) * V_ROWS, :]
            acc_ref[qi, hh] = alpha * acc_ref[qi, hh] + _dot(v, p)
            m_ref[qi, hh, 0:1, :] = m_new

    def step(e, r, carry, additive):
        tmax_cur, tmax_nxt = carry
        tmax_new = scores(e + 2, bufs[(r + 2) % ATT_STAGES], additive)
        update(e, bufs[r], tmax_cur)
        return tmax_nxt, tmax_new

    def steps(lo, hi, carry, additive):
        n_groups = max(hi - lo, 0) // ATT_GROUP

        def group(g, carry):
            for r in range(ATT_GROUP):
                carry = step(lo + ATT_GROUP * g + r, (lo + r) % ATT_STAGES, carry, additive)
            return carry

        carry = lax.fori_loop(0, n_groups, group, carry)
        for e in range(lo + ATT_GROUP * n_groups, hi):
            carry = step(e, e % ATT_STAGES, carry, additive)
        return carry

    carry = (scores(0, bufs[0], 0 < n_add), scores(1, bufs[1], 1 < n_add))
    split = max(n_add - 2, 0)
    carry = steps(0, split, carry, True)
    carry = steps(split, n_items - 2, carry, False)
    update(n_items - 2, bufs[(n_items - 2) % ATT_STAGES], carry[0])
    update(n_items - 1, bufs[(n_items - 1) % ATT_STAGES], carry[1])

    def finish(qi, carry):
        o = [acc_ref[qi, hh, 0:HEAD_DIM, :] / acc_ref[qi, hh, HEAD_DIM:HEAD_DIM + 1, :] for hh in range(2)]
        o_ref[0, pl.ds(pl.multiple_of(qi * tq, tq), tq), :] = jnp.concatenate(o, axis=0).T.astype(BF16)
        return carry

    lax.fori_loop(0, nq, finish, 0)


def _attn_schedule(nq, tq, tk, n_near):
    blk = MOBA_BLOCK
    add, plain = [], []
    for q in range(nq):
        for j in range(q, -1, -1):
            min_dist = q * (tq // blk) - (j * (tk // blk) + tk // blk - 1)
            (add if min_dist < n_near else plain).append((q, j))
    items = add + plain
    return (jnp.asarray([q for q, _ in items], jnp.int32), jnp.asarray([j for _, j in items], jnp.int32),
            len(add), len(plain))


def _attn_call(qt, k, vt, n_pairs, bias_inputs=None):
    b, nq, _, tq = qt.shape
    _, s, _ = k.shape
    tk = vt.shape[-1]
    assert tq == tk, "the causal-triangle tile assumes equal query and key tiles"
    has_bias = bias_inputs is not None
    q_tab, j_tab, n_add, n_plain = _attn_schedule(nq, tq, tk, N_NEAR if has_bias else 1)
    assert n_add + n_plain >= 2
    smem = pl.BlockSpec(memory_space=pltpu.SMEM)
    in_specs = [
        smem, smem,
        pl.BlockSpec((1, nq, 2 * LANES, tq), lambda p, bi: (bi, 0, p, 0)),
        pl.BlockSpec((1, s, 2 * LANES), lambda p, bi: (bi, 0, p)),
        pl.BlockSpec((1, s // tk, 2 * V_ROWS, tk), lambda p, bi: (bi, 0, p, 0)),
    ]
    args = [q_tab, j_tab, qt, k, vt]
    n_tiles = (N_NEAR if has_bias else 1) + 2
    scratch = [pltpu.VMEM((nq, 2, V_ROWS, tq), F32), pltpu.VMEM((nq, 2, SUBLANES, tq), F32),
               *[pltpu.VMEM((2, tk, tq), F32) for _ in range(ATT_STAGES)],
               pltpu.VMEM((2, n_tiles, MOBA_BLOCK, MOBA_BLOCK), F32)]
    if has_bias:
        bidx, tab = bias_inputs
        in_specs += [pl.BlockSpec(bidx.shape, lambda p, bi: (0, 0, 0)), smem]
        args += [bidx, tab]
    return pl.pallas_call(
        functools.partial(_attn_kernel, has_bias=has_bias, tq=tq, tk=tk, nq=nq, n_add=n_add, n_plain=n_plain),
        grid=(n_pairs, b),
        in_specs=in_specs,
        out_specs=pl.BlockSpec((1, s, 2 * HEAD_DIM), lambda p, bi: (bi, 0, p)),
        out_shape=jax.ShapeDtypeStruct((b, s, n_pairs * 2 * HEAD_DIM), BF16),
        scratch_shapes=scratch,
        compiler_params=pltpu.CompilerParams(dimension_semantics=("arbitrary",) * 2,
                                             vmem_limit_bytes=VMEM_LIMIT),
        name="attn_bias" if has_bias else "attn",
    )(*args)


def _oproj_kernel(x_ref, oa_ref, ob_ref, oc_ref, wo_ref, g_ref, o_ref):
    merged = jnp.concatenate([oa_ref[0], ob_ref[0], oc_ref[0]], axis=1)
    o_ref[0] = x_ref[0] + _rms(_dot(merged, wo_ref[...]), g_ref[...])


def _oproj_call(x, oa, ob, oc, p, g, tm):
    b, s, d = x.shape

    def rows(width):
        return pl.BlockSpec((1, tm, width), lambda bi, ti: (bi, ti, 0))

    def full(a):
        return pl.BlockSpec(a.shape, lambda bi, ti: (0,) * a.ndim)

    return pl.pallas_call(
        _oproj_kernel,
        grid=(b, s // tm),
        in_specs=[rows(d), rows(W_MOBA), rows(W_MLA), rows(W_FOX),
                  full(p["w_o"]), full(g)],
        out_specs=rows(d),
        out_shape=jax.ShapeDtypeStruct(x.shape, F32),
        compiler_params=pltpu.CompilerParams(dimension_semantics=("arbitrary", "arbitrary"),
                                             vmem_limit_bytes=VMEM_LIMIT),
        name="oproj",
    )(x, oa, ob, oc, p["w_o"], g)


def _ffn_kernel(x_ref, xp_ref, gpre_ref, wup_ref, cw_ref, cb_ref, wd_ref, gpost_ref, o_ref, h_ref, u_ref, *, tm):
    ti = pl.program_id(1)
    halo = FFN_HALO

    h_ref[halo:, :] = _rms(x_ref[0], gpre_ref[...]).astype(BF16)
    prev = _rms(xp_ref[0], gpre_ref[...])
    h_ref[0:halo, :] = jnp.where(ti == 0, 0.0, prev).astype(BF16)

    h = h_ref[...]
    for half in range(2):
        u_ref[half] = _dot(h, wup_ref[:, half * D_FF:(half + 1) * D_FF])

    def conv(half):
        cols = slice(half * D_FF, (half + 1) * D_FF)
        out = cb_ref[:, cols]
        for tap in range(CONV_WIDTH):
            first = halo - (CONV_WIDTH - 1) + tap
            out = out + cw_ref[tap:tap + 1, cols] * u_ref[half, first:first + tm, :]
        return out

    act = (jax.nn.gelu(conv(0), approximate=True) * conv(1)).astype(BF16)
    o_ref[0] = x_ref[0] + _rms(_dot(act, wd_ref[...]), gpost_ref[...])


def _ffn_call(x, p, gpre, gpost, tm):
    b, s, d = x.shape
    halo = FFN_HALO
    blocks_per_tile = tm // halo

    def full(a):
        return pl.BlockSpec(a.shape, lambda bi, ti: (0,) * a.ndim)

    in_specs = [
        pl.BlockSpec((1, tm, d), lambda bi, ti: (bi, ti, 0)),
        pl.BlockSpec((1, halo, d), lambda bi, ti: (bi, jnp.maximum(ti * blocks_per_tile - 1, 0), 0)),
        full(gpre), full(p["w_up"]), full(p["conv_w"]), full(p["conv_b"]), full(p["w_down"]), full(gpost),
    ]
    return pl.pallas_call(
        functools.partial(_ffn_kernel, tm=tm),
        grid=(b, s // tm),
        in_specs=in_specs,
        out_specs=pl.BlockSpec((1, tm, d), lambda bi, ti: (bi, ti, 0)),
        out_shape=jax.ShapeDtypeStruct(x.shape, F32),
        scratch_shapes=[pltpu.VMEM((halo + tm, d), BF16), pltpu.VMEM((2, halo + tm, D_FF), F32)],
        compiler_params=pltpu.CompilerParams(dimension_semantics=("arbitrary",) * 2,
                                             vmem_limit_bytes=VMEM_LIMIT),
        name="ffn",
    )(x, x, gpre, p["w_up"], p["conv_w"], p["conv_b"], p["w_down"], gpost)


def _pad_cols(w, left, total):
    return jnp.pad(w, ((0, 0), (left, total - left - w.shape[1])))


def _prep_layer(w_in, b_f, q_norm, kv_norm, w_uq, w_ukv, w_o, w_up, conv_w, conv_b, w_down):
    offs = [0]
    for sz in PROJ_SIZES:
        offs.append(offs[-1] + sz)
    a_q, a_k, a_v, c_q, c_kv, k_r, f_q, f_k, f_v, f_g = [w_in[:, offs[i]:offs[i + 1]] for i in range(10)]
    half = MLA_ROPE_DIM // 2

    def rot(w):
        return jnp.concatenate([-w[:, half:], w[:, :half]], axis=1)

    w_tok = jnp.concatenate([
        a_k, f_k, c_q, c_kv,
        _pad_cols(k_r, FEAT0, LANES), _pad_cols(rot(k_r), FEAT0, LANES), _pad_cols(f_g, 0, LANES)], axis=1)
    w_t = jnp.concatenate([a_q, f_q, a_v, f_v], axis=1).T
    uq = w_uq.reshape(MLA_Q_RANK, N_HEADS_MLA, MLA_QK_DIM)
    uq_main = jnp.pad(uq, ((0, 0), (0, 0), (0, LANES - MLA_QK_DIM)))
    uq_rope = uq[:, :, MLA_NOPE_DIM:]
    uq_rot = jnp.concatenate([-uq_rope[:, :, half:], uq_rope[:, :, :half]], axis=2)
    uq_rot = jnp.pad(uq_rot, ((0, 0), (0, 0), (MLA_NOPE_DIM, LANES - MLA_QK_DIM)))
    ukv = w_ukv.reshape(MLA_KV_RANK, N_HEADS_MLA, MLA_NOPE_DIM + MLA_V_DIM)
    ukv_k = jnp.pad(ukv[:, :, :MLA_NOPE_DIM], ((0, 0), (0, 0), (0, LANES - MLA_NOPE_DIM)))
    ukv_v = ukv[:, :, MLA_NOPE_DIM:]
    return {
        "w_tok": w_tok.astype(BF16),
        "w_t": w_t.astype(BF16),
        "b_f": _pad_cols(b_f[None, :], 0, LANES),
        "q_norm": q_norm[None, :],
        "kv_norm": kv_norm[None, :],
        "wuq_t": jnp.concatenate([uq_main.reshape(MLA_Q_RANK, N_HEADS_MLA * LANES),
                                  uq_rot.reshape(MLA_Q_RANK, N_HEADS_MLA * LANES)], axis=1).T.astype(BF16),
        "wukv_k": ukv_k.reshape(MLA_KV_RANK, N_HEADS_MLA * LANES).astype(BF16),
        "wukv_v_t": ukv_v.reshape(MLA_KV_RANK, W_MLA).T.astype(BF16),
        "w_o": w_o.astype(BF16),
        "w_up": w_up.astype(BF16),
        "conv_w": conv_w,
        "conv_b": conv_b[None, :],
        "w_down": w_down.astype(BF16),
    }


def _rope_tables(s):
    half = MLA_ROPE_DIM // 2
    inv = ROPE_THETA ** (-jnp.arange(half, dtype=F32) / half)
    ang = jnp.arange(s).astype(F32)[:, None] * inv[None, :]
    cos = jnp.concatenate([jnp.cos(ang)] * 2, axis=1)
    sin = jnp.concatenate([jnp.sin(ang)] * 2, axis=1)
    ctok = _pad_cols(cos, FEAT0, LANES)
    stok = _pad_cols(sin, FEAT0, LANES)
    ones = jnp.ones((s, MLA_NOPE_DIM), F32)
    ct = jnp.concatenate([ones, cos, jnp.zeros((s, LANES - MLA_QK_DIM), F32)], axis=1).T
    st = stok.T
    return ctok, stok, ct, st


def _t5_bucket_tiles(tile):
    d = jnp.arange(N_NEAR)[:, None, None]
    j = jnp.arange(tile)[None, :, None]
    i = jnp.arange(tile)[None, None, :]
    n = jnp.maximum(d * tile + i - j, 0)
    exact = T5_BUCKETS // 2
    large = exact + (jnp.log(jnp.maximum(n, 1).astype(F32) / exact)
                     / math.log(T5_MAX_DIST / exact) * (T5_BUCKETS - exact)).astype(jnp.int32)
    return jnp.where(n < exact, n, jnp.minimum(large, T5_BUCKETS - 1)).astype(jnp.int32)


def kernel(x, rel_bias, ln_mix_pre, ln_mix_post, ln_ffn_pre, ln_ffn_post, w_in, b_f, q_norm, kv_norm, w_uq,
           w_ukv, w_o, w_up, conv_w, conv_b, w_down):
    b, s, d = x.shape
    depth = w_in.shape[0]
    assert s % PROJ_ROWS == 0 and s % FFN_ROWS == 0 and s % OPROJ_ROWS == 0
    assert s // MOBA_BLOCK <= NBLK_PAD
    tables = _rope_tables(s)
    bidx = _t5_bucket_tiles(MOBA_BLOCK)
    far = rel_bias[T5_BUCKETS - 1, :]
    for l in range(depth):
        p = _prep_layer(w_in[l], b_f[l], q_norm[l], kv_norm[l], w_uq[l], w_ukv[l], w_o[l], w_up[l],
                        conv_w[l], conv_b[l], w_down[l])
        qat, ka, vat, qft, kf, vft, qmt, km, vmt = _proj_call(x, ln_mix_pre[l][None, :], p, tables, far, PROJ_ROWS)
        oa = _attn_call(qat, ka, vat, N_HEADS_MOBA // 2, (bidx, rel_bias))
        ob = _attn_call(qmt, km, vmt, N_HEADS_MLA // 2)
        oc = _attn_call(qft, kf, vft, N_HEADS_FOX // 2)
        x = _oproj_call(x, oa, ob, oc, p, ln_mix_post[l][None, :], OPROJ_ROWS)
        x = _ffn_call(x, p, ln_ffn_pre[l][None, :], ln_ffn_post[l][None, :], FFN_ROWS)
    return x
```

```python
import functools
import math

import jax
import jax.numpy as jnp
from jax import lax
from jax.experimental import pallas as pl
from jax.experimental.pallas import tpu as pltpu

F32 = jnp.float32
BF16 = jnp.bfloat16

HEAD_DIM = 64
N_HEADS_MOBA = 6
N_HEADS_MLA = 4
N_HEADS_FOX = 6
MOBA_BLOCK = 256
MOBA_TOPK = 3
MLA_Q_RANK = 256
MLA_KV_RANK = 128
MLA_NOPE_DIM = 64
MLA_ROPE_DIM = 32
MLA_V_DIM = 64
MLA_QK_DIM = MLA_NOPE_DIM + MLA_ROPE_DIM
ROPE_THETA = 10000.0
T5_BUCKETS = 32
T5_MAX_DIST = 1024
D_FF = 2816
CONV_WIDTH = 3
NORM_EPS = 1e-6
NEG_INF = -1e30
W_MOBA = N_HEADS_MOBA * HEAD_DIM
W_MLA = N_HEADS_MLA * MLA_V_DIM
W_FOX = N_HEADS_FOX * HEAD_DIM
PROJ_SIZES = (W_MOBA, W_MOBA, W_MOBA, MLA_Q_RANK, MLA_KV_RANK, MLA_ROPE_DIM, W_FOX, W_FOX, W_FOX, N_HEADS_FOX)

LANES = 128
SUBLANES = 8
BF16_ROWS = 16
VMEM_LIMIT = 56 * 1024 * 1024

ATT_TILE = 512
PROJ_ROWS = ATT_TILE
ATT_STAGES = 4
ATT_GROUP = 2 * ATT_STAGES
OPROJ_ROWS = 1024
FFN_ROWS = 512
FFN_HALO = BF16_ROWS
FEAT0 = HEAD_DIM
V_ROWS = HEAD_DIM + BF16_ROWS
LOG2E = math.log2(math.e)
NBLK_PAD = 16
N_NEAR = (T5_MAX_DIST + MOBA_BLOCK - 1 + MOBA_BLOCK - 1) // MOBA_BLOCK


def _ranges(sizes):
    out, start = [], 0
    for size in sizes:
        out.append((start, start + size))
        start += size
    return out


_TOK_AK, _TOK_FK, _TOK_CQ, _TOK_CKV, _TOK_KR, _TOK_KRR, _TOK_FG = _ranges(
    (W_MOBA, W_FOX, MLA_Q_RANK, MLA_KV_RANK, LANES, LANES, LANES))
_T_AQ, _T_FQ, _T_AV, _T_FV = _ranges((W_MOBA, W_FOX, W_MOBA, W_FOX))


def _rms(xf, g):
    return xf * lax.rsqrt(jnp.mean(xf * xf, axis=-1, keepdims=True) + NORM_EPS) * g


def _dot(a, b):
    return jnp.dot(a, b, preferred_element_type=F32)


def _dot_nt(a, b):
    return lax.dot_general(a, b, (((1,), (1,)), ((), ())), preferred_element_type=F32)


def _bf16_pieces(x):
    hi = x.astype(BF16).astype(F32)
    r = x - hi
    mid = r.astype(BF16).astype(F32)
    lo = (r - mid).astype(BF16).astype(F32)
    return hi, mid, lo


def _dot_3pass(a, b):
    a_hi = a.astype(BF16)
    a_lo = (a - a_hi.astype(F32)).astype(BF16)
    b_hi = b.astype(BF16)
    b_lo = (b - b_hi.astype(F32)).astype(BF16)
    return _dot(jnp.concatenate([a_hi, a_hi, a_lo], axis=1), jnp.concatenate([b_hi, b_lo, b_hi], axis=0))


def _head_slot(arr, hd):
    pair = arr[:, (hd // 2) * LANES:(hd // 2 + 1) * LANES]
    if hd % 2:
        pair = pltpu.roll(pair, HEAD_DIM, axis=1)
    return pair


def _proj_kernel(x_ref, g_ref, wtok_ref, wt_ref, bf_ref, qn_ref, kvn_ref, wuq_ref,
                 wukvk_ref, wukvv_ref, ctok_ref, stok_ref, ct_ref, st_ref, far_ref,
                 qat_ref, ka_ref, vat_ref, qft_ref, kf_ref, vft_ref, qmt_ref, km_ref, vmt_ref,
                 kmean_ref, fcarry_ref, *, tm):
    t = pl.program_id(1)
    n_sub = tm // MOBA_BLOCK

    @pl.when(t == 0)
    def _():
        kmean_ref[...] = jnp.zeros_like(kmean_ref)
        fcarry_ref[...] = jnp.zeros_like(fcarry_ref)

    h = _rms(x_ref[0], g_ref[...]).astype(BF16)
    lane = lax.broadcasted_iota(jnp.int32, (tm, LANES), 1)
    row = lax.broadcasted_iota(jnp.int32, (tm, LANES), 0)
    blk0 = t * n_sub
    is_head = lane < HEAD_DIM

    tok_all = _dot(h, wtok_ref[...])
    t_all = _dot_nt(wt_ref[...], h)

    def tok(rng):
        return tok_all[:, rng[0]:rng[1]]

    def tr(rng):
        return t_all[rng[0]:rng[1], :]

    sub8v = lax.broadcasted_iota(jnp.int32, (SUBLANES, tm), 0)
    ones_row = jnp.where(sub8v == 0, 1.0, 0.0)
    v_pad = jnp.zeros((V_ROWS - HEAD_DIM - SUBLANES, tm), F32)

    def store_vt(ref, vt):
        n_heads = vt.shape[0] // HEAD_DIM
        ref[0, 0] = jnp.concatenate(
            [piece for hd in range(n_heads)
             for piece in (vt[hd * HEAD_DIM:(hd + 1) * HEAD_DIM, :], ones_row, v_pad)], axis=0).astype(BF16)

    ak = tok(_TOK_AK)
    blk_row = blk0 + lax.shift_right_logical(row, int(math.log2(MOBA_BLOCK)))
    k_feat = jnp.where(lane == FEAT0 + blk_row, 1.0, 0.0)
    for hd in range(N_HEADS_MOBA):
        ka_ref[0, :, hd * LANES:(hd + 1) * LANES] = jnp.where(is_head, _head_slot(ak, hd), k_feat).astype(BF16)

    lane_k = lax.broadcasted_iota(jnp.int32, (1, W_MOBA), 1)
    for bi in range(n_sub):
        mean_row = jnp.mean(ak[bi * MOBA_BLOCK:(bi + 1) * MOBA_BLOCK, :], axis=0, keepdims=True)
        for hd in range(N_HEADS_MOBA):
            in_head = (lane_k >= hd * HEAD_DIM) & (lane_k < (hd + 1) * HEAD_DIM)
            kmean_ref[pl.ds(hd * NBLK_PAD + blk0 + bi, 1), :] = jnp.where(in_head, mean_row, 0.0)

    qat = tr(_T_AQ) * (HEAD_DIM ** -0.5 * LOG2E)
    gate = _dot_3pass(kmean_ref[...], qat)
    own = blk0 + lax.shift_right_logical(
        lax.broadcasted_iota(jnp.int32, (NBLK_PAD, tm), 1), int(math.log2(MOBA_BLOCK)))
    n_io = lax.broadcasted_iota(jnp.int32, (NBLK_PAD, tm), 0)
    past = n_io < own
    zeros_tail = jnp.zeros((LANES - HEAD_DIM - NBLK_PAD, tm), F32)
    for hd in range(N_HEADS_MOBA):
        g = jnp.where(past, gate[hd * NBLK_PAD:(hd + 1) * NBLK_PAD, :], NEG_INF)
        rank = jnp.zeros((NBLK_PAD, tm), jnp.int32)
        for n2 in range(NBLK_PAD):
            r = g[n2:n2 + 1, :]
            tie = jnp.where(n2 < n_io, 1, 0)
            rank = rank + jnp.where(r > g, 1, jnp.where(r == g, tie, 0))
        sel = jnp.where(past, rank, MOBA_TOPK) < MOBA_TOPK
        far = jnp.where(own - n_io >= N_NEAR, far_ref[hd] * LOG2E, 0.0)
        q_feat = jnp.where(sel | (n_io == own), far, NEG_INF)
        qat_ref[0, 0, hd * LANES:(hd + 1) * LANES, :] = jnp.concatenate(
            [qat[hd * HEAD_DIM:(hd + 1) * HEAD_DIM, :], q_feat, zeros_tail], axis=0).astype(BF16)
    store_vt(vat_ref, tr(_T_AV))

    fg = tok(_TOK_FG) + bf_ref[...]
    logf = jnp.minimum(fg, 0.0) - jnp.log1p(jnp.exp(-jnp.abs(fg)))
    csum = jnp.where(lane < N_HEADS_FOX, logf, 0.0)
    sft = 1
    while sft < tm:
        csum = csum + jnp.where(row >= sft, pltpu.roll(csum, sft, axis=0), 0.0)
        sft *= 2
    decay = csum + fcarry_ref[0:1, :]
    fcarry_ref[0:1, :] = decay[tm - 1:tm, :]
    decay = decay * LOG2E
    decay_t = decay.T

    fk = tok(_TOK_FK)
    fqt = tr(_T_FQ) * (HEAD_DIM ** -0.5 * LOG2E)
    sub8 = lax.broadcasted_iota(jnp.int32, (SUBLANES, tm), 0)
    zeros_tail_f = jnp.zeros((LANES - HEAD_DIM - SUBLANES, tm), F32)
    for hd in range(N_HEADS_FOX):
        fcol = jnp.broadcast_to(decay[:, hd:hd + 1], (tm, LANES))
        hi, mid, lo = _bf16_pieces(fcol)
        k_feat_f = jnp.where(lane < FEAT0 + 3, 1.0,
                             jnp.where(lane == FEAT0 + 3, -hi,
                                       jnp.where(lane == FEAT0 + 4, -mid,
                                                 jnp.where(lane == FEAT0 + 5, -lo, 0.0))))
        kf_ref[0, :, hd * LANES:(hd + 1) * LANES] = jnp.where(is_head, _head_slot(fk, hd), k_feat_f).astype(BF16)
        hi, mid, lo = _bf16_pieces(decay_t[hd:hd + 1, :])
        q_feat_f = jnp.where(sub8 == 0, hi,
                             jnp.where(sub8 == 1, mid,
                                       jnp.where(sub8 == 2, lo,
                                                 jnp.where(sub8 < 6, 1.0, 0.0))))
        qft_ref[0, 0, hd * LANES:(hd + 1) * LANES, :] = jnp.concatenate(
            [fqt[hd * HEAD_DIM:(hd + 1) * HEAD_DIM, :], q_feat_f, zeros_tail_f], axis=0).astype(BF16)
    store_vt(vft_ref, tr(_T_FV))

    cqn = _rms(tok(_TOK_CQ), qn_ref[...]).astype(BF16)
    qm_all = _dot_nt(wuq_ref[...], cqn)
    qm = qm_all[0:N_HEADS_MLA * LANES, :]
    qmr = qm_all[N_HEADS_MLA * LANES:, :]
    kvn = _rms(tok(_TOK_CKV), kvn_ref[...]).astype(BF16)
    k_nope = _dot(kvn, wukvk_ref[...])
    k_rope = tok(_TOK_KR) * ctok_ref[...] + tok(_TOK_KRR) * stok_ref[...]
    cos_t = ct_ref[...]
    sin_t = st_ref[...]
    for hd in range(N_HEADS_MLA):
        sl = slice(hd * LANES, (hd + 1) * LANES)
        qmt_ref[0, 0, sl, :] = ((qm[sl, :] * cos_t + qmr[sl, :] * sin_t) * (MLA_QK_DIM ** -0.5 * LOG2E)).astype(BF16)
        km_ref[0, :, sl] = (k_nope[:, sl] + k_rope).astype(BF16)
    store_vt(vmt_ref, _dot_nt(wukvv_ref[...], kvn))


def _proj_call(x, p, layer, tables, far, tm):
    b, s, d = x.shape
    nt = s // tm
    weights = [p[name] for name in ("ln_mix_pre", "w_tok", "w_t", "b_f", "q_norm", "kv_norm", "wuq_t", "wukv_k",
                                    "wukv_v_t")]
    ctok, stok, ct, st = tables
    in_specs = [
        pl.BlockSpec((1, tm, d), lambda bi, ti: (bi, ti, 0)),
        *[_layer_spec(w, layer) for w in weights],
        pl.BlockSpec((tm, LANES), lambda bi, ti: (ti, 0)),
        pl.BlockSpec((tm, LANES), lambda bi, ti: (ti, 0)),
        pl.BlockSpec((LANES, tm), lambda bi, ti: (0, ti)),
        pl.BlockSpec((LANES, tm), lambda bi, ti: (0, ti)),
        pl.BlockSpec(memory_space=pltpu.SMEM),
    ]

    def qt_spec(n_heads):
        return pl.BlockSpec((1, 1, n_heads * LANES, tm), lambda bi, ti: (bi, ti, 0, 0))

    def k_spec(n_heads):
        return pl.BlockSpec((1, tm, n_heads * LANES), lambda bi, ti: (bi, ti, 0))

    def vt_spec(width):
        return pl.BlockSpec((1, 1, width, tm), lambda bi, ti: (bi, ti, 0, 0))

    def qt_shape(n_heads):
        return jax.ShapeDtypeStruct((b, nt, n_heads * LANES, tm), BF16)

    def k_shape(n_heads):
        return jax.ShapeDtypeStruct((b, s, n_heads * LANES), BF16)

    def vt_shape(width):
        return jax.ShapeDtypeStruct((b, nt, width, tm), BF16)

    heads = (N_HEADS_MOBA, N_HEADS_FOX, N_HEADS_MLA)
    out_specs = [spec for n in heads for spec in (qt_spec(n), k_spec(n), vt_spec(n * V_ROWS))]
    out_shape = [shp for n in heads for shp in (qt_shape(n), k_shape(n), vt_shape(n * V_ROWS))]
    return pl.pallas_call(
        functools.partial(_proj_kernel, tm=tm),
        grid=(b, nt),
        in_specs=in_specs,
        out_specs=out_specs,
        out_shape=out_shape,
        scratch_shapes=[pltpu.VMEM((N_HEADS_MOBA * NBLK_PAD, W_MOBA), F32),
                        pltpu.VMEM((SUBLANES, LANES), F32)],
        compiler_params=pltpu.CompilerParams(dimension_semantics=("arbitrary", "arbitrary"),
                                             vmem_limit_bytes=VMEM_LIMIT),
        name="proj",
    )(x, *weights, ctok, stok, ct, st, far)


def _attn_kernel(*refs, has_bias, tq, tk, nq, n_add, n_plain):
    if has_bias:
        (q_ref, j_ref, qt_ref, k_ref, vt_ref, bidx_ref, tab_ref, o_ref,
         acc_ref, m_ref, *bufs, bias_ref) = refs
    else:
        (q_ref, j_ref, qt_ref, k_ref, vt_ref, o_ref,
         acc_ref, m_ref, *bufs, bias_ref) = refs
    hp = pl.program_id(0)
    bi = pl.program_id(1)
    blk = MOBA_BLOCK
    n_near = N_NEAR if has_bias else 1

    @pl.when(bi == 0)
    def _():
        tri = (lax.broadcasted_iota(jnp.int32, (blk, blk), 0) <= lax.broadcasted_iota(jnp.int32, (blk, blk), 1))
        for hh in range(2):
            bias_ref[hh, 0] = jnp.full((blk, blk), NEG_INF, F32)
            bias_ref[hh, n_near + 1] = jnp.zeros((blk, blk), F32)
            if has_bias:
                def build(d, carry, hh=hh):
                    idx = bidx_ref[d]
                    bias_ref[hh, d + 1] = lax.fori_loop(
                        0, T5_BUCKETS,
                        lambda bk, tl: jnp.where(idx == bk, tab_ref[bk, hp * 2 + hh] * LOG2E, tl),
                        jnp.zeros((blk, blk), F32))
                    return carry
                lax.fori_loop(0, n_near, build, 0)
                bias_ref[hh, 1] = jnp.where(tri, bias_ref[hh, 1], NEG_INF)
            else:
                bias_ref[hh, 1] = jnp.where(tri, 0.0, NEG_INF)

    acc_ref[...] = jnp.zeros_like(acc_ref)
    m_ref[...] = jnp.full(m_ref.shape, NEG_INF, F32)

    n_items = n_add + n_plain

    def scores(e, s_ref, additive):
        qi = q_ref[e]
        j = j_ref[e]
        start = pl.multiple_of(j * tk, tk)
        tmax = []
        for hh in range(2):
            k = k_ref[0, pl.ds(start, tk), hh * LANES:(hh + 1) * LANES]
            s = _dot(k, qt_ref[0, qi, hh * LANES:(hh + 1) * LANES, :])
            if additive:
                s = s + jnp.concatenate([
                    jnp.concatenate([
                        bias_ref[hh, jnp.clip((qi * (tq // blk) + c) - (j * (tk // blk) + r), -1, n_near) + 1]
                        for c in range(tq // blk)], axis=1)
                    for r in range(tk // blk)], axis=0)
            s_ref[hh] = s
            tmax.append(jnp.max(s, axis=0, keepdims=True))
        return tuple(tmax)

    def update(e, s_ref, tmax):
        qi = q_ref[e]
        j = j_ref[e]
        for hh in range(2):
            m_old = m_ref[qi, hh, 0:1, :]
            m_new = jnp.maximum(m_old, tmax[hh])
            alpha = jnp.exp2(m_old - m_new)
            p = jnp.exp2(s_ref[hh] - m_new).astype(BF16)
            v = vt_ref[0, j, hh * V_ROWS:(hh + 1) * V_ROWS, :]
            acc_ref[qi, hh] = alpha * acc_ref[qi, hh] + _dot(v, p)
            m_ref[qi, hh, 0:1, :] = m_new

    def step(e, r, carry, additive):
        tmax_cur, tmax_nxt = carry
        tmax_new = scores(e + 2, bufs[(r + 2) % ATT_STAGES], additive)
        update(e, bufs[r], tmax_cur)
        return tmax_nxt, tmax_new

    def steps(lo, hi, carry, additive):
        n_groups = max(hi - lo, 0) // ATT_GROUP

        def group(g, carry):
            for r in range(ATT_GROUP):
                carry = step(lo + ATT_GROUP * g + r, (lo + r) % ATT_STAGES, carry, additive)
            return carry

        carry = lax.fori_loop(0, n_groups, group, carry)
        for e in range(lo + ATT_GROUP * n_groups, hi):
            carry = step(e, e % ATT_STAGES, carry, additive)
        return carry

    carry = (scores(0, bufs[0], 0 < n_add), scores(1, bufs[1], 1 < n_add))
    split = max(n_add - 2, 0)
    carry = steps(0, split, carry, True)
    carry = steps(split, n_items - 2, carry, False)
    update(n_items - 2, bufs[(n_items - 2) % ATT_STAGES], carry[0])
    update(n_items - 1, bufs[(n_items - 1) % ATT_STAGES], carry[1])

    def finish(qi, carry):
        o = [acc_ref[qi, hh, 0:HEAD_DIM, :] / acc_ref[qi, hh, HEAD_DIM:HEAD_DIM + 1, :] for hh in range(2)]
        o_ref[0, pl.ds(pl.multiple_of(qi * tq, tq), tq), :] = jnp.concatenate(o, axis=0).T.astype(BF16)
        return carry

    lax.fori_loop(0, nq, finish, 0)


def _attn_schedule(nq, tq, tk, n_near):
    blk = MOBA_BLOCK
    add, plain = [], []
    for q in range(nq):
        for j in range(q, -1, -1):
            min_dist = q * (tq // blk) - (j * (tk // blk) + tk // blk - 1)
            (add if min_dist < n_near else plain).append((q, j))
    items = add + plain
    return (jnp.asarray([q for q, _ in items], jnp.int32), jnp.asarray([j for _, j in items], jnp.int32),
            len(add), len(plain))


def _attn_call(qt, k, vt, n_pairs, bias_inputs=None):
    b, nq, _, tq = qt.shape
    _, s, _ = k.shape
    tk = vt.shape[-1]
    assert tq == tk, "the causal-triangle tile assumes equal query and key tiles"
    has_bias = bias_inputs is not None
    q_tab, j_tab, n_add, n_plain = _attn_schedule(nq, tq, tk, N_NEAR if has_bias else 1)
    assert n_add + n_plain >= 2
    smem = pl.BlockSpec(memory_space=pltpu.SMEM)
    in_specs = [
        smem, smem,
        pl.BlockSpec((1, nq, 2 * LANES, tq), lambda p, bi: (bi, 0, p, 0)),
        pl.BlockSpec((1, s, 2 * LANES), lambda p, bi: (bi, 0, p)),
        pl.BlockSpec((1, s // tk, 2 * V_ROWS, tk), lambda p, bi: (bi, 0, p, 0)),
    ]
    args = [q_tab, j_tab, qt, k, vt]
    n_tiles = (N_NEAR if has_bias else 1) + 2
    scratch = [pltpu.VMEM((nq, 2, V_ROWS, tq), F32), pltpu.VMEM((nq, 2, SUBLANES, tq), F32),
               *[pltpu.VMEM((2, tk, tq), F32) for _ in range(ATT_STAGES)],
               pltpu.VMEM((2, n_tiles, MOBA_BLOCK, MOBA_BLOCK), F32)]
    if has_bias:
        bidx, tab = bias_inputs
        in_specs += [pl.BlockSpec(bidx.shape, lambda p, bi: (0, 0, 0)), smem]
        args += [bidx, tab]
    return pl.pallas_call(
        functools.partial(_attn_kernel, has_bias=has_bias, tq=tq, tk=tk, nq=nq, n_add=n_add, n_plain=n_plain),
        grid=(n_pairs, b),
        in_specs=in_specs,
        out_specs=pl.BlockSpec((1, s, 2 * HEAD_DIM), lambda p, bi: (bi, 0, p)),
        out_shape=jax.ShapeDtypeStruct((b, s, n_pairs * 2 * HEAD_DIM), BF16),
        scratch_shapes=scratch,
        compiler_params=pltpu.CompilerParams(dimension_semantics=("arbitrary",) * 2,
                                             vmem_limit_bytes=VMEM_LIMIT),
        name="attn_bias" if has_bias else "attn",
    )(*args)


def _oproj_kernel(x_ref, oa_ref, ob_ref, oc_ref, wo_ref, g_ref, o_ref):
    merged = jnp.concatenate([oa_ref[0], ob_ref[0], oc_ref[0]], axis=1)
    o_ref[0] = x_ref[0] + _rms(_dot(merged, wo_ref[...]), g_ref[...])


def _oproj_call(x, oa, ob, oc, p, layer, tm):
    b, s, d = x.shape

    def rows(width):
        return pl.BlockSpec((1, tm, width), lambda bi, ti: (bi, ti, 0))

    return pl.pallas_call(
        _oproj_kernel,
        grid=(b, s // tm),
        in_specs=[rows(d), rows(W_MOBA), rows(W_MLA), rows(W_FOX),
                  _layer_spec(p["w_o"], layer), _layer_spec(p["ln_mix_post"], layer)],
        out_specs=rows(d),
        out_shape=jax.ShapeDtypeStruct(x.shape, F32),
        compiler_params=pltpu.CompilerParams(dimension_semantics=("arbitrary", "arbitrary"),
                                             vmem_limit_bytes=VMEM_LIMIT),
        name="oproj",
    )(x, oa, ob, oc, p["w_o"], p["ln_mix_post"])


def _ffn_kernel(x_ref, xp_ref, gpre_ref, wup_ref, cw_ref, cb_ref, wd_ref, gpost_ref, o_ref, h_ref, u_ref, *, tm):
    ti = pl.program_id(1)
    halo = FFN_HALO

    h_ref[halo:, :] = _rms(x_ref[0], gpre_ref[...]).astype(BF16)
    prev = _rms(xp_ref[0], gpre_ref[...])
    h_ref[0:halo, :] = jnp.where(ti == 0, 0.0, prev).astype(BF16)

    h = h_ref[...]
    for half in range(2):
        u_ref[half] = _dot(h, wup_ref[:, half * D_FF:(half + 1) * D_FF])

    def conv(half):
        cols = slice(half * D_FF, (half + 1) * D_FF)
        out = cb_ref[:, cols]
        for tap in range(CONV_WIDTH):
            first = halo - (CONV_WIDTH - 1) + tap
            out = out + cw_ref[tap:tap + 1, cols] * u_ref[half, first:first + tm, :]
        return out

    act = (jax.nn.gelu(conv(0), approximate=True) * conv(1)).astype(BF16)
    o_ref[0] = x_ref[0] + _rms(_dot(act, wd_ref[...]), gpost_ref[...])


def _ffn_call(x, p, layer, tm):
    b, s, d = x.shape
    halo = FFN_HALO
    blocks_per_tile = tm // halo
    weights = [p[name] for name in ("ln_ffn_pre", "w_up", "conv_w", "conv_b", "w_down", "ln_ffn_post")]
    in_specs = [
        pl.BlockSpec((1, tm, d), lambda bi, ti: (bi, ti, 0)),
        pl.BlockSpec((1, halo, d), lambda bi, ti: (bi, jnp.maximum(ti * blocks_per_tile - 1, 0), 0)),
        *[_layer_spec(w, layer) for w in weights],
    ]
    return pl.pallas_call(
        functools.partial(_ffn_kernel, tm=tm),
        grid=(b, s // tm),
        in_specs=in_specs,
        out_specs=pl.BlockSpec((1, tm, d), lambda bi, ti: (bi, ti, 0)),
        out_shape=jax.ShapeDtypeStruct(x.shape, F32),
        scratch_shapes=[pltpu.VMEM((halo + tm, d), BF16), pltpu.VMEM((2, halo + tm, D_FF), F32)],
        compiler_params=pltpu.CompilerParams(dimension_semantics=("arbitrary",) * 2,
                                             vmem_limit_bytes=VMEM_LIMIT),
        name="ffn",
    )(x, x, *weights)


def _pad_cols(w, left, total):
    return jnp.pad(w, [(0, 0)] * (w.ndim - 1) + [(left, total - left - w.shape[-1])])


def _prep_params(ln_mix_pre, ln_mix_post, ln_ffn_pre, ln_ffn_post, w_in, b_f, q_norm, kv_norm, w_uq, w_ukv, w_o,
                 w_up, conv_w, conv_b, w_down):
    depth = w_in.shape[0]
    offs = [0]
    for sz in PROJ_SIZES:
        offs.append(offs[-1] + sz)
    a_q, a_k, a_v, c_q, c_kv, k_r, f_q, f_k, f_v, f_g = [w_in[..., offs[i]:offs[i + 1]] for i in range(10)]
    half = MLA_ROPE_DIM // 2

    def rot(w):
        return jnp.concatenate([-w[..., half:], w[..., :half]], axis=-1)

    def t(w):
        return jnp.swapaxes(w, -1, -2)

    w_tok = jnp.concatenate([
        a_k, f_k, c_q, c_kv,
        _pad_cols(k_r, FEAT0, LANES), _pad_cols(rot(k_r), FEAT0, LANES), _pad_cols(f_g, 0, LANES)], axis=-1)
    uq = w_uq.reshape(depth, MLA_Q_RANK, N_HEADS_MLA, MLA_QK_DIM)
    uq_main = _pad_cols(uq, 0, LANES)
    uq_rot = _pad_cols(rot(uq[..., MLA_NOPE_DIM:]), MLA_NOPE_DIM, LANES)
    ukv = w_ukv.reshape(depth, MLA_KV_RANK, N_HEADS_MLA, MLA_NOPE_DIM + MLA_V_DIM)
    ukv_k = _pad_cols(ukv[..., :MLA_NOPE_DIM], 0, LANES)
    ukv_v = ukv[..., MLA_NOPE_DIM:]
    return {
        "ln_mix_pre": ln_mix_pre[:, None, :],
        "ln_mix_post": ln_mix_post[:, None, :],
        "ln_ffn_pre": ln_ffn_pre[:, None, :],
        "ln_ffn_post": ln_ffn_post[:, None, :],
        "w_tok": w_tok.astype(BF16),
        "w_t": t(jnp.concatenate([a_q, f_q, a_v, f_v], axis=-1)).astype(BF16),
        "b_f": _pad_cols(b_f[:, None, :], 0, LANES),
        "q_norm": q_norm[:, None, :],
        "kv_norm": kv_norm[:, None, :],
        "wuq_t": t(jnp.concatenate([uq_main.reshape(depth, MLA_Q_RANK, N_HEADS_MLA * LANES),
                                    uq_rot.reshape(depth, MLA_Q_RANK, N_HEADS_MLA * LANES)], axis=-1)).astype(BF16),
        "wukv_k": ukv_k.reshape(depth, MLA_KV_RANK, N_HEADS_MLA * LANES).astype(BF16),
        "wukv_v_t": t(ukv_v.reshape(depth, MLA_KV_RANK, W_MLA)).astype(BF16),
        "w_o": w_o.astype(BF16),
        "w_up": w_up.astype(BF16),
        "conv_w": conv_w,
        "conv_b": conv_b[:, None, :],
        "w_down": w_down.astype(BF16),
    }


def _layer_spec(a, layer):
    return pl.BlockSpec((None,) + a.shape[1:], lambda *_: (layer,) + (0,) * (a.ndim - 1))


def _rope_tables(s):
    half = MLA_ROPE_DIM // 2
    inv = ROPE_THETA ** (-jnp.arange(half, dtype=F32) / half)
    ang = jnp.arange(s).astype(F32)[:, None] * inv[None, :]
    cos = jnp.concatenate([jnp.cos(ang)] * 2, axis=1)
    sin = jnp.concatenate([jnp.sin(ang)] * 2, axis=1)
    ctok = _pad_cols(cos, FEAT0, LANES)
    stok = _pad_cols(sin, FEAT0, LANES)
    ones = jnp.ones((s, MLA_NOPE_DIM), F32)
    ct = jnp.concatenate([ones, cos, jnp.zeros((s, LANES - MLA_QK_DIM), F32)], axis=1).T
    st = stok.T
    return ctok, stok, ct, st


def _t5_bucket_tiles(tile):
    d = jnp.arange(N_NEAR)[:, None, None]
    j = jnp.arange(tile)[None, :, None]
    i = jnp.arange(tile)[None, None, :]
    n = jnp.maximum(d * tile + i - j, 0)
    exact = T5_BUCKETS // 2
    large = exact + (jnp.log(jnp.maximum(n, 1).astype(F32) / exact)
                     / math.log(T5_MAX_DIST / exact) * (T5_BUCKETS - exact)).astype(jnp.int32)
    return jnp.where(n < exact, n, jnp.minimum(large, T5_BUCKETS - 1)).astype(jnp.int32)


def kernel(x, rel_bias, ln_mix_pre, ln_mix_post, ln_ffn_pre, ln_ffn_post, w_in, b_f, q_norm, kv_norm, w_uq,
           w_ukv, w_o, w_up, conv_w, conv_b, w_down):
    b, s, d = x.shape
    depth = w_in.shape[0]
    assert s % PROJ_ROWS == 0 and s % FFN_ROWS == 0 and s % OPROJ_ROWS == 0
    assert s // MOBA_BLOCK <= NBLK_PAD
    tables = _rope_tables(s)
    bidx = _t5_bucket_tiles(MOBA_BLOCK)
    far = rel_bias[T5_BUCKETS - 1, :]
    p = _prep_params(ln_mix_pre, ln_mix_post, ln_ffn_pre, ln_ffn_post, w_in, b_f, q_norm, kv_norm, w_uq, w_ukv, w_o,
                     w_up, conv_w, conv_b, w_down)
    for l in range(depth):
        qat, ka, vat, qft, kf, vft, qmt, km, vmt = _proj_call(x, p, l, tables, far, PROJ_ROWS)
        oa = _attn_call(qat, ka, vat, N_HEADS_MOBA // 2, (bidx, rel_bias))
        ob = _attn_call(qmt, km, vmt, N_HEADS_MLA // 2)
        oc = _attn_call(qft, kf, vft, N_HEADS_FOX // 2)
        x = _oproj_call(x, oa, ob, oc, p, l, OPROJ_ROWS)
        x = _ffn_call(x, p, l, FFN_ROWS)
    return x
```

```python
import functools
import math

import jax
import jax.numpy as jnp
from jax import lax
from jax.experimental import pallas as pl
from jax.experimental.pallas import tpu as pltpu

F32 = jnp.float32
BF16 = jnp.bfloat16

HEAD_DIM = 64
N_HEADS_MOBA = 6
N_HEADS_MLA = 4
N_HEADS_FOX = 6
MOBA_BLOCK = 256
MOBA_TOPK = 3
MLA_Q_RANK = 256
MLA_KV_RANK = 128
MLA_NOPE_DIM = 64
MLA_ROPE_DIM = 32
MLA_V_DIM = 64
MLA_QK_DIM = MLA_NOPE_DIM + MLA_ROPE_DIM
ROPE_THETA = 10000.0
T5_BUCKETS = 32
T5_MAX_DIST = 1024
D_FF = 2816
CONV_WIDTH = 3
NORM_EPS = 1e-6
NEG_INF = -1e30
W_MOBA = N_HEADS_MOBA * HEAD_DIM
W_MLA = N_HEADS_MLA * MLA_V_DIM
W_FOX = N_HEADS_FOX * HEAD_DIM
PROJ_SIZES = (W_MOBA, W_MOBA, W_MOBA, MLA_Q_RANK, MLA_KV_RANK, MLA_ROPE_DIM, W_FOX, W_FOX, W_FOX, N_HEADS_FOX)

LANES = 128
SUBLANES = 8
BF16_ROWS = 16
VMEM_LIMIT = 56 * 1024 * 1024

ATT_TILE = 512
PROJ_ROWS = ATT_TILE
ATT_STAGES = 4
ATT_GROUP = 2 * ATT_STAGES
OPROJ_ROWS = 1024
FFN_ROWS = 512
FFN_HALO = BF16_ROWS
FEAT0 = HEAD_DIM
V_ROWS = HEAD_DIM + BF16_ROWS
LOG2E = math.log2(math.e)
NBLK_PAD = 16
N_NEAR = (T5_MAX_DIST + MOBA_BLOCK - 1 + MOBA_BLOCK - 1) // MOBA_BLOCK


def _ranges(sizes):
    out, start = [], 0
    for size in sizes:
        out.append((start, start + size))
        start += size
    return out


_TOK_AK, _TOK_FK, _TOK_CQ, _TOK_CKV, _TOK_KR, _TOK_KRR, _TOK_FG = _ranges(
    (W_MOBA, W_FOX, MLA_Q_RANK, MLA_KV_RANK, LANES, LANES, LANES))
_T_AQ, _T_FQ, _T_AV, _T_FV = _ranges((W_MOBA, W_FOX, W_MOBA, W_FOX))


def _rms(xf, g):
    return xf * lax.rsqrt(jnp.mean(xf * xf, axis=-1, keepdims=True) + NORM_EPS) * g


def _dot(a, b):
    return jnp.dot(a, b, preferred_element_type=F32)


def _dot_nt(a, b):
    return lax.dot_general(a, b, (((1,), (1,)), ((), ())), preferred_element_type=F32)


def _bf16_pieces(x):
    hi = x.astype(BF16).astype(F32)
    r = x - hi
    mid = r.astype(BF16).astype(F32)
    lo = (r - mid).astype(BF16).astype(F32)
    return hi, mid, lo


def _dot_3pass(a, b):
    a_hi = a.astype(BF16)
    a_lo = (a - a_hi.astype(F32)).astype(BF16)
    b_hi = b.astype(BF16)
    b_lo = (b - b_hi.astype(F32)).astype(BF16)
    return _dot(jnp.concatenate([a_hi, a_hi, a_lo], axis=1), jnp.concatenate([b_hi, b_lo, b_hi], axis=0))


def _head_slot(arr, hd):
    pair = arr[:, (hd // 2) * LANES:(hd // 2 + 1) * LANES]
    if hd % 2:
        pair = pltpu.roll(pair, HEAD_DIM, axis=1)
    return pair


def _proj_kernel(x_ref, g_ref, wtok_ref, wt_ref, bf_ref, qn_ref, kvn_ref, wuq_ref,
                 wukvk_ref, wukvv_ref, ctok_ref, stok_ref, ct_ref, st_ref, far_ref,
                 qat_ref, ka_ref, vat_ref, qft_ref, kf_ref, vft_ref, qmt_ref, km_ref, vmt_ref,
                 kmean_ref, fcarry_ref, *, tm):
    t = pl.program_id(1)
    n_sub = tm // MOBA_BLOCK

    @pl.when(t == 0)
    def _():
        kmean_ref[...] = jnp.zeros_like(kmean_ref)
        fcarry_ref[...] = jnp.zeros_like(fcarry_ref)

    h = _rms(x_ref[0], g_ref[...]).astype(BF16)
    lane = lax.broadcasted_iota(jnp.int32, (tm, LANES), 1)
    row = lax.broadcasted_iota(jnp.int32, (tm, LANES), 0)
    blk0 = t * n_sub
    is_head = lane < HEAD_DIM

    tok_all = _dot(h, wtok_ref[...])
    t_all = _dot_nt(wt_ref[...], h)

    def tok(rng):
        return tok_all[:, rng[0]:rng[1]]

    def tr(rng):
        return t_all[rng[0]:rng[1], :]

    sub8v = lax.broadcasted_iota(jnp.int32, (SUBLANES, tm), 0)
    ones_row = jnp.where(sub8v == 0, 1.0, 0.0)
    v_pad = jnp.zeros((V_ROWS - HEAD_DIM - SUBLANES, tm), F32)

    def store_vt(ref, vt):
        n_heads = vt.shape[0] // HEAD_DIM
        ref[0, 0] = jnp.concatenate(
            [piece for hd in range(n_heads)
             for piece in (vt[hd * HEAD_DIM:(hd + 1) * HEAD_DIM, :], ones_row, v_pad)], axis=0).astype(BF16)

    ak = tok(_TOK_AK)
    blk_row = blk0 + lax.shift_right_logical(row, int(math.log2(MOBA_BLOCK)))
    k_feat = jnp.where(lane == FEAT0 + blk_row, 1.0, 0.0)
    for hd in range(N_HEADS_MOBA):
        ka_ref[0, :, hd * LANES:(hd + 1) * LANES] = jnp.where(is_head, _head_slot(ak, hd), k_feat).astype(BF16)

    lane_k = lax.broadcasted_iota(jnp.int32, (1, W_MOBA), 1)
    for bi in range(n_sub):
        mean_row = jnp.mean(ak[bi * MOBA_BLOCK:(bi + 1) * MOBA_BLOCK, :], axis=0, keepdims=True)
        for hd in range(N_HEADS_MOBA):
            in_head = (lane_k >= hd * HEAD_DIM) & (lane_k < (hd + 1) * HEAD_DIM)
            kmean_ref[pl.ds(hd * NBLK_PAD + blk0 + bi, 1), :] = jnp.where(in_head, mean_row, 0.0)

    qat = tr(_T_AQ) * (HEAD_DIM ** -0.5 * LOG2E)
    gate = _dot_3pass(kmean_ref[...], qat)
    own = blk0 + lax.shift_right_logical(
        lax.broadcasted_iota(jnp.int32, (NBLK_PAD, tm), 1), int(math.log2(MOBA_BLOCK)))
    n_io = lax.broadcasted_iota(jnp.int32, (NBLK_PAD, tm), 0)
    past = n_io < own
    zeros_tail = jnp.zeros((LANES - HEAD_DIM - NBLK_PAD, tm), F32)
    for hd in range(N_HEADS_MOBA):
        g = jnp.where(past, gate[hd * NBLK_PAD:(hd + 1) * NBLK_PAD, :], NEG_INF)
        rank = jnp.zeros((NBLK_PAD, tm), jnp.int32)
        for n2 in range(NBLK_PAD):
            r = g[n2:n2 + 1, :]
            tie = jnp.where(n2 < n_io, 1, 0)
            rank = rank + jnp.where(r > g, 1, jnp.where(r == g, tie, 0))
        sel = jnp.where(past, rank, MOBA_TOPK) < MOBA_TOPK
        far = jnp.where(own - n_io >= N_NEAR, far_ref[hd] * LOG2E, 0.0)
        q_feat = jnp.where(sel | (n_io == own), far, NEG_INF)
        qat_ref[0, 0, hd * LANES:(hd + 1) * LANES, :] = jnp.concatenate(
            [qat[hd * HEAD_DIM:(hd + 1) * HEAD_DIM, :], q_feat, zeros_tail], axis=0).astype(BF16)
    store_vt(vat_ref, tr(_T_AV))

    fg = tok(_TOK_FG) + bf_ref[...]
    logf = jnp.minimum(fg, 0.0) - jnp.log1p(jnp.exp(-jnp.abs(fg)))
    csum = jnp.where(lane < N_HEADS_FOX, logf, 0.0)
    sft = 1
    while sft < tm:
        csum = csum + jnp.where(row >= sft, pltpu.roll(csum, sft, axis=0), 0.0)
        sft *= 2
    decay = csum + fcarry_ref[0:1, :]
    fcarry_ref[0:1, :] = decay[tm - 1:tm, :]
    decay = decay * LOG2E
    decay_t = decay.T

    fk = tok(_TOK_FK)
    fqt = tr(_T_FQ) * (HEAD_DIM ** -0.5 * LOG2E)
    sub8 = lax.broadcasted_iota(jnp.int32, (SUBLANES, tm), 0)
    zeros_tail_f = jnp.zeros((LANES - HEAD_DIM - SUBLANES, tm), F32)
    for hd in range(N_HEADS_FOX):
        fcol = jnp.broadcast_to(decay[:, hd:hd + 1], (tm, LANES))
        hi, mid, lo = _bf16_pieces(fcol)
        k_feat_f = jnp.where(lane < FEAT0 + 3, 1.0,
                             jnp.where(lane == FEAT0 + 3, -hi,
                                       jnp.where(lane == FEAT0 + 4, -mid,
                                                 jnp.where(lane == FEAT0 + 5, -lo, 0.0))))
        kf_ref[0, :, hd * LANES:(hd + 1) * LANES] = jnp.where(is_head, _head_slot(fk, hd), k_feat_f).astype(BF16)
        hi, mid, lo = _bf16_pieces(decay_t[hd:hd + 1, :])
        q_feat_f = jnp.where(sub8 == 0, hi,
                             jnp.where(sub8 == 1, mid,
                                       jnp.where(sub8 == 2, lo,
                                                 jnp.where(sub8 < 6, 1.0, 0.0))))
        qft_ref[0, 0, hd * LANES:(hd + 1) * LANES, :] = jnp.concatenate(
            [fqt[hd * HEAD_DIM:(hd + 1) * HEAD_DIM, :], q_feat_f, zeros_tail_f], axis=0).astype(BF16)
    store_vt(vft_ref, tr(_T_FV))

    cqn = _rms(tok(_TOK_CQ), qn_ref[...]).astype(BF16)
    qm_all = _dot_nt(wuq_ref[...], cqn)
    qm = qm_all[0:N_HEADS_MLA * LANES, :]
    qmr = qm_all[N_HEADS_MLA * LANES:, :]
    kvn = _rms(tok(_TOK_CKV), kvn_ref[...]).astype(BF16)
    k_nope = _dot(kvn, wukvk_ref[...])
    k_rope = tok(_TOK_KR) * ctok_ref[...] + tok(_TOK_KRR) * stok_ref[...]
    cos_t = ct_ref[...]
    sin_t = st_ref[...]
    for hd in range(N_HEADS_MLA):
        sl = slice(hd * LANES, (hd + 1) * LANES)
        qmt_ref[0, 0, sl, :] = ((qm[sl, :] * cos_t + qmr[sl, :] * sin_t) * (MLA_QK_DIM ** -0.5 * LOG2E)).astype(BF16)
        km_ref[0, :, sl] = (k_nope[:, sl] + k_rope).astype(BF16)
    store_vt(vmt_ref, _dot_nt(wukvv_ref[...], kvn))


def _proj_call(x, p, layer, tables, far, tm):
    b, s, d = x.shape
    nt = s // tm
    weights = [p[name] for name in ("ln_mix_pre", "w_tok", "w_t", "b_f", "q_norm", "kv_norm", "wuq_t", "wukv_k",
                                    "wukv_v_t")]
    ctok, stok, ct, st = tables
    in_specs = [
        pl.BlockSpec((1, tm, d), lambda bi, ti: (bi, ti, 0)),
        *[_layer_spec(w, layer) for w in weights],
        pl.BlockSpec((tm, LANES), lambda bi, ti: (ti, 0)),
        pl.BlockSpec((tm, LANES), lambda bi, ti: (ti, 0)),
        pl.BlockSpec((LANES, tm), lambda bi, ti: (0, ti)),
        pl.BlockSpec((LANES, tm), lambda bi, ti: (0, ti)),
        pl.BlockSpec(memory_space=pltpu.SMEM),
    ]

    def qt_spec(n_heads):
        return pl.BlockSpec((1, 1, n_heads * LANES, tm), lambda bi, ti: (bi, ti, 0, 0))

    def k_spec(n_heads):
        return pl.BlockSpec((1, tm, n_heads * LANES), lambda bi, ti: (bi, ti, 0))

    def vt_spec(width):
        return pl.BlockSpec((1, 1, width, tm), lambda bi, ti: (bi, ti, 0, 0))

    def qt_shape(n_heads):
        return jax.ShapeDtypeStruct((b, nt, n_heads * LANES, tm), BF16)

    def k_shape(n_heads):
        return jax.ShapeDtypeStruct((b, s, n_heads * LANES), BF16)

    def vt_shape(width):
        return jax.ShapeDtypeStruct((b, nt, width, tm), BF16)

    heads = (N_HEADS_MOBA, N_HEADS_FOX, N_HEADS_MLA)
    out_specs = [spec for n in heads for spec in (qt_spec(n), k_spec(n), vt_spec(n * V_ROWS))]
    out_shape = [shp for n in heads for shp in (qt_shape(n), k_shape(n), vt_shape(n * V_ROWS))]
    return pl.pallas_call(
        functools.partial(_proj_kernel, tm=tm),
        grid=(b, nt),
        in_specs=in_specs,
        out_specs=out_specs,
        out_shape=out_shape,
        scratch_shapes=[pltpu.VMEM((N_HEADS_MOBA * NBLK_PAD, W_MOBA), F32),
                        pltpu.VMEM((SUBLANES, LANES), F32)],
        compiler_params=pltpu.CompilerParams(dimension_semantics=("arbitrary", "arbitrary"),
                                             vmem_limit_bytes=VMEM_LIMIT),
        name="proj",
    )(x, *weights, ctok, stok, ct, st, far)


def _attn_kernel(*refs, has_bias, tq, tk, nq, n_add, n_plain):
    if has_bias:
        (q_ref, j_ref, qt_ref, k_ref, vt_ref, bidx_ref, tab_ref, o_ref,
         acc_ref, m_ref, *bufs, bias_ref) = refs
    else:
        (q_ref, j_ref, qt_ref, k_ref, vt_ref, o_ref,
         acc_ref, m_ref, *bufs, bias_ref) = refs
    hp = pl.program_id(0)
    bi = pl.program_id(1)
    blk = MOBA_BLOCK
    n_near = N_NEAR if has_bias else 1

    @pl.when(bi == 0)
    def _():
        tri = (lax.broadcasted_iota(jnp.int32, (blk, blk), 0) <= lax.broadcasted_iota(jnp.int32, (blk, blk), 1))
        for hh in range(2):
            bias_ref[hh, 0] = jnp.full((blk, blk), NEG_INF, F32)
            bias_ref[hh, n_near + 1] = jnp.zeros((blk, blk), F32)
            if has_bias:
                def build(d, carry, hh=hh):
                    idx = bidx_ref[d]
                    bias_ref[hh, d + 1] = lax.fori_loop(
                        0, T5_BUCKETS,
                        lambda bk, tl: jnp.where(idx == bk, tab_ref[bk, hp * 2 + hh] * LOG2E, tl),
                        jnp.zeros((blk, blk), F32))
                    return carry
                lax.fori_loop(0, n_near, build, 0)
                bias_ref[hh, 1] = jnp.where(tri, bias_ref[hh, 1], NEG_INF)
            else:
                bias_ref[hh, 1] = jnp.where(tri, 0.0, NEG_INF)

    acc_ref[...] = jnp.zeros_like(acc_ref)
    m_ref[...] = jnp.full(m_ref.shape, NEG_INF, F32)

    n_items = n_add + n_plain

    def scores(e, s_ref, additive):
        qi = q_ref[e]
        j = j_ref[e]
        start = pl.multiple_of(j * tk, tk)
        tmax = []
        for hh in range(2):
            k = k_ref[0, pl.ds(start, tk), hh * LANES:(hh + 1) * LANES]
            s = _dot(k, qt_ref[0, qi, hh * LANES:(hh + 1) * LANES, :])
            if additive:
                s = s + jnp.concatenate([
                    jnp.concatenate([
                        bias_ref[hh, jnp.clip((qi * (tq // blk) + c) - (j * (tk // blk) + r), -1, n_near) + 1]
                        for c in range(tq // blk)], axis=1)
                    for r in range(tk // blk)], axis=0)
            s_ref[hh] = s
            tmax.append(jnp.max(s, axis=0, keepdims=True))
        return tuple(tmax)

    def update(e, s_ref, tmax):
        qi = q_ref[e]
        j = j_ref[e]
        for hh in range(2):
            m_old = m_ref[qi, hh, 0:1, :]
            m_new = jnp.maximum(m_old, tmax[hh])
            alpha = jnp.exp2(m_old - m_new)
            p = jnp.exp2(s_ref[hh] - m_new).astype(BF16)
            v = vt_ref[0, j, hh * V_ROWS:(hh + 1) * V_ROWS, :]
            acc_ref[qi, hh] = alpha * acc_ref[qi, hh] + _dot(v, p)
            m_ref[qi, hh, 0:1, :] = m_new

    def step(e, r, carry, additive):
        tmax_cur, tmax_nxt = carry
        tmax_new = scores(e + 2, bufs[(r + 2) % ATT_STAGES], additive)
        update(e, bufs[r], tmax_cur)
        return tmax_nxt, tmax_new

    def steps(lo, hi, carry, additive):
        n_groups = max(hi - lo, 0) // ATT_GROUP

        def group(g, carry):
            for r in range(ATT_GROUP):
                carry = step(lo + ATT_GROUP * g + r, (lo + r) % ATT_STAGES, carry, additive)
            return carry

        carry = lax.fori_loop(0, n_groups, group, carry)
        for e in range(lo + ATT_GROUP * n_groups, hi):
            carry = step(e, e % ATT_STAGES, carry, additive)
        return carry

    carry = (scores(0, bufs[0], 0 < n_add), scores(1, bufs[1], 1 < n_add))
    split = max(n_add - 2, 0)
    carry = steps(0, split, carry, True)
    carry = steps(split, n_items - 2, carry, False)
    update(n_items - 2, bufs[(n_items - 2) % ATT_STAGES], carry[0])
    update(n_items - 1, bufs[(n_items - 1) % ATT_STAGES], carry[1])

    for qi in range(nq):
        o = [acc_ref[qi, hh, 0:HEAD_DIM, :] / acc_ref[qi, hh, HEAD_DIM:HEAD_DIM + 1, :] for hh in range(2)]
        o_ref[0, qi * tq:(qi + 1) * tq, :] = jnp.concatenate(o, axis=0).T.astype(BF16)


def _attn_schedule(nq, tq, tk, n_near):
    blk = MOBA_BLOCK
    add, plain = [], []
    for q in range(nq):
        for j in range(q, -1, -1):
            min_dist = q * (tq // blk) - (j * (tk // blk) + tk // blk - 1)
            (add if min_dist < n_near else plain).append((q, j))
    items = add + plain
    return (jnp.asarray([q for q, _ in items], jnp.int32), jnp.asarray([j for _, j in items], jnp.int32),
            len(add), len(plain))


def _attn_call(qt, k, vt, n_pairs, bias_inputs=None):
    b, nq, _, tq = qt.shape
    _, s, _ = k.shape
    tk = vt.shape[-1]
    assert tq == tk, "the causal-triangle tile assumes equal query and key tiles"
    has_bias = bias_inputs is not None
    q_tab, j_tab, n_add, n_plain = _attn_schedule(nq, tq, tk, N_NEAR if has_bias else 1)
    assert n_add + n_plain >= 2
    smem = pl.BlockSpec(memory_space=pltpu.SMEM)
    in_specs = [
        smem, smem,
        pl.BlockSpec((1, nq, 2 * LANES, tq), lambda p, bi: (bi, 0, p, 0)),
        pl.BlockSpec((1, s, 2 * LANES), lambda p, bi: (bi, 0, p)),
        pl.BlockSpec((1, s // tk, 2 * V_ROWS, tk), lambda p, bi: (bi, 0, p, 0)),
    ]
    args = [q_tab, j_tab, qt, k, vt]
    n_tiles = (N_NEAR if has_bias else 1) + 2
    scratch = [pltpu.VMEM((nq, 2, V_ROWS, tq), F32), pltpu.VMEM((nq, 2, SUBLANES, tq), F32),
               *[pltpu.VMEM((2, tk, tq), F32) for _ in range(ATT_STAGES)],
               pltpu.VMEM((2, n_tiles, MOBA_BLOCK, MOBA_BLOCK), F32)]
    if has_bias:
        bidx, tab = bias_inputs
        in_specs += [pl.BlockSpec(bidx.shape, lambda p, bi: (0, 0, 0)), smem]
        args += [bidx, tab]
    return pl.pallas_call(
        functools.partial(_attn_kernel, has_bias=has_bias, tq=tq, tk=tk, nq=nq, n_add=n_add, n_plain=n_plain),
        grid=(n_pairs, b),
        in_specs=in_specs,
        out_specs=pl.BlockSpec((1, s, 2 * HEAD_DIM), lambda p, bi: (bi, 0, p)),
        out_shape=jax.ShapeDtypeStruct((b, s, n_pairs * 2 * HEAD_DIM), BF16),
        scratch_shapes=scratch,
        compiler_params=pltpu.CompilerParams(dimension_semantics=("arbitrary",) * 2,
                                             vmem_limit_bytes=VMEM_LIMIT),
        name="attn_bias" if has_bias else "attn",
    )(*args)


def _oproj_kernel(x_ref, oa_ref, ob_ref, oc_ref, wo_ref, g_ref, o_ref):
    merged = jnp.concatenate([oa_ref[0], ob_ref[0], oc_ref[0]], axis=1)
    o_ref[0] = x_ref[0] + _rms(_dot(merged, wo_ref[...]), g_ref[...])


def _oproj_call(x, oa, ob, oc, p, layer, tm):
    b, s, d = x.shape

    def rows(width):
        return pl.BlockSpec((1, tm, width), lambda bi, ti: (bi, ti, 0))

    return pl.pallas_call(
        _oproj_kernel,
        grid=(b, s // tm),
        in_specs=[rows(d), rows(W_MOBA), rows(W_MLA), rows(W_FOX),
                  _layer_spec(p["w_o"], layer), _layer_spec(p["ln_mix_post"], layer)],
        out_specs=rows(d),
        out_shape=jax.ShapeDtypeStruct(x.shape, F32),
        compiler_params=pltpu.CompilerParams(dimension_semantics=("arbitrary", "arbitrary"),
                                             vmem_limit_bytes=VMEM_LIMIT),
        name="oproj",
    )(x, oa, ob, oc, p["w_o"], p["ln_mix_post"])


def _ffn_kernel(x_ref, xp_ref, gpre_ref, wup_ref, cw_ref, cb_ref, wd_ref, gpost_ref, o_ref, h_ref, u_ref, *, tm):
    ti = pl.program_id(1)
    halo = FFN_HALO

    h_ref[halo:, :] = _rms(x_ref[0], gpre_ref[...]).astype(BF16)
    prev = _rms(xp_ref[0], gpre_ref[...])
    h_ref[0:halo, :] = jnp.where(ti == 0, 0.0, prev).astype(BF16)

    h = h_ref[...]
    for half in range(2):
        u_ref[half] = _dot(h, wup_ref[:, half * D_FF:(half + 1) * D_FF])

    def conv(half):
        cols = slice(half * D_FF, (half + 1) * D_FF)
        out = cb_ref[:, cols]
        for tap in range(CONV_WIDTH):
            first = halo - (CONV_WIDTH - 1) + tap
            out = out + cw_ref[tap:tap + 1, cols] * u_ref[half, first:first + tm, :]
        return out

    act = (jax.nn.gelu(conv(0), approximate=True) * conv(1)).astype(BF16)
    o_ref[0] = x_ref[0] + _rms(_dot(act, wd_ref[...]), gpost_ref[...])


def _ffn_call(x, p, layer, tm):
    b, s, d = x.shape
    halo = FFN_HALO
    blocks_per_tile = tm // halo
    weights = [p[name] for name in ("ln_ffn_pre", "w_up", "conv_w", "conv_b", "w_down", "ln_ffn_post")]
    in_specs = [
        pl.BlockSpec((1, tm, d), lambda bi, ti: (bi, ti, 0)),
        pl.BlockSpec((1, halo, d), lambda bi, ti: (bi, jnp.maximum(ti * blocks_per_tile - 1, 0), 0)),
        *[_layer_spec(w, layer) for w in weights],
    ]
    return pl.pallas_call(
        functools.partial(_ffn_kernel, tm=tm),
        grid=(b, s // tm),
        in_specs=in_specs,
        out_specs=pl.BlockSpec((1, tm, d), lambda bi, ti: (bi, ti, 0)),
        out_shape=jax.ShapeDtypeStruct(x.shape, F32),
        scratch_shapes=[pltpu.VMEM((halo + tm, d), BF16), pltpu.VMEM((2, halo + tm, D_FF), F32)],
        compiler_params=pltpu.CompilerParams(dimension_semantics=("arbitrary",) * 2,
                                             vmem_limit_bytes=VMEM_LIMIT),
        name="ffn",
    )(x, x, *weights)


def _pad_cols(w, left, total):
    return jnp.pad(w, [(0, 0)] * (w.ndim - 1) + [(left, total - left - w.shape[-1])])


def _prep_params(ln_mix_pre, ln_mix_post, ln_ffn_pre, ln_ffn_post, w_in, b_f, q_norm, kv_norm, w_uq, w_ukv, w_o,
                 w_up, conv_w, conv_b, w_down):
    depth = w_in.shape[0]
    offs = [0]
    for sz in PROJ_SIZES:
        offs.append(offs[-1] + sz)
    a_q, a_k, a_v, c_q, c_kv, k_r, f_q, f_k, f_v, f_g = [w_in[..., offs[i]:offs[i + 1]] for i in range(10)]
    half = MLA_ROPE_DIM // 2

    def rot(w):
        return jnp.concatenate([-w[..., half:], w[..., :half]], axis=-1)

    def t(w):
        return jnp.swapaxes(w, -1, -2)

    w_tok = jnp.concatenate([
        a_k, f_k, c_q, c_kv,
        _pad_cols(k_r, FEAT0, LANES), _pad_cols(rot(k_r), FEAT0, LANES), _pad_cols(f_g, 0, LANES)], axis=-1)
    uq = w_uq.reshape(depth, MLA_Q_RANK, N_HEADS_MLA, MLA_QK_DIM)
    uq_main = _pad_cols(uq, 0, LANES)
    uq_rot = _pad_cols(rot(uq[..., MLA_NOPE_DIM:]), MLA_NOPE_DIM, LANES)
    ukv = w_ukv.reshape(depth, MLA_KV_RANK, N_HEADS_MLA, MLA_NOPE_DIM + MLA_V_DIM)
    ukv_k = _pad_cols(ukv[..., :MLA_NOPE_DIM], 0, LANES)
    ukv_v = ukv[..., MLA_NOPE_DIM:]
    return {
        "ln_mix_pre": ln_mix_pre[:, None, :],
        "ln_mix_post": ln_mix_post[:, None, :],
        "ln_ffn_pre": ln_ffn_pre[:, None, :],
        "ln_ffn_post": ln_ffn_post[:, None, :],
        "w_tok": w_tok.astype(BF16),
        "w_t": t(jnp.concatenate([a_q, f_q, a_v, f_v], axis=-1)).astype(BF16),
        "b_f": _pad_cols(b_f[:, None, :], 0, LANES),
        "q_norm": q_norm[:, None, :],
        "kv_norm": kv_norm[:, None, :],
        "wuq_t": t(jnp.concatenate([uq_main.reshape(depth, MLA_Q_RANK, N_HEADS_MLA * LANES),
                                    uq_rot.reshape(depth, MLA_Q_RANK, N_HEADS_MLA * LANES)], axis=-1)).astype(BF16),
        "wukv_k": ukv_k.reshape(depth, MLA_KV_RANK, N_HEADS_MLA * LANES).astype(BF16),
        "wukv_v_t": t(ukv_v.reshape(depth, MLA_KV_RANK, W_MLA)).astype(BF16),
        "w_o": w_o.astype(BF16),
        "w_up": w_up.astype(BF16),
        "conv_w": conv_w,
        "conv_b": conv_b[:, None, :],
        "w_down": w_down.astype(BF16),
    }


def _layer_spec(a, layer):
    return pl.BlockSpec((None,) + a.shape[1:], lambda *_: (layer,) + (0,) * (a.ndim - 1))


def _rope_tables(s):
    half = MLA_ROPE_DIM // 2
    inv = ROPE_THETA ** (-jnp.arange(half, dtype=F32) / half)
    ang = jnp.arange(s).astype(F32)[:, None] * inv[None, :]
    cos = jnp.concatenate([jnp.cos(ang)] * 2, axis=1)
    sin = jnp.concatenate([jnp.sin(ang)] * 2, axis=1)
    ctok = _pad_cols(cos, FEAT0, LANES)
    stok = _pad_cols(sin, FEAT0, LANES)
    ones = jnp.ones((s, MLA_NOPE_DIM), F32)
    ct = jnp.concatenate([ones, cos, jnp.zeros((s, LANES - MLA_QK_DIM), F32)], axis=1).T
    st = stok.T
    return ctok, stok, ct, st


def _t5_bucket_tiles(tile):
    d = jnp.arange(N_NEAR)[:, None, None]
    j = jnp.arange(tile)[None, :, None]
    i = jnp.arange(tile)[None, None, :]
    n = jnp.maximum(d * tile + i - j, 0)
    exact = T5_BUCKETS // 2
    large = exact + (jnp.log(jnp.maximum(n, 1).astype(F32) / exact)
                     / math.log(T5_MAX_DIST / exact) * (T5_BUCKETS - exact)).astype(jnp.int32)
    return jnp.where(n < exact, n, jnp.minimum(large, T5_BUCKETS - 1)).astype(jnp.int32)


def kernel(x, rel_bias, ln_mix_pre, ln_mix_post, ln_ffn_pre, ln_ffn_post, w_in, b_f, q_norm, kv_norm, w_uq,
           w_ukv, w_o, w_up, conv_w, conv_b, w_down):
    b, s, d = x.shape
    depth = w_in.shape[0]
    assert s % PROJ_ROWS == 0 and s % FFN_ROWS == 0 and s % OPROJ_ROWS == 0
    assert s // MOBA_BLOCK <= NBLK_PAD
    tables = _rope_tables(s)
    bidx = _t5_bucket_tiles(MOBA_BLOCK)
    far = rel_bias[T5_BUCKETS - 1, :]
    p = _prep_params(ln_mix_pre, ln_mix_post, ln_ffn_pre, ln_ffn_post, w_in, b_f, q_norm, kv_norm, w_uq, w_ukv, w_o,
                     w_up, conv_w, conv_b, w_down)
    for l in range(depth):
        qat, ka, vat, qft, kf, vft, qmt, km, vmt = _proj_call(x, p, l, tables, far, PROJ_ROWS)
        oa = _attn_call(qat, ka, vat, N_HEADS_MOBA // 2, (bidx, rel_bias))
        ob = _attn_call(qmt, km, vmt, N_HEADS_MLA // 2)
        oc = _attn_call(qft, kf, vft, N_HEADS_FOX // 2)
        x = _oproj_call(x, oa, ob, oc, p, l, OPROJ_ROWS)
        x = _ffn_call(x, p, l, FFN_ROWS)
    return x
```

```python
import functools
import math

import jax
import jax.numpy as jnp
from jax import lax
from jax.experimental import pallas as pl
from jax.experimental.pallas import tpu as pltpu

F32 = jnp.float32
BF16 = jnp.bfloat16

HEAD_DIM = 64
N_HEADS_MOBA = 6
N_HEADS_MLA = 4
N_HEADS_FOX = 6
MOBA_BLOCK = 256
MOBA_TOPK = 3
MLA_Q_RANK = 256
MLA_KV_RANK = 128
MLA_NOPE_DIM = 64
MLA_ROPE_DIM = 32
MLA_V_DIM = 64
MLA_QK_DIM = MLA_NOPE_DIM + MLA_ROPE_DIM
ROPE_THETA = 10000.0
T5_BUCKETS = 32
T5_MAX_DIST = 1024
D_FF = 2816
CONV_WIDTH = 3
NORM_EPS = 1e-6
NEG_INF = -1e30
W_MOBA = N_HEADS_MOBA * HEAD_DIM
W_MLA = N_HEADS_MLA * MLA_V_DIM
W_FOX = N_HEADS_FOX * HEAD_DIM
PROJ_SIZES = (W_MOBA, W_MOBA, W_MOBA, MLA_Q_RANK, MLA_KV_RANK, MLA_ROPE_DIM, W_FOX, W_FOX, W_FOX, N_HEADS_FOX)

LANES = 128
SUBLANES = 8
BF16_ROWS = 16
VMEM_LIMIT = 56 * 1024 * 1024

ATT_TILE = 256
PROJ_ROWS = 2 * ATT_TILE
ATT_STAGES = 4
ATT_GROUP = 8 * ATT_STAGES
OPROJ_ROWS = 1024
FFN_ROWS = 512
FFN_HALO = BF16_ROWS
FEAT0 = HEAD_DIM
V_ROWS = HEAD_DIM + BF16_ROWS
LOG2E = math.log2(math.e)
NBLK_PAD = 16
N_NEAR = (T5_MAX_DIST + MOBA_BLOCK - 1 + MOBA_BLOCK - 1) // MOBA_BLOCK


def _ranges(sizes):
    out, start = [], 0
    for size in sizes:
        out.append((start, start + size))
        start += size
    return out


_TOK_AK, _TOK_FK, _TOK_CQ, _TOK_CKV, _TOK_KR, _TOK_KRR, _TOK_FG = _ranges(
    (W_MOBA, W_FOX, MLA_Q_RANK, MLA_KV_RANK, LANES, LANES, LANES))
_T_AQ, _T_FQ, _T_AV, _T_FV = _ranges((W_MOBA, W_FOX, W_MOBA, W_FOX))


def _rms(xf, g):
    return xf * lax.rsqrt(jnp.mean(xf * xf, axis=-1, keepdims=True) + NORM_EPS) * g


def _dot(a, b):
    return jnp.dot(a, b, preferred_element_type=F32)


def _dot_nt(a, b):
    return lax.dot_general(a, b, (((1,), (1,)), ((), ())), preferred_element_type=F32)


def _bf16_pieces(x):
    hi = x.astype(BF16).astype(F32)
    r = x - hi
    mid = r.astype(BF16).astype(F32)
    lo = (r - mid).astype(BF16).astype(F32)
    return hi, mid, lo


def _dot_3pass(a, b):
    a_hi = a.astype(BF16)
    a_lo = (a - a_hi.astype(F32)).astype(BF16)
    b_hi = b.astype(BF16)
    b_lo = (b - b_hi.astype(F32)).astype(BF16)
    return _dot(jnp.concatenate([a_hi, a_hi, a_lo], axis=1), jnp.concatenate([b_hi, b_lo, b_hi], axis=0))


def _head_slot(arr, hd):
    pair = arr[:, (hd // 2) * LANES:(hd // 2 + 1) * LANES]
    if hd % 2:
        pair = pltpu.roll(pair, HEAD_DIM, axis=1)
    return pair


def _proj_kernel(x_ref, g_ref, wtok_ref, wt_ref, bf_ref, qn_ref, kvn_ref, wuq_ref,
                 wukvk_ref, wukvv_ref, ctok_ref, stok_ref, ct_ref, st_ref, far_ref,
                 qat_ref, ka_ref, vat_ref, qft_ref, kf_ref, vft_ref, qmt_ref, km_ref, vmt_ref,
                 kmean_ref, fcarry_ref, *, tm):
    t = pl.program_id(1)
    n_sub = tm // MOBA_BLOCK

    @pl.when(t == 0)
    def _():
        kmean_ref[...] = jnp.zeros_like(kmean_ref)
        fcarry_ref[...] = jnp.zeros_like(fcarry_ref)

    h = _rms(x_ref[0], g_ref[...]).astype(BF16)
    lane = lax.broadcasted_iota(jnp.int32, (tm, LANES), 1)
    row = lax.broadcasted_iota(jnp.int32, (tm, LANES), 0)
    blk0 = t * n_sub
    is_head = lane < HEAD_DIM

    tok_all = _dot(h, wtok_ref[...])
    t_all = _dot_nt(wt_ref[...], h)

    def tok(rng):
        return tok_all[:, rng[0]:rng[1]]

    def tr(rng):
        return t_all[rng[0]:rng[1], :]

    sub8v = lax.broadcasted_iota(jnp.int32, (SUBLANES, tm), 0)
    ones_row = jnp.where(sub8v == 0, 1.0, 0.0)
    v_pad = jnp.zeros((V_ROWS - HEAD_DIM - SUBLANES, tm), F32)

    def store_slabs(ref, rows, val):
        for si in range(tm // ATT_TILE):
            ref[0, si, rows, :] = val[:, si * ATT_TILE:(si + 1) * ATT_TILE]

    def store_vt(ref, vt):
        n_heads = vt.shape[0] // HEAD_DIM
        store_slabs(ref, slice(None), jnp.concatenate(
            [piece for hd in range(n_heads)
             for piece in (vt[hd * HEAD_DIM:(hd + 1) * HEAD_DIM, :], ones_row, v_pad)], axis=0).astype(BF16))

    ak = tok(_TOK_AK)
    blk_row = blk0 + lax.shift_right_logical(row, int(math.log2(MOBA_BLOCK)))
    k_feat = jnp.where(lane == FEAT0 + blk_row, 1.0, 0.0)
    for hd in range(N_HEADS_MOBA):
        ka_ref[0, :, hd * LANES:(hd + 1) * LANES] = jnp.where(is_head, _head_slot(ak, hd), k_feat).astype(BF16)

    lane_k = lax.broadcasted_iota(jnp.int32, (1, W_MOBA), 1)
    for bi in range(n_sub):
        mean_row = jnp.mean(ak[bi * MOBA_BLOCK:(bi + 1) * MOBA_BLOCK, :], axis=0, keepdims=True)
        for hd in range(N_HEADS_MOBA):
            in_head = (lane_k >= hd * HEAD_DIM) & (lane_k < (hd + 1) * HEAD_DIM)
            kmean_ref[pl.ds(hd * NBLK_PAD + blk0 + bi, 1), :] = jnp.where(in_head, mean_row, 0.0)

    qat = tr(_T_AQ) * (HEAD_DIM ** -0.5 * LOG2E)
    gate = _dot_3pass(kmean_ref[...], qat)
    own = blk0 + lax.shift_right_logical(
        lax.broadcasted_iota(jnp.int32, (NBLK_PAD, tm), 1), int(math.log2(MOBA_BLOCK)))
    n_io = lax.broadcasted_iota(jnp.int32, (NBLK_PAD, tm), 0)
    past = n_io < own
    zeros_tail = jnp.zeros((LANES - HEAD_DIM - NBLK_PAD, tm), F32)
    for hd in range(N_HEADS_MOBA):
        g = jnp.where(past, gate[hd * NBLK_PAD:(hd + 1) * NBLK_PAD, :], NEG_INF)
        rank = jnp.zeros((NBLK_PAD, tm), jnp.int32)
        for n2 in range(NBLK_PAD):
            r = g[n2:n2 + 1, :]
            tie = jnp.where(n2 < n_io, 1, 0)
            rank = rank + jnp.where(r > g, 1, jnp.where(r == g, tie, 0))
        sel = jnp.where(past, rank, MOBA_TOPK) < MOBA_TOPK
        far = jnp.where(own - n_io >= N_NEAR, far_ref[hd] * LOG2E, 0.0)
        q_feat = jnp.where(sel | (n_io == own), far, NEG_INF)
        store_slabs(qat_ref, slice(hd * LANES, (hd + 1) * LANES), jnp.concatenate(
            [qat[hd * HEAD_DIM:(hd + 1) * HEAD_DIM, :], q_feat, zeros_tail], axis=0).astype(BF16))
    store_vt(vat_ref, tr(_T_AV))

    fg = tok(_TOK_FG) + bf_ref[...]
    logf = jnp.minimum(fg, 0.0) - jnp.log1p(jnp.exp(-jnp.abs(fg)))
    csum = jnp.where(lane < N_HEADS_FOX, logf, 0.0)
    sft = 1
    while sft < tm:
        csum = csum + jnp.where(row >= sft, pltpu.roll(csum, sft, axis=0), 0.0)
        sft *= 2
    decay = csum + fcarry_ref[0:1, :]
    fcarry_ref[0:1, :] = decay[tm - 1:tm, :]
    decay = decay * LOG2E
    decay_t = decay.T

    fk = tok(_TOK_FK)
    fqt = tr(_T_FQ) * (HEAD_DIM ** -0.5 * LOG2E)
    sub8 = lax.broadcasted_iota(jnp.int32, (SUBLANES, tm), 0)
    zeros_tail_f = jnp.zeros((LANES - HEAD_DIM - SUBLANES, tm), F32)
    for hd in range(N_HEADS_FOX):
        fcol = jnp.broadcast_to(decay[:, hd:hd + 1], (tm, LANES))
        hi, mid, lo = _bf16_pieces(fcol)
        k_feat_f = jnp.where(lane < FEAT0 + 3, 1.0,
                             jnp.where(lane == FEAT0 + 3, -hi,
                                       jnp.where(lane == FEAT0 + 4, -mid,
                                                 jnp.where(lane == FEAT0 + 5, -lo, 0.0))))
        kf_ref[0, :, hd * LANES:(hd + 1) * LANES] = jnp.where(is_head, _head_slot(fk, hd), k_feat_f).astype(BF16)
        hi, mid, lo = _bf16_pieces(decay_t[hd:hd + 1, :])
        q_feat_f = jnp.where(sub8 == 0, hi,
                             jnp.where(sub8 == 1, mid,
                                       jnp.where(sub8 == 2, lo,
                                                 jnp.where(sub8 < 6, 1.0, 0.0))))
        store_slabs(qft_ref, slice(hd * LANES, (hd + 1) * LANES), jnp.concatenate(
            [fqt[hd * HEAD_DIM:(hd + 1) * HEAD_DIM, :], q_feat_f, zeros_tail_f], axis=0).astype(BF16))
    store_vt(vft_ref, tr(_T_FV))

    cqn = _rms(tok(_TOK_CQ), qn_ref[...]).astype(BF16)
    qm_all = _dot_nt(wuq_ref[...], cqn)
    qm = qm_all[0:N_HEADS_MLA * LANES, :]
    qmr = qm_all[N_HEADS_MLA * LANES:, :]
    kvn = _rms(tok(_TOK_CKV), kvn_ref[...]).astype(BF16)
    k_nope = _dot(kvn, wukvk_ref[...])
    k_rope = tok(_TOK_KR) * ctok_ref[...] + tok(_TOK_KRR) * stok_ref[...]
    cos_t = ct_ref[...]
    sin_t = st_ref[...]
    for hd in range(N_HEADS_MLA):
        sl = slice(hd * LANES, (hd + 1) * LANES)
        store_slabs(qmt_ref, sl,
                    ((qm[sl, :] * cos_t + qmr[sl, :] * sin_t) * (MLA_QK_DIM ** -0.5 * LOG2E)).astype(BF16))
        km_ref[0, :, sl] = (k_nope[:, sl] + k_rope).astype(BF16)
    store_vt(vmt_ref, _dot_nt(wukvv_ref[...], kvn))


def _proj_call(x, p, layer, tables, far, tm):
    b, s, d = x.shape
    nt = s // tm
    weights = [p[name] for name in ("ln_mix_pre", "w_tok", "w_t", "b_f", "q_norm", "kv_norm", "wuq_t", "wukv_k",
                                    "wukv_v_t")]
    ctok, stok, ct, st = tables
    in_specs = [
        pl.BlockSpec((1, tm, d), lambda bi, ti: (bi, ti, 0)),
        *[_layer_spec(w, layer) for w in weights],
        pl.BlockSpec((tm, LANES), lambda bi, ti: (ti, 0)),
        pl.BlockSpec((tm, LANES), lambda bi, ti: (ti, 0)),
        pl.BlockSpec((LANES, tm), lambda bi, ti: (0, ti)),
        pl.BlockSpec((LANES, tm), lambda bi, ti: (0, ti)),
        pl.BlockSpec(memory_space=pltpu.SMEM),
    ]

    per = tm // ATT_TILE

    def qt_spec(n_heads):
        return pl.BlockSpec((1, per, n_heads * LANES, ATT_TILE), lambda bi, ti: (bi, ti, 0, 0))

    def k_spec(n_heads):
        return pl.BlockSpec((1, tm, n_heads * LANES), lambda bi, ti: (bi, ti, 0))

    def vt_spec(width):
        return pl.BlockSpec((1, per, width, ATT_TILE), lambda bi, ti: (bi, ti, 0, 0))

    def qt_shape(n_heads):
        return jax.ShapeDtypeStruct((b, s // ATT_TILE, n_heads * LANES, ATT_TILE), BF16)

    def k_shape(n_heads):
        return jax.ShapeDtypeStruct((b, s, n_heads * LANES), BF16)

    def vt_shape(width):
        return jax.ShapeDtypeStruct((b, s // ATT_TILE, width, ATT_TILE), BF16)

    heads = (N_HEADS_MOBA, N_HEADS_FOX, N_HEADS_MLA)
    out_specs = [spec for n in heads for spec in (qt_spec(n), k_spec(n), vt_spec(n * V_ROWS))]
    out_shape = [shp for n in heads for shp in (qt_shape(n), k_shape(n), vt_shape(n * V_ROWS))]
    return pl.pallas_call(
        functools.partial(_proj_kernel, tm=tm),
        grid=(b, nt),
        in_specs=in_specs,
        out_specs=out_specs,
        out_shape=out_shape,
        scratch_shapes=[pltpu.VMEM((N_HEADS_MOBA * NBLK_PAD, W_MOBA), F32),
                        pltpu.VMEM((SUBLANES, LANES), F32)],
        compiler_params=pltpu.CompilerParams(dimension_semantics=("arbitrary", "arbitrary"),
                                             vmem_limit_bytes=VMEM_LIMIT),
        name="proj",
    )(x, *weights, ctok, stok, ct, st, far)


def _attn_kernel(*refs, has_bias, tq, tk, nq, n_add, n_plain):
    if has_bias:
        (q_ref, j_ref, qt_ref, k_ref, vt_ref, bidx_ref, tab_ref, o_ref,
         acc_ref, m_ref, *bufs, bias_ref) = refs
    else:
        (q_ref, j_ref, qt_ref, k_ref, vt_ref, o_ref,
         acc_ref, m_ref, *bufs, bias_ref) = refs
    hp = pl.program_id(0)
    bi = pl.program_id(1)
    blk = MOBA_BLOCK
    n_near = N_NEAR if has_bias else 1

    @pl.when(bi == 0)
    def _():
        tri = (lax.broadcasted_iota(jnp.int32, (blk, blk), 0) <= lax.broadcasted_iota(jnp.int32, (blk, blk), 1))
        for hh in range(2):
            bias_ref[hh, 0] = jnp.full((blk, blk), NEG_INF, F32)
            bias_ref[hh, n_near + 1] = jnp.zeros((blk, blk), F32)
            if has_bias:
                def build(d, carry, hh=hh):
                    idx = bidx_ref[d]
                    bias_ref[hh, d + 1] = lax.fori_loop(
                        0, T5_BUCKETS,
                        lambda bk, tl: jnp.where(idx == bk, tab_ref[bk, hp * 2 + hh] * LOG2E, tl),
                        jnp.zeros((blk, blk), F32))
                    return carry
                lax.fori_loop(0, n_near, build, 0)
                bias_ref[hh, 1] = jnp.where(tri, bias_ref[hh, 1], NEG_INF)
            else:
                bias_ref[hh, 1] = jnp.where(tri, 0.0, NEG_INF)

    acc_ref[...] = jnp.zeros_like(acc_ref)
    m_ref[...] = jnp.full(m_ref.shape, NEG_INF, F32)

    n_items = n_add + n_plain

    def scores(e, s_ref, additive):
        qi = q_ref[e]
        j = j_ref[e]
        start = pl.multiple_of(j * tk, tk)
        tmax = []
        for hh in range(2):
            k = k_ref[0, pl.ds(start, tk), hh * LANES:(hh + 1) * LANES]
            s = _dot(k, qt_ref[0, qi, hh * LANES:(hh + 1) * LANES, :])
            if additive:
                s = s + jnp.concatenate([
                    jnp.concatenate([
                        bias_ref[hh, jnp.clip((qi * (tq // blk) + c) - (j * (tk // blk) + r), -1, n_near) + 1]
                        for c in range(tq // blk)], axis=1)
                    for r in range(tk // blk)], axis=0)
            s_ref[hh] = s
            tmax.append(jnp.max(s, axis=0, keepdims=True))
        return tuple(tmax)

    def update(e, s_ref, tmax):
        qi = q_ref[e]
        j = j_ref[e]
        for hh in range(2):
            m_old = m_ref[qi, hh, 0:1, :]
            m_new = jnp.maximum(m_old, tmax[hh])
            alpha = jnp.exp2(m_old - m_new)
            p = jnp.exp2(s_ref[hh] - m_new).astype(BF16)
            v = vt_ref[0, j, hh * V_ROWS:(hh + 1) * V_ROWS, :]
            acc_ref[qi, hh] = alpha * acc_ref[qi, hh] + _dot(v, p)
            m_ref[qi, hh, 0:1, :] = m_new

    def step(e, r, carry, additive):
        tmax_cur, tmax_nxt = carry
        tmax_new = scores(e + 2, bufs[(r + 2) % ATT_STAGES], additive)
        update(e, bufs[r], tmax_cur)
        return tmax_nxt, tmax_new

    def steps(lo, hi, carry, additive):
        n_groups = max(hi - lo, 0) // ATT_GROUP

        def group(g, carry):
            for r in range(ATT_GROUP):
                carry = step(lo + ATT_GROUP * g + r, (lo + r) % ATT_STAGES, carry, additive)
            return carry

        carry = lax.fori_loop(0, n_groups, group, carry)
        for e in range(lo + ATT_GROUP * n_groups, hi):
            carry = step(e, e % ATT_STAGES, carry, additive)
        return carry

    carry = (scores(0, bufs[0], 0 < n_add), scores(1, bufs[1], 1 < n_add))
    split = max(n_add - 2, 0)
    carry = steps(0, split, carry, True)
    carry = steps(split, n_items - 2, carry, False)
    update(n_items - 2, bufs[(n_items - 2) % ATT_STAGES], carry[0])
    update(n_items - 1, bufs[(n_items - 1) % ATT_STAGES], carry[1])

    for qi in range(nq):
        o = [acc_ref[qi, hh, 0:HEAD_DIM, :] / acc_ref[qi, hh, HEAD_DIM:HEAD_DIM + 1, :] for hh in range(2)]
        o_ref[0, qi * tq:(qi + 1) * tq, :] = jnp.concatenate(o, axis=0).T.astype(BF16)


def _attn_schedule(nq, tq, tk, n_near):
    blk = MOBA_BLOCK
    add, plain = [], []
    for q in range(nq):
        for j in range(q, -1, -1):
            min_dist = q * (tq // blk) - (j * (tk // blk) + tk // blk - 1)
            (add if min_dist < n_near else plain).append((q, j))
    items = add + plain
    return (jnp.asarray([q for q, _ in items], jnp.int32), jnp.asarray([j for _, j in items], jnp.int32),
            len(add), len(plain))


def _attn_call(qt, k, vt, n_pairs, bias_inputs=None):
    b, nq, _, tq = qt.shape
    _, s, _ = k.shape
    tk = vt.shape[-1]
    assert tq == tk, "the causal-triangle tile assumes equal query and key tiles"
    has_bias = bias_inputs is not None
    q_tab, j_tab, n_add, n_plain = _attn_schedule(nq, tq, tk, N_NEAR if has_bias else 1)
    assert n_add + n_plain >= 2
    smem = pl.BlockSpec(memory_space=pltpu.SMEM)
    in_specs = [
        smem, smem,
        pl.BlockSpec((1, nq, 2 * LANES, tq), lambda p, bi: (bi, 0, p, 0)),
        pl.BlockSpec((1, s, 2 * LANES), lambda p, bi: (bi, 0, p)),
        pl.BlockSpec((1, s // tk, 2 * V_ROWS, tk), lambda p, bi: (bi, 0, p, 0)),
    ]
    args = [q_tab, j_tab, qt, k, vt]
    n_tiles = (N_NEAR if has_bias else 1) + 2
    scratch = [pltpu.VMEM((nq, 2, V_ROWS, tq), F32), pltpu.VMEM((nq, 2, SUBLANES, tq), F32),
               *[pltpu.VMEM((2, tk, tq), F32) for _ in range(ATT_STAGES)],
               pltpu.VMEM((2, n_tiles, MOBA_BLOCK, MOBA_BLOCK), F32)]
    if has_bias:
        bidx, tab = bias_inputs
        in_specs += [pl.BlockSpec(bidx.shape, lambda p, bi: (0, 0, 0)), smem]
        args += [bidx, tab]
    return pl.pallas_call(
        functools.partial(_attn_kernel, has_bias=has_bias, tq=tq, tk=tk, nq=nq, n_add=n_add, n_plain=n_plain),
        grid=(n_pairs, b),
        in_specs=in_specs,
        out_specs=pl.BlockSpec((1, s, 2 * HEAD_DIM), lambda p, bi: (bi, 0, p)),
        out_shape=jax.ShapeDtypeStruct((b, s, n_pairs * 2 * HEAD_DIM), BF16),
        scratch_shapes=scratch,
        compiler_params=pltpu.CompilerParams(dimension_semantics=("arbitrary",) * 2,
                                             vmem_limit_bytes=VMEM_LIMIT),
        name="attn_bias" if has_bias else "attn",
    )(*args)


def _oproj_kernel(x_ref, oa_ref, ob_ref, oc_ref, wo_ref, g_ref, o_ref):
    merged = jnp.concatenate([oa_ref[0], ob_ref[0], oc_ref[0]], axis=1)
    o_ref[0] = x_ref[0] + _rms(_dot(merged, wo_ref[...]), g_ref[...])


def _oproj_call(x, oa, ob, oc, p, layer, tm):
    b, s, d = x.shape

    def rows(width):
        return pl.BlockSpec((1, tm, width), lambda bi, ti: (bi, ti, 0))

    return pl.pallas_call(
        _oproj_kernel,
        grid=(b, s // tm),
        in_specs=[rows(d), rows(W_MOBA), rows(W_MLA), rows(W_FOX),
                  _layer_spec(p["w_o"], layer), _layer_spec(p["ln_mix_post"], layer)],
        out_specs=rows(d),
        out_shape=jax.ShapeDtypeStruct(x.shape, F32),
        compiler_params=pltpu.CompilerParams(dimension_semantics=("arbitrary", "arbitrary"),
                                             vmem_limit_bytes=VMEM_LIMIT),
        name="oproj",
    )(x, oa, ob, oc, p["w_o"], p["ln_mix_post"])


def _ffn_kernel(x_ref, xp_ref, gpre_ref, wup_ref, cw_ref, cb_ref, wd_ref, gpost_ref, o_ref, h_ref, u_ref, *, tm):
    ti = pl.program_id(1)
    halo = FFN_HALO

    h_ref[halo:, :] = _rms(x_ref[0], gpre_ref[...]).astype(BF16)
    prev = _rms(xp_ref[0], gpre_ref[...])
    h_ref[0:halo, :] = jnp.where(ti == 0, 0.0, prev).astype(BF16)

    h = h_ref[...]
    for half in range(2):
        u_ref[half] = _dot(h, wup_ref[:, half * D_FF:(half + 1) * D_FF])

    def conv(half):
        cols = slice(half * D_FF, (half + 1) * D_FF)
        out = cb_ref[:, cols]
        for tap in range(CONV_WIDTH):
            first = halo - (CONV_WIDTH - 1) + tap
            out = out + cw_ref[tap:tap + 1, cols] * u_ref[half, first:first + tm, :]
        return out

    act = (jax.nn.gelu(conv(0), approximate=True) * conv(1)).astype(BF16)
    o_ref[0] = x_ref[0] + _rms(_dot(act, wd_ref[...]), gpost_ref[...])


def _ffn_call(x, p, layer, tm):
    b, s, d = x.shape
    halo = FFN_HALO
    blocks_per_tile = tm // halo
    weights = [p[name] for name in ("ln_ffn_pre", "w_up", "conv_w", "conv_b", "w_down", "ln_ffn_post")]
    in_specs = [
        pl.BlockSpec((1, tm, d), lambda bi, ti: (bi, ti, 0)),
        pl.BlockSpec((1, halo, d), lambda bi, ti: (bi, jnp.maximum(ti * blocks_per_tile - 1, 0), 0)),
        *[_layer_spec(w, layer) for w in weights],
    ]
    return pl.pallas_call(
        functools.partial(_ffn_kernel, tm=tm),
        grid=(b, s // tm),
        in_specs=in_specs,
        out_specs=pl.BlockSpec((1, tm, d), lambda bi, ti: (bi, ti, 0)),
        out_shape=jax.ShapeDtypeStruct(x.shape, F32),
        scratch_shapes=[pltpu.VMEM((halo + tm, d), BF16), pltpu.VMEM((2, halo + tm, D_FF), F32)],
        compiler_params=pltpu.CompilerParams(dimension_semantics=("arbitrary",) * 2,
                                             vmem_limit_bytes=VMEM_LIMIT),
        name="ffn",
    )(x, x, *weights)


def _pad_cols(w, left, total):
    return jnp.pad(w, [(0, 0)] * (w.ndim - 1) + [(left, total - left - w.shape[-1])])


def _prep_params(ln_mix_pre, ln_mix_post, ln_ffn_pre, ln_ffn_post, w_in, b_f, q_norm, kv_norm, w_uq, w_ukv, w_o,
                 w_up, conv_w, conv_b, w_down):
    depth = w_in.shape[0]
    offs = [0]
    for sz in PROJ_SIZES:
        offs.append(offs[-1] + sz)
    a_q, a_k, a_v, c_q, c_kv, k_r, f_q, f_k, f_v, f_g = [w_in[..., offs[i]:offs[i + 1]] for i in range(10)]
    half = MLA_ROPE_DIM // 2

    def rot(w):
        return jnp.concatenate([-w[..., half:], w[..., :half]], axis=-1)

    def t(w):
        return jnp.swapaxes(w, -1, -2)

    w_tok = jnp.concatenate([
        a_k, f_k, c_q, c_kv,
        _pad_cols(k_r, FEAT0, LANES), _pad_cols(rot(k_r), FEAT0, LANES), _pad_cols(f_g, 0, LANES)], axis=-1)
    uq = w_uq.reshape(depth, MLA_Q_RANK, N_HEADS_MLA, MLA_QK_DIM)
    uq_main = _pad_cols(uq, 0, LANES)
    uq_rot = _pad_cols(rot(uq[..., MLA_NOPE_DIM:]), MLA_NOPE_DIM, LANES)
    ukv = w_ukv.reshape(depth, MLA_KV_RANK, N_HEADS_MLA, MLA_NOPE_DIM + MLA_V_DIM)
    ukv_k = _pad_cols(ukv[..., :MLA_NOPE_DIM], 0, LANES)
    ukv_v = ukv[..., MLA_NOPE_DIM:]
    return {
        "ln_mix_pre": ln_mix_pre[:, None, :],
        "ln_mix_post": ln_mix_post[:, None, :],
        "ln_ffn_pre": ln_ffn_pre[:, None, :],
        "ln_ffn_post": ln_ffn_post[:, None, :],
        "w_tok": w_tok.astype(BF16),
        "w_t": t(jnp.concatenate([a_q, f_q, a_v, f_v], axis=-1)).astype(BF16),
        "b_f": _pad_cols(b_f[:, None, :], 0, LANES),
        "q_norm": q_norm[:, None, :],
        "kv_norm": kv_norm[:, None, :],
        "wuq_t": t(jnp.concatenate([uq_main.reshape(depth, MLA_Q_RANK, N_HEADS_MLA * LANES),
                                    uq_rot.reshape(depth, MLA_Q_RANK, N_HEADS_MLA * LANES)], axis=-1)).astype(BF16),
        "wukv_k": ukv_k.reshape(depth, MLA_KV_RANK, N_HEADS_MLA * LANES).astype(BF16),
        "wukv_v_t": t(ukv_v.reshape(depth, MLA_KV_RANK, W_MLA)).astype(BF16),
        "w_o": w_o.astype(BF16),
        "w_up": w_up.astype(BF16),
        "conv_w": conv_w,
        "conv_b": conv_b[:, None, :],
        "w_down": w_down.astype(BF16),
    }


def _layer_spec(a, layer):
    return pl.BlockSpec((None,) + a.shape[1:], lambda *_: (layer,) + (0,) * (a.ndim - 1))


def _rope_tables(s):
    half = MLA_ROPE_DIM // 2
    inv = ROPE_THETA ** (-jnp.arange(half, dtype=F32) / half)
    ang = jnp.arange(s).astype(F32)[:, None] * inv[None, :]
    cos = jnp.concatenate([jnp.cos(ang)] * 2, axis=1)
    sin = jnp.concatenate([jnp.sin(ang)] * 2, axis=1)
    ctok = _pad_cols(cos, FEAT0, LANES)
    stok = _pad_cols(sin, FEAT0, LANES)
    ones = jnp.ones((s, MLA_NOPE_DIM), F32)
    ct = jnp.concatenate([ones, cos, jnp.zeros((s, LANES - MLA_QK_DIM), F32)], axis=1).T
    st = stok.T
    return ctok, stok, ct, st


def _t5_bucket_tiles(tile):
    d = jnp.arange(N_NEAR)[:, None, None]
    j = jnp.arange(tile)[None, :, None]
    i = jnp.arange(tile)[None, None, :]
    n = jnp.maximum(d * tile + i - j, 0)
    exact = T5_BUCKETS // 2
    large = exact + (jnp.log(jnp.maximum(n, 1).astype(F32) / exact)
                     / math.log(T5_MAX_DIST / exact) * (T5_BUCKETS - exact)).astype(jnp.int32)
    return jnp.where(n < exact, n, jnp.minimum(large, T5_BUCKETS - 1)).astype(jnp.int32)


def kernel(x, rel_bias, ln_mix_pre, ln_mix_post, ln_ffn_pre, ln_ffn_post, w_in, b_f, q_norm, kv_norm, w_uq,
           w_ukv, w_o, w_up, conv_w, conv_b, w_down):
    b, s, d = x.shape
    depth = w_in.shape[0]
    assert s % PROJ_ROWS == 0 and s % FFN_ROWS == 0 and s % OPROJ_ROWS == 0
    assert s // MOBA_BLOCK <= NBLK_PAD
    tables = _rope_tables(s)
    bidx = _t5_bucket_tiles(MOBA_BLOCK)
    far = rel_bias[T5_BUCKETS - 1, :]
    p = _prep_params(ln_mix_pre, ln_mix_post, ln_ffn_pre, ln_ffn_post, w_in, b_f, q_norm, kv_norm, w_uq, w_ukv, w_o,
                     w_up, conv_w, conv_b, w_down)
    for l in range(depth):
        qat, ka, vat, qft, kf, vft, qmt, km, vmt = _proj_call(x, p, l, tables, far, PROJ_ROWS)
        oa = _attn_call(qat, ka, vat, N_HEADS_MOBA // 2, (bidx, rel_bias))
        ob = _attn_call(qmt, km, vmt, N_HEADS_MLA // 2)
        oc = _attn_call(qft, kf, vft, N_HEADS_FOX // 2)
        x = _oproj_call(x, oa, ob, oc, p, l, OPROJ_ROWS)
        x = _ffn_call(x, p, l, FFN_ROWS)
    return x
```

```python
import functools
import math

import jax
import jax.numpy as jnp
from jax import lax
from jax.experimental import pallas as pl
from jax.experimental.pallas import tpu as pltpu

F32 = jnp.float32
BF16 = jnp.bfloat16

HEAD_DIM = 64
N_HEADS_MOBA = 6
N_HEADS_MLA = 4
N_HEADS_FOX = 6
MOBA_BLOCK = 256
MOBA_TOPK = 3
MLA_Q_RANK = 256
MLA_KV_RANK = 128
MLA_NOPE_DIM = 64
MLA_ROPE_DIM = 32
MLA_V_DIM = 64
MLA_QK_DIM = MLA_NOPE_DIM + MLA_ROPE_DIM
ROPE_THETA = 10000.0
T5_BUCKETS = 32
T5_MAX_DIST = 1024
D_FF = 2816
CONV_WIDTH = 3
NORM_EPS = 1e-6
NEG_INF = -1e30
W_MOBA = N_HEADS_MOBA * HEAD_DIM
W_MLA = N_HEADS_MLA * MLA_V_DIM
W_FOX = N_HEADS_FOX * HEAD_DIM
PROJ_SIZES = (W_MOBA, W_MOBA, W_MOBA, MLA_Q_RANK, MLA_KV_RANK, MLA_ROPE_DIM, W_FOX, W_FOX, W_FOX, N_HEADS_FOX)

LANES = 128
SUBLANES = 8
BF16_ROWS = 16
VMEM_LIMIT = 56 * 1024 * 1024

ATT_TILE = 256
PROJ_ROWS = 2 * ATT_TILE
ATT_STAGES = 4
ATT_GROUP = 8 * ATT_STAGES
OPROJ_ROWS = 1024
OPROJ_SLOTS = 3
FFN_ROWS = 512
FFN_HALO = BF16_ROWS
FEAT0 = HEAD_DIM
V_ROWS = HEAD_DIM + BF16_ROWS
LOG2E = math.log2(math.e)
NBLK_PAD = 16
N_NEAR = (T5_MAX_DIST + MOBA_BLOCK - 1 + MOBA_BLOCK - 1) // MOBA_BLOCK


def _ranges(sizes):
    out, start = [], 0
    for size in sizes:
        out.append((start, start + size))
        start += size
    return out


_TOK_AK, _TOK_FK, _TOK_CQ, _TOK_CKV, _TOK_KR, _TOK_KRR, _TOK_FG = _ranges(
    (W_MOBA, W_FOX, MLA_Q_RANK, MLA_KV_RANK, LANES, LANES, LANES))
_T_AQ, _T_FQ, _T_AV, _T_FV = _ranges((W_MOBA, W_FOX, W_MOBA, W_FOX))


def _rms(xf, g):
    return xf * lax.rsqrt(jnp.mean(xf * xf, axis=-1, keepdims=True) + NORM_EPS) * g


def _dot(a, b):
    return jnp.dot(a, b, preferred_element_type=F32)


def _dot_nt(a, b):
    return lax.dot_general(a, b, (((1,), (1,)), ((), ())), preferred_element_type=F32)


def _bf16_pieces(x):
    hi = x.astype(BF16).astype(F32)
    r = x - hi
    mid = r.astype(BF16).astype(F32)
    lo = (r - mid).astype(BF16).astype(F32)
    return hi, mid, lo


def _dot_3pass(a, b):
    a_hi = a.astype(BF16)
    a_lo = (a - a_hi.astype(F32)).astype(BF16)
    b_hi = b.astype(BF16)
    b_lo = (b - b_hi.astype(F32)).astype(BF16)
    return _dot(jnp.concatenate([a_hi, a_hi, a_lo], axis=1), jnp.concatenate([b_hi, b_lo, b_hi], axis=0))


def _head_slot(arr, hd):
    pair = arr[:, (hd // 2) * LANES:(hd // 2 + 1) * LANES]
    if hd % 2:
        pair = pltpu.roll(pair, HEAD_DIM, axis=1)
    return pair


def _proj_kernel(x_ref, g_ref, wtok_ref, wt_ref, bf_ref, qn_ref, kvn_ref, wuq_ref,
                 wukvk_ref, wukvv_ref, ctok_ref, stok_ref, ct_ref, st_ref, far_ref,
                 qat_ref, ka_ref, vat_ref, qft_ref, kf_ref, vft_ref, qmt_ref, km_ref, vmt_ref,
                 kmean_ref, fcarry_ref, *, tm):
    t = pl.program_id(1)
    n_sub = tm // MOBA_BLOCK

    @pl.when(t == 0)
    def _():
        kmean_ref[...] = jnp.zeros_like(kmean_ref)
        fcarry_ref[...] = jnp.zeros_like(fcarry_ref)

    h = _rms(x_ref[0], g_ref[...]).astype(BF16)
    lane = lax.broadcasted_iota(jnp.int32, (tm, LANES), 1)
    row = lax.broadcasted_iota(jnp.int32, (tm, LANES), 0)
    blk0 = t * n_sub
    is_head = lane < HEAD_DIM

    tok_all = _dot(h, wtok_ref[...])
    t_all = _dot_nt(wt_ref[...], h)

    def tok(rng):
        return tok_all[:, rng[0]:rng[1]]

    def tr(rng):
        return t_all[rng[0]:rng[1], :]

    sub8v = lax.broadcasted_iota(jnp.int32, (SUBLANES, tm), 0)
    ones_row = jnp.where(sub8v == 0, 1.0, 0.0)
    v_pad = jnp.zeros((V_ROWS - HEAD_DIM - SUBLANES, tm), F32)

    def store_slabs(ref, rows, val):
        for si in range(tm // ATT_TILE):
            ref[0, si, rows, :] = val[:, si * ATT_TILE:(si + 1) * ATT_TILE]

    def store_vt(ref, vt):
        n_heads = vt.shape[0] // HEAD_DIM
        store_slabs(ref, slice(None), jnp.concatenate(
            [piece for hd in range(n_heads)
             for piece in (vt[hd * HEAD_DIM:(hd + 1) * HEAD_DIM, :], ones_row, v_pad)], axis=0).astype(BF16))

    ak = tok(_TOK_AK)
    blk_row = blk0 + lax.shift_right_logical(row, int(math.log2(MOBA_BLOCK)))
    k_feat = jnp.where(lane == FEAT0 + blk_row, 1.0, 0.0)
    for hd in range(N_HEADS_MOBA):
        ka_ref[0, :, hd * LANES:(hd + 1) * LANES] = jnp.where(is_head, _head_slot(ak, hd), k_feat).astype(BF16)

    lane_k = lax.broadcasted_iota(jnp.int32, (1, W_MOBA), 1)
    for bi in range(n_sub):
        mean_row = jnp.mean(ak[bi * MOBA_BLOCK:(bi + 1) * MOBA_BLOCK, :], axis=0, keepdims=True)
        for hd in range(N_HEADS_MOBA):
            in_head = (lane_k >= hd * HEAD_DIM) & (lane_k < (hd + 1) * HEAD_DIM)
            kmean_ref[pl.ds(hd * NBLK_PAD + blk0 + bi, 1), :] = jnp.where(in_head, mean_row, 0.0)

    qat = tr(_T_AQ) * (HEAD_DIM ** -0.5 * LOG2E)
    gate = _dot_3pass(kmean_ref[...], qat)
    own = blk0 + lax.shift_right_logical(
        lax.broadcasted_iota(jnp.int32, (NBLK_PAD, tm), 1), int(math.log2(MOBA_BLOCK)))
    n_io = lax.broadcasted_iota(jnp.int32, (NBLK_PAD, tm), 0)
    past = n_io < own
    zeros_tail = jnp.zeros((LANES - HEAD_DIM - NBLK_PAD, tm), F32)
    for hd in range(N_HEADS_MOBA):
        g = jnp.where(past, gate[hd * NBLK_PAD:(hd + 1) * NBLK_PAD, :], NEG_INF)
        rank = jnp.zeros((NBLK_PAD, tm), jnp.int32)
        for n2 in range(NBLK_PAD):
            r = g[n2:n2 + 1, :]
            tie = jnp.where(n2 < n_io, 1, 0)
            rank = rank + jnp.where(r > g, 1, jnp.where(r == g, tie, 0))
        sel = jnp.where(past, rank, MOBA_TOPK) < MOBA_TOPK
        far = jnp.where(own - n_io >= N_NEAR, far_ref[hd] * LOG2E, 0.0)
        q_feat = jnp.where(sel | (n_io == own), far, NEG_INF)
        store_slabs(qat_ref, slice(hd * LANES, (hd + 1) * LANES), jnp.concatenate(
            [qat[hd * HEAD_DIM:(hd + 1) * HEAD_DIM, :], q_feat, zeros_tail], axis=0).astype(BF16))
    store_vt(vat_ref, tr(_T_AV))

    fg = tok(_TOK_FG) + bf_ref[...]
    logf = jnp.minimum(fg, 0.0) - jnp.log1p(jnp.exp(-jnp.abs(fg)))
    csum = jnp.where(lane < N_HEADS_FOX, logf, 0.0)
    sft = 1
    while sft < tm:
        csum = csum + jnp.where(row >= sft, pltpu.roll(csum, sft, axis=0), 0.0)
        sft *= 2
    decay = csum + fcarry_ref[0:1, :]
    fcarry_ref[0:1, :] = decay[tm - 1:tm, :]
    decay = decay * LOG2E
    decay_t = decay.T

    fk = tok(_TOK_FK)
    fqt = tr(_T_FQ) * (HEAD_DIM ** -0.5 * LOG2E)
    sub8 = lax.broadcasted_iota(jnp.int32, (SUBLANES, tm), 0)
    zeros_tail_f = jnp.zeros((LANES - HEAD_DIM - SUBLANES, tm), F32)
    for hd in range(N_HEADS_FOX):
        fcol = jnp.broadcast_to(decay[:, hd:hd + 1], (tm, LANES))
        hi, mid, lo = _bf16_pieces(fcol)
        k_feat_f = jnp.where(lane < FEAT0 + 3, 1.0,
                             jnp.where(lane == FEAT0 + 3, -hi,
                                       jnp.where(lane == FEAT0 + 4, -mid,
                                                 jnp.where(lane == FEAT0 + 5, -lo, 0.0))))
        kf_ref[0, :, hd * LANES:(hd + 1) * LANES] = jnp.where(is_head, _head_slot(fk, hd), k_feat_f).astype(BF16)
        hi, mid, lo = _bf16_pieces(decay_t[hd:hd + 1, :])
        q_feat_f = jnp.where(sub8 == 0, hi,
                             jnp.where(sub8 == 1, mid,
                                       jnp.where(sub8 == 2, lo,
                                                 jnp.where(sub8 < 6, 1.0, 0.0))))
        store_slabs(qft_ref, slice(hd * LANES, (hd + 1) * LANES), jnp.concatenate(
            [fqt[hd * HEAD_DIM:(hd + 1) * HEAD_DIM, :], q_feat_f, zeros_tail_f], axis=0).astype(BF16))
    store_vt(vft_ref, tr(_T_FV))

    cqn = _rms(tok(_TOK_CQ), qn_ref[...]).astype(BF16)
    qm_all = _dot_nt(wuq_ref[...], cqn)
    qm = qm_all[0:N_HEADS_MLA * LANES, :]
    qmr = qm_all[N_HEADS_MLA * LANES:, :]
    kvn = _rms(tok(_TOK_CKV), kvn_ref[...]).astype(BF16)
    k_nope = _dot(kvn, wukvk_ref[...])
    k_rope = tok(_TOK_KR) * ctok_ref[...] + tok(_TOK_KRR) * stok_ref[...]
    cos_t = ct_ref[...]
    sin_t = st_ref[...]
    for hd in range(N_HEADS_MLA):
        sl = slice(hd * LANES, (hd + 1) * LANES)
        store_slabs(qmt_ref, sl,
                    ((qm[sl, :] * cos_t + qmr[sl, :] * sin_t) * (MLA_QK_DIM ** -0.5 * LOG2E)).astype(BF16))
        km_ref[0, :, sl] = (k_nope[:, sl] + k_rope).astype(BF16)
    store_vt(vmt_ref, _dot_nt(wukvv_ref[...], kvn))


def _proj_call(x, p, layer, tables, far, tm):
    b, s, d = x.shape
    nt = s // tm
    weights = [p[name] for name in ("ln_mix_pre", "w_tok", "w_t", "b_f", "q_norm", "kv_norm", "wuq_t", "wukv_k",
                                    "wukv_v_t")]
    ctok, stok, ct, st = tables
    in_specs = [
        pl.BlockSpec((1, tm, d), lambda bi, ti: (bi, ti, 0)),
        *[_layer_spec(w, layer) for w in weights],
        pl.BlockSpec((tm, LANES), lambda bi, ti: (ti, 0)),
        pl.BlockSpec((tm, LANES), lambda bi, ti: (ti, 0)),
        pl.BlockSpec((LANES, tm), lambda bi, ti: (0, ti)),
        pl.BlockSpec((LANES, tm), lambda bi, ti: (0, ti)),
        pl.BlockSpec(memory_space=pltpu.SMEM),
    ]

    per = tm // ATT_TILE

    def qt_spec(n_heads):
        return pl.BlockSpec((1, per, n_heads * LANES, ATT_TILE), lambda bi, ti: (bi, ti, 0, 0))

    def k_spec(n_heads):
        return pl.BlockSpec((1, tm, n_heads * LANES), lambda bi, ti: (bi, ti, 0))

    def vt_spec(width):
        return pl.BlockSpec((1, per, width, ATT_TILE), lambda bi, ti: (bi, ti, 0, 0))

    def qt_shape(n_heads):
        return jax.ShapeDtypeStruct((b, s // ATT_TILE, n_heads * LANES, ATT_TILE), BF16)

    def k_shape(n_heads):
        return jax.ShapeDtypeStruct((b, s, n_heads * LANES), BF16)

    def vt_shape(width):
        return jax.ShapeDtypeStruct((b, s // ATT_TILE, width, ATT_TILE), BF16)

    heads = (N_HEADS_MOBA, N_HEADS_FOX, N_HEADS_MLA)
    out_specs = [spec for n in heads for spec in (qt_spec(n), k_spec(n), vt_spec(n * V_ROWS))]
    out_shape = [shp for n in heads for shp in (qt_shape(n), k_shape(n), vt_shape(n * V_ROWS))]
    return pl.pallas_call(
        functools.partial(_proj_kernel, tm=tm),
        grid=(b, nt),
        in_specs=in_specs,
        out_specs=out_specs,
        out_shape=out_shape,
        scratch_shapes=[pltpu.VMEM((N_HEADS_MOBA * NBLK_PAD, W_MOBA), F32),
                        pltpu.VMEM((SUBLANES, LANES), F32)],
        compiler_params=pltpu.CompilerParams(dimension_semantics=("arbitrary", "arbitrary"),
                                             vmem_limit_bytes=VMEM_LIMIT),
        name="proj",
    )(x, *weights, ctok, stok, ct, st, far)


def _attn_kernel(*refs, has_bias, tq, tk, nq, n_add, n_plain):
    if has_bias:
        (q_ref, j_ref, qt_ref, k_ref, vt_ref, bidx_ref, tab_ref, o_ref,
         acc_ref, m_ref, *bufs, bias_ref) = refs
    else:
        (q_ref, j_ref, qt_ref, k_ref, vt_ref, o_ref,
         acc_ref, m_ref, *bufs, bias_ref) = refs
    hp = pl.program_id(0)
    bi = pl.program_id(1)
    blk = MOBA_BLOCK
    n_near = N_NEAR if has_bias else 1

    @pl.when(bi == 0)
    def _():
        tri = (lax.broadcasted_iota(jnp.int32, (blk, blk), 0) <= lax.broadcasted_iota(jnp.int32, (blk, blk), 1))
        for hh in range(2):
            bias_ref[hh, 0] = jnp.full((blk, blk), NEG_INF, F32)
            bias_ref[hh, n_near + 1] = jnp.zeros((blk, blk), F32)
            if has_bias:
                def build(d, carry, hh=hh):
                    idx = bidx_ref[d]
                    bias_ref[hh, d + 1] = lax.fori_loop(
                        0, T5_BUCKETS,
                        lambda bk, tl: jnp.where(idx == bk, tab_ref[bk, hp * 2 + hh] * LOG2E, tl),
                        jnp.zeros((blk, blk), F32))
                    return carry
                lax.fori_loop(0, n_near, build, 0)
                bias_ref[hh, 1] = jnp.where(tri, bias_ref[hh, 1], NEG_INF)
            else:
                bias_ref[hh, 1] = jnp.where(tri, 0.0, NEG_INF)

    acc_ref[...] = jnp.zeros_like(acc_ref)
    m_ref[...] = jnp.full(m_ref.shape, NEG_INF, F32)

    n_items = n_add + n_plain

    def scores(e, s_ref, additive):
        qi = q_ref[e]
        j = j_ref[e]
        start = pl.multiple_of(j * tk, tk)
        tmax = []
        for hh in range(2):
            k = k_ref[0, pl.ds(start, tk), hh * LANES:(hh + 1) * LANES]
            s = _dot(k, qt_ref[0, qi, hh * LANES:(hh + 1) * LANES, :])
            if additive:
                s = s + jnp.concatenate([
                    jnp.concatenate([
                        bias_ref[hh, jnp.clip((qi * (tq // blk) + c) - (j * (tk // blk) + r), -1, n_near) + 1]
                        for c in range(tq // blk)], axis=1)
                    for r in range(tk // blk)], axis=0)
            s_ref[hh] = s
            tmax.append(jnp.max(s, axis=0, keepdims=True))
        return tuple(tmax)

    def update(e, s_ref, tmax):
        qi = q_ref[e]
        j = j_ref[e]
        for hh in range(2):
            m_old = m_ref[qi, hh, 0:1, :]
            m_new = jnp.maximum(m_old, tmax[hh])
            alpha = jnp.exp2(m_old - m_new)
            p = jnp.exp2(s_ref[hh] - m_new).astype(BF16)
            v = vt_ref[0, j, hh * V_ROWS:(hh + 1) * V_ROWS, :]
            acc_ref[qi, hh] = alpha * acc_ref[qi, hh] + _dot(v, p)
            m_ref[qi, hh, 0:1, :] = m_new

    def step(e, r, carry, additive):
        tmax_cur, tmax_nxt = carry
        tmax_new = scores(e + 2, bufs[(r + 2) % ATT_STAGES], additive)
        update(e, bufs[r], tmax_cur)
        return tmax_nxt, tmax_new

    def steps(lo, hi, carry, additive):
        n_groups = max(hi - lo, 0) // ATT_GROUP

        def group(g, carry):
            for r in range(ATT_GROUP):
                carry = step(lo + ATT_GROUP * g + r, (lo + r) % ATT_STAGES, carry, additive)
            return carry

        carry = lax.fori_loop(0, n_groups, group, carry)
        for e in range(lo + ATT_GROUP * n_groups, hi):
            carry = step(e, e % ATT_STAGES, carry, additive)
        return carry

    carry = (scores(0, bufs[0], 0 < n_add), scores(1, bufs[1], 1 < n_add))
    split = max(n_add - 2, 0)
    carry = steps(0, split, carry, True)
    carry = steps(split, n_items - 2, carry, False)
    update(n_items - 2, bufs[(n_items - 2) % ATT_STAGES], carry[0])
    update(n_items - 1, bufs[(n_items - 1) % ATT_STAGES], carry[1])

    for qi in range(nq):
        o = [acc_ref[qi, hh, 0:HEAD_DIM, :] / acc_ref[qi, hh, HEAD_DIM:HEAD_DIM + 1, :] for hh in range(2)]
        o_ref[0, qi * tq:(qi + 1) * tq, :] = jnp.concatenate(o, axis=0).T.astype(BF16)


def _attn_schedule(nq, tq, tk, n_near):
    blk = MOBA_BLOCK
    add, plain = [], []
    for q in range(nq):
        for j in range(q, -1, -1):
            min_dist = q * (tq // blk) - (j * (tk // blk) + tk // blk - 1)
            (add if min_dist < n_near else plain).append((q, j))
    items = add + plain
    return (jnp.asarray([q for q, _ in items], jnp.int32), jnp.asarray([j for _, j in items], jnp.int32),
            len(add), len(plain))


def _attn_call(qt, k, vt, n_pairs, bias_inputs=None):
    b, nq, _, tq = qt.shape
    _, s, _ = k.shape
    tk = vt.shape[-1]
    assert tq == tk, "the causal-triangle tile assumes equal query and key tiles"
    has_bias = bias_inputs is not None
    q_tab, j_tab, n_add, n_plain = _attn_schedule(nq, tq, tk, N_NEAR if has_bias else 1)
    assert n_add + n_plain >= 2
    smem = pl.BlockSpec(memory_space=pltpu.SMEM)
    in_specs = [
        smem, smem,
        pl.BlockSpec((1, nq, 2 * LANES, tq), lambda p, bi: (bi, 0, p, 0)),
        pl.BlockSpec((1, s, 2 * LANES), lambda p, bi: (bi, 0, p)),
        pl.BlockSpec((1, s // tk, 2 * V_ROWS, tk), lambda p, bi: (bi, 0, p, 0)),
    ]
    args = [q_tab, j_tab, qt, k, vt]
    n_tiles = (N_NEAR if has_bias else 1) + 2
    scratch = [pltpu.VMEM((nq, 2, V_ROWS, tq), F32), pltpu.VMEM((nq, 2, SUBLANES, tq), F32),
               *[pltpu.VMEM((2, tk, tq), F32) for _ in range(ATT_STAGES)],
               pltpu.VMEM((2, n_tiles, MOBA_BLOCK, MOBA_BLOCK), F32)]
    if has_bias:
        bidx, tab = bias_inputs
        in_specs += [pl.BlockSpec(bidx.shape, lambda p, bi: (0, 0, 0)), smem]
        args += [bidx, tab]
    return pl.pallas_call(
        functools.partial(_attn_kernel, has_bias=has_bias, tq=tq, tk=tk, nq=nq, n_add=n_add, n_plain=n_plain),
        grid=(n_pairs, b),
        in_specs=in_specs,
        out_specs=pl.BlockSpec((1, s, 2 * HEAD_DIM), lambda p, bi: (bi, 0, p)),
        out_shape=jax.ShapeDtypeStruct((b, s, n_pairs * 2 * HEAD_DIM), BF16),
        scratch_shapes=scratch,
        compiler_params=pltpu.CompilerParams(dimension_semantics=("arbitrary",) * 2,
                                             vmem_limit_bytes=VMEM_LIMIT),
        name="attn_bias" if has_bias else "attn",
    )(*args)


def _oproj_kernel(x_hbm, oa_ref, ob_ref, oc_ref, wo_ref, g_ref, o_ref, xbuf_ref, sem_ref, *, tm):
    nt = pl.num_programs(1)
    n_steps = pl.num_programs(0) * nt
    step = pl.program_id(0) * nt + pl.program_id(1)

    def x_copy(s_idx):
        bi = lax.div(s_idx, nt)
        start = pl.multiple_of(lax.rem(s_idx, nt) * tm, tm)
        slot = lax.rem(s_idx, OPROJ_SLOTS)
        return pltpu.make_async_copy(x_hbm.at[bi, pl.ds(start, tm), :], xbuf_ref.at[slot], sem_ref.at[slot])

    @pl.when(step == 0)
    def _():
        x_copy(step).start()

    @pl.when((step == 0) & (n_steps > 1))
    def _():
        x_copy(step + 1).start()

    @pl.when(step + 2 < n_steps)
    def _():
        x_copy(step + 2).start()

    x_copy(step).wait()
    merged = jnp.concatenate([oa_ref[0], ob_ref[0], oc_ref[0]], axis=1)
    o_ref[0] = xbuf_ref[lax.rem(step, OPROJ_SLOTS)] + _rms(_dot(merged, wo_ref[...]), g_ref[...])


def _oproj_call(x, oa, ob, oc, p, layer, tm):
    b, s, d = x.shape

    def rows(width):
        return pl.BlockSpec((1, tm, width), lambda bi, ti: (bi, ti, 0))

    return pl.pallas_call(
        functools.partial(_oproj_kernel, tm=tm),
        grid=(b, s // tm),
        in_specs=[pl.BlockSpec(memory_space=pl.ANY), rows(W_MOBA), rows(W_MLA), rows(W_FOX),
                  _layer_spec(p["w_o"], layer), _layer_spec(p["ln_mix_post"], layer)],
        out_specs=rows(d),
        out_shape=jax.ShapeDtypeStruct(x.shape, F32),
        scratch_shapes=[pltpu.VMEM((OPROJ_SLOTS, tm, d), F32), pltpu.SemaphoreType.DMA((OPROJ_SLOTS,))],
        compiler_params=pltpu.CompilerParams(dimension_semantics=("arbitrary", "arbitrary"),
                                             vmem_limit_bytes=VMEM_LIMIT),
        name="oproj",
    )(x, oa, ob, oc, p["w_o"], p["ln_mix_post"])


def _ffn_kernel(x_ref, xp_ref, gpre_ref, wup_ref, cw_ref, cb_ref, wd_ref, gpost_ref, o_ref, h_ref, u_ref, *, tm):
    ti = pl.program_id(1)
    halo = FFN_HALO

    h_ref[halo:, :] = _rms(x_ref[0], gpre_ref[...]).astype(BF16)
    prev = _rms(xp_ref[0], gpre_ref[...])
    h_ref[0:halo, :] = jnp.where(ti == 0, 0.0, prev).astype(BF16)

    h = h_ref[...]
    for half in range(2):
        u_ref[half] = _dot(h, wup_ref[:, half * D_FF:(half + 1) * D_FF])

    def conv(half):
        cols = slice(half * D_FF, (half + 1) * D_FF)
        out = cb_ref[:, cols]
        for tap in range(CONV_WIDTH):
            first = halo - (CONV_WIDTH - 1) + tap
            out = out + cw_ref[tap:tap + 1, cols] * u_ref[half, first:first + tm, :]
        return out

    act = (jax.nn.gelu(conv(0), approximate=True) * conv(1)).astype(BF16)
    o_ref[0] = x_ref[0] + _rms(_dot(act, wd_ref[...]), gpost_ref[...])


def _ffn_call(x, p, layer, tm):
    b, s, d = x.shape
    halo = FFN_HALO
    blocks_per_tile = tm // halo
    weights = [p[name] for name in ("ln_ffn_pre", "w_up", "conv_w", "conv_b", "w_down", "ln_ffn_post")]
    in_specs = [
        pl.BlockSpec((1, tm, d), lambda bi, ti: (bi, ti, 0)),
        pl.BlockSpec((1, halo, d), lambda bi, ti: (bi, jnp.maximum(ti * blocks_per_tile - 1, 0), 0)),
        *[_layer_spec(w, layer) for w in weights],
    ]
    return pl.pallas_call(
        functools.partial(_ffn_kernel, tm=tm),
        grid=(b, s // tm),
        in_specs=in_specs,
        out_specs=pl.BlockSpec((1, tm, d), lambda bi, ti: (bi, ti, 0)),
        out_shape=jax.ShapeDtypeStruct(x.shape, F32),
        scratch_shapes=[pltpu.VMEM((halo + tm, d), BF16), pltpu.VMEM((2, halo + tm, D_FF), F32)],
        compiler_params=pltpu.CompilerParams(dimension_semantics=("arbitrary",) * 2,
                                             vmem_limit_bytes=VMEM_LIMIT),
        name="ffn",
    )(x, x, *weights)


def _pad_cols(w, left, total):
    return jnp.pad(w, [(0, 0)] * (w.ndim - 1) + [(left, total - left - w.shape[-1])])


def _prep_params(ln_mix_pre, ln_mix_post, ln_ffn_pre, ln_ffn_post, w_in, b_f, q_norm, kv_norm, w_uq, w_ukv, w_o,
                 w_up, conv_w, conv_b, w_down):
    depth = w_in.shape[0]
    offs = [0]
    for sz in PROJ_SIZES:
        offs.append(offs[-1] + sz)
    a_q, a_k, a_v, c_q, c_kv, k_r, f_q, f_k, f_v, f_g = [w_in[..., offs[i]:offs[i + 1]] for i in range(10)]
    half = MLA_ROPE_DIM // 2

    def rot(w):
        return jnp.concatenate([-w[..., half:], w[..., :half]], axis=-1)

    def t(w):
        return jnp.swapaxes(w, -1, -2)

    w_tok = jnp.concatenate([
        a_k, f_k, c_q, c_kv,
        _pad_cols(k_r, FEAT0, LANES), _pad_cols(rot(k_r), FEAT0, LANES), _pad_cols(f_g, 0, LANES)], axis=-1)
    uq = w_uq.reshape(depth, MLA_Q_RANK, N_HEADS_MLA, MLA_QK_DIM)
    uq_main = _pad_cols(uq, 0, LANES)
    uq_rot = _pad_cols(rot(uq[..., MLA_NOPE_DIM:]), MLA_NOPE_DIM, LANES)
    ukv = w_ukv.reshape(depth, MLA_KV_RANK, N_HEADS_MLA, MLA_NOPE_DIM + MLA_V_DIM)
    ukv_k = _pad_cols(ukv[..., :MLA_NOPE_DIM], 0, LANES)
    ukv_v = ukv[..., MLA_NOPE_DIM:]
    return {
        "ln_mix_pre": ln_mix_pre[:, None, :],
        "ln_mix_post": ln_mix_post[:, None, :],
        "ln_ffn_pre": ln_ffn_pre[:, None, :],
        "ln_ffn_post": ln_ffn_post[:, None, :],
        "w_tok": w_tok.astype(BF16),
        "w_t": t(jnp.concatenate([a_q, f_q, a_v, f_v], axis=-1)).astype(BF16),
        "b_f": _pad_cols(b_f[:, None, :], 0, LANES),
        "q_norm": q_norm[:, None, :],
        "kv_norm": kv_norm[:, None, :],
        "wuq_t": t(jnp.concatenate([uq_main.reshape(depth, MLA_Q_RANK, N_HEADS_MLA * LANES),
                                    uq_rot.reshape(depth, MLA_Q_RANK, N_HEADS_MLA * LANES)], axis=-1)).astype(BF16),
        "wukv_k": ukv_k.reshape(depth, MLA_KV_RANK, N_HEADS_MLA * LANES).astype(BF16),
        "wukv_v_t": t(ukv_v.reshape(depth, MLA_KV_RANK, W_MLA)).astype(BF16),
        "w_o": w_o.astype(BF16),
        "w_up": w_up.astype(BF16),
        "conv_w": conv_w,
        "conv_b": conv_b[:, None, :],
        "w_down": w_down.astype(BF16),
    }


def _layer_spec(a, layer):
    return pl.BlockSpec((None,) + a.shape[1:], lambda *_: (layer,) + (0,) * (a.ndim - 1))


def _rope_tables(s):
    half = MLA_ROPE_DIM // 2
    inv = ROPE_THETA ** (-jnp.arange(half, dtype=F32) / half)
    ang = jnp.arange(s).astype(F32)[:, None] * inv[None, :]
    cos = jnp.concatenate([jnp.cos(ang)] * 2, axis=1)
    sin = jnp.concatenate([jnp.sin(ang)] * 2, axis=1)
    ctok = _pad_cols(cos, FEAT0, LANES)
    stok = _pad_cols(sin, FEAT0, LANES)
    ones = jnp.ones((s, MLA_NOPE_DIM), F32)
    ct = jnp.concatenate([ones, cos, jnp.zeros((s, LANES - MLA_QK_DIM), F32)], axis=1).T
    st = stok.T
    return ctok, stok, ct, st


def _t5_bucket_tiles(tile):
    d = jnp.arange(N_NEAR)[:, None, None]
    j = jnp.arange(tile)[None, :, None]
    i = jnp.arange(tile)[None, None, :]
    n = jnp.maximum(d * tile + i - j, 0)
    exact = T5_BUCKETS // 2
    large = exact + (jnp.log(jnp.maximum(n, 1).astype(F32) / exact)
                     / math.log(T5_MAX_DIST / exact) * (T5_BUCKETS - exact)).astype(jnp.int32)
    return jnp.where(n < exact, n, jnp.minimum(large, T5_BUCKETS - 1)).astype(jnp.int32)


def kernel(x, rel_bias, ln_mix_pre, ln_mix_post, ln_ffn_pre, ln_ffn_post, w_in, b_f, q_norm, kv_norm, w_uq,
           w_ukv, w_o, w_up, conv_w, conv_b, w_down):
    b, s, d = x.shape
    depth = w_in.shape[0]
    assert s % PROJ_ROWS == 0 and s % FFN_ROWS == 0 and s % OPROJ_ROWS == 0
    assert s // MOBA_BLOCK <= NBLK_PAD
    tables = _rope_tables(s)
    bidx = _t5_bucket_tiles(MOBA_BLOCK)
    far = rel_bias[T5_BUCKETS - 1, :]
    p = _prep_params(ln_mix_pre, ln_mix_post, ln_ffn_pre, ln_ffn_post, w_in, b_f, q_norm, kv_norm, w_uq, w_ukv, w_o,
                     w_up, conv_w, conv_b, w_down)
    for l in range(depth):
        qat, ka, vat, qft, kf, vft, qmt, km, vmt = _proj_call(x, p, l, tables, far, PROJ_ROWS)
        oa = _attn_call(qat, ka, vat, N_HEADS_MOBA // 2, (bidx, rel_bias))
        ob = _attn_call(qmt, km, vmt, N_HEADS_MLA // 2)
        oc = _attn_call(qft, kf, vft, N_HEADS_FOX // 2)
        x = _oproj_call(x, oa, ob, oc, p, l, OPROJ_ROWS)
        x = _ffn_call(x, p, l, FFN_ROWS)
    return x
```
